```python
import math
import jax
import jax.numpy as jnp
from jax import lax
import numpy as np

D_MODEL = 1024
BATCH = 8
SEQ = 2048
DEPTH = 4
DEC_BATCH = 128
DEC_SEQ = 1
PAST_LEN = 16384
PAGE_SIZE = 128

N_HYB_LAYERS = (DEPTH + 1) // 2
N_SSM_LAYERS = DEPTH // 2
CONV_WIDTH = 4
CHUNK = 64
NORM_EPS = 1e-6

GDN_HEADS = 4
GDN_KEY_DIM = 128
GDN_VAL_DIM = 128
GDN_QKV_DIM = GDN_HEADS * (2 * GDN_KEY_DIM + GDN_VAL_DIM)
GDN_OUT_DIM = GDN_HEADS * GDN_VAL_DIM

RET_HEADS = 4
RET_KEY_DIM = 64
RET_VAL_DIM = 128
RET_OUT_DIM = RET_HEADS * RET_VAL_DIM
ROPE_BASE = 10000.0

HYB_MIX_DIM = GDN_OUT_DIM + RET_OUT_DIM
HYB_IN_SIZES = (GDN_QKV_DIM, GDN_OUT_DIM, GDN_HEADS, GDN_HEADS,
                RET_HEADS * RET_KEY_DIM, RET_HEADS * RET_KEY_DIM, RET_OUT_DIM, RET_OUT_DIM)
HYB_IN_DIM = sum(HYB_IN_SIZES)

SSM_INNER = 2 * D_MODEL
SSM_HEAD_DIM = 64
SSM_HEADS = SSM_INNER // SSM_HEAD_DIM
SSM_GROUPS = 4
SSM_HEADS_PER_GROUP = SSM_HEADS // SSM_GROUPS
SSM_STATE = 128
SSM_CONV_DIM = SSM_INNER + 2 * SSM_GROUPS * SSM_STATE
SSM_IN_SIZES = (SSM_INNER, SSM_CONV_DIM, SSM_HEADS)
SSM_IN_DIM = sum(SSM_IN_SIZES)

MLP_HIDDEN = 4 * D_MODEL

kernel_name = 'hybrid_gdn_retnet_ssd_decoder_step'


def split_last(t, sizes):
    return jnp.split(t, np.cumsum(sizes)[:-1].tolist(), axis=-1)


def chunk_len(length):
    return CHUNK if length % CHUNK == 0 else length


def rms_norm(x, w):
    x32 = x.astype(jnp.float32)
    y = x32 * lax.rsqrt(jnp.mean(x32 * x32, axis=-1, keepdims=True) + NORM_EPS)
    return (y * w.astype(jnp.float32)).astype(x.dtype)


def l2_normalize(x):
    x32 = x.astype(jnp.float32)
    return x32 * lax.rsqrt(jnp.sum(x32 * x32, axis=-1, keepdims=True) + NORM_EPS)


def head_group_norm(x):
    x32 = x.astype(jnp.float32)
    xc = x32 - jnp.mean(x32, axis=-1, keepdims=True)
    return xc * lax.rsqrt(jnp.mean(xc * xc, axis=-1, keepdims=True) + NORM_EPS)


def rotary(t, pos):
    half = t.shape[-1] // 2
    inv_freq = ROPE_BASE ** (-jnp.arange(half, dtype=jnp.float32) / half)
    ang = pos.astype(jnp.float32)[:, None] * inv_freq[None, :]
    cos = jnp.cos(ang)[None, :, None, :]
    sin = jnp.sin(ang)[None, :, None, :]
    t32 = t.astype(jnp.float32)
    t1, t2 = t32[..., :half], t32[..., half:]
    return jnp.concatenate([t1 * cos - t2 * sin, t2 * cos + t1 * sin], axis=-1)


def causal_conv(x, buf, w, bias=None):
    l = x.shape[1]
    xp = jnp.concatenate([buf.astype(x.dtype), x], axis=1)
    out = xp[:, 0:l] * w[0]
    for i in range(1, w.shape[0]):
        out = out + xp[:, i:i + l] * w[i]
    if bias is not None:
        out = out + bias
    return out, xp[:, l:]


def decay_linear_attention(q, k, v, g, s0, chunk):
    dtype = s0.dtype
    b, l = q.shape[:2]
    out_shape = v.shape

    def blocks(t):
        return jnp.moveaxis(t.astype(jnp.float32).reshape(b, l // chunk, chunk, *t.shape[2:]), 1, 0)

    idx = jnp.arange(chunk)
    causal = (idx[:, None] >= idx[None, :])[None, :, :, None, None]

    def step(s, blk):
        qc, kc, vc, gc = blk
        gcum = jnp.cumsum(gc, axis=1)
        diff = gcum[:, :, None] - gcum[:, None, :]
        decay = jnp.where(causal, jnp.exp(jnp.where(causal, diff, 0.0)), 0.0)
        scores = jnp.einsum('bigk,bjgk->bijg', qc, kc)
        y_intra = jnp.einsum('bijgr,bjgrv->bigrv', scores[..., None] * decay, vc)
        y_inter = jnp.einsum('bigk,bgrkv->bigrv', qc, s) * jnp.exp(gcum)[..., None]
        g_end = gcum[:, -1]
        s_new = s * jnp.exp(g_end)[..., None, None] + jnp.einsum(
            'bjgk,bjgrv->bgrkv', kc, vc * jnp.exp(g_end[:, None] - gcum)[..., None])
        return s_new, y_intra + y_inter

    s_final, ys = lax.scan(step, s0.astype(jnp.float32), (blocks(q), blocks(k), blocks(v), blocks(g)))
    y = jnp.moveaxis(ys, 0, 1).reshape(out_shape)
    return y, s_final.astype(dtype)


def gated_delta_rule(q, k, v, g, beta, s0, chunk):
    dtype = s0.dtype
    b, l, h, _ = q.shape
    dv = v.shape[-1]

    def blocks(t):
        t = t.astype(jnp.float32).reshape(b, l // chunk, chunk, *t.shape[2:])
        return jnp.moveaxis(jnp.moveaxis(t, 1, 0), 2, 3)

    idx = jnp.arange(chunk)
    causal = idx[:, None] >= idx[None, :]
    strict = idx[:, None] > idx[None, :]
    eye = jnp.eye(chunk, dtype=jnp.float32)

    def step(s, blk):
        qc, kc, vc, gc, bc = blk
        gcum = jnp.cumsum(gc, axis=-1)
        diff = gcum[..., :, None] - gcum[..., None, :]
        decay = jnp.where(causal, jnp.exp(jnp.where(causal, diff, 0.0)), 0.0)
        kk = jnp.einsum('bhik,bhjk->bhij', kc, kc)
        lower = jnp.where(strict, bc[..., :, None] * kk * decay, 0.0)
        rhs = jnp.concatenate([vc * bc[..., None], kc * (bc * jnp.exp(gcum))[..., None]], axis=-1)
        sol = lax.linalg.triangular_solve(eye + lower, rhs, left_side=True, lower=True, unit_diagonal=True)
        u, w = sol[..., :dv], sol[..., dv:]
        delta = u - jnp.einsum('bhik,bhkv->bhiv', w, s)
        qk = jnp.einsum('bhik,bhjk->bhij', qc, kc) * decay
        y = (jnp.einsum('bhik,bhkv->bhiv', qc * jnp.exp(gcum)[..., None], s)
             + jnp.einsum('bhij,bhjv->bhiv', qk, delta))
        g_end = gcum[..., -1:]
        s_new = s * jnp.exp(g_end)[..., None] + jnp.einsum(
            'bhjk,bhjv->bhkv', kc * jnp.exp(g_end - gcum)[..., None], delta)
        return s_new, y

    s_final, ys = lax.scan(step, s0.astype(jnp.float32),
                           (blocks(q), blocks(k), blocks(v), blocks(g), blocks(beta)))
    y = jnp.swapaxes(jnp.moveaxis(ys, 0, 1), 2, 3).reshape(b, l, h, dv)
    return y, s_final.astype(dtype)


def hybrid_mixer(h, pos, s_gdn, c_gdn, s_ret, w_in, gdn_conv_w, gdn_a_log, gdn_dt_bias,
                 gdn_norm_w, ret_norm_w, w_out):
    b, l, _ = h.shape
    chunk = chunk_len(l)
    proj = jnp.einsum('bld,de->ble', h, w_in)
    qkv_a, z_a, beta_raw, a_raw, q_b, k_b, v_b, gate_b = split_last(proj, HYB_IN_SIZES)

    qkv_a, c_gdn_new = causal_conv(qkv_a, c_gdn, gdn_conv_w)
    qkv_a = jax.nn.silu(qkv_a)
    q_a, k_a, v_a = split_last(qkv_a, (GDN_HEADS * GDN_KEY_DIM, GDN_HEADS * GDN_KEY_DIM, GDN_OUT_DIM))
    q_a = l2_normalize(q_a.reshape(b, l, GDN_HEADS, GDN_KEY_DIM)) * GDN_KEY_DIM ** -0.5
    k_a = l2_normalize(k_a.reshape(b, l, GDN_HEADS, GDN_KEY_DIM))
    v_a = v_a.reshape(b, l, GDN_HEADS, GDN_VAL_DIM)
    beta = jax.nn.sigmoid(beta_raw.astype(jnp.float32))
    g_a = -jnp.exp(gdn_a_log.astype(jnp.float32)) * jax.nn.softplus(a_raw.astype(jnp.float32) + gdn_dt_bias)
    o_a, s_gdn_new = gated_delta_rule(q_a, k_a, v_a, g_a, beta, s_gdn, chunk)
    o_a = rms_norm(o_a, gdn_norm_w) * jax.nn.silu(z_a.astype(jnp.float32).reshape(b, l, GDN_HEADS, GDN_VAL_DIM))

    q_b = rotary(q_b.reshape(b, l, RET_HEADS, RET_KEY_DIM), pos)
    k_b = rotary(k_b.reshape(b, l, RET_HEADS, RET_KEY_DIM), pos) * RET_KEY_DIM ** -0.5
    v_b = v_b.reshape(b, l, RET_HEADS, 1, RET_VAL_DIM)
    log_gamma = jnp.log(1.0 - jnp.exp2(-5.0 - jnp.arange(RET_HEADS, dtype=jnp.float32)))
    g_b = jnp.broadcast_to(log_gamma[:, None], (b, l, RET_HEADS, 1))
    o_b, s_ret_new = decay_linear_attention(q_b, k_b, v_b, g_b, s_ret[:, :, None], chunk)
    o_b = head_group_norm(o_b.reshape(b, l, RET_HEADS, RET_VAL_DIM)).reshape(b, l, RET_OUT_DIM)
    o_b = o_b * ret_norm_w * jax.nn.silu(gate_b.astype(jnp.float32))

    mixed = jnp.concatenate([o_a.reshape(b, l, GDN_OUT_DIM), o_b], axis=-1).astype(h.dtype)
    out = jnp.einsum('ble,ed->bld', mixed, w_out)
    return out, s_gdn_new, c_gdn_new, s_ret_new[:, :, 0]


def ssd_mixer(h, s_ssm, c_ssm, w_in, conv_w, conv_b, dt_bias, a_log, d_skip, norm_w, w_out):
    b, l, _ = h.shape
    chunk = chunk_len(l)
    proj = jnp.einsum('bld,de->ble', h, w_in)
    z, xbc, dt_raw = split_last(proj, SSM_IN_SIZES)
    xbc, c_ssm_new = causal_conv(xbc, c_ssm, conv_w, conv_b)
    xbc = jax.nn.silu(xbc)
    xs, b_mat, c_mat = split_last(xbc, (SSM_INNER, SSM_GROUPS * SSM_STATE, SSM_GROUPS * SSM_STATE))
    dt = jax.nn.softplus(dt_raw.astype(jnp.float32) + dt_bias).reshape(b, l, SSM_GROUPS, SSM_HEADS_PER_GROUP)
    a = -jnp.exp(a_log.astype(jnp.float32)).reshape(SSM_GROUPS, SSM_HEADS_PER_GROUP)
    xh = xs.astype(jnp.float32).reshape(b, l, SSM_GROUPS, SSM_HEADS_PER_GROUP, SSM_HEAD_DIM)
    s0 = s_ssm.reshape(b, SSM_GROUPS, SSM_HEADS_PER_GROUP, SSM_STATE, SSM_HEAD_DIM)
    y, s_new = decay_linear_attention(
        c_mat.reshape(b, l, SSM_GROUPS, SSM_STATE), b_mat.reshape(b, l, SSM_GROUPS, SSM_STATE),
        xh * dt[..., None], dt * a, s0, chunk)
    y = y + d_skip.reshape(SSM_GROUPS, SSM_HEADS_PER_GROUP)[..., None] * xh
    y = y.reshape(b, l, SSM_INNER) * jax.nn.silu(z.astype(jnp.float32))
    y = rms_norm(y.reshape(b, l, SSM_GROUPS, SSM_INNER // SSM_GROUPS),
                 norm_w.reshape(SSM_GROUPS, SSM_INNER // SSM_GROUPS)).reshape(b, l, SSM_INNER)
    out = jnp.einsum('ble,ed->bld', y.astype(h.dtype), w_out)
    return out, s_new.reshape(b, SSM_HEADS, SSM_STATE, SSM_HEAD_DIM), c_ssm_new


def squared_relu_mlp(h, w1, w2):
    a = jax.nn.relu(jnp.einsum('bld,df->blf', h, w1))
    return jnp.einsum('blf,fd->bld', a * a, w2)


def trunk(x, pos, s_gdn, c_gdn, s_ret, s_ssm, c_ssm, norm_mix, norm_mlp, norm_final,
          w_in_hyb, gdn_conv_w, gdn_a_log, gdn_dt_bias, gdn_norm_w, ret_norm_w, w_out_hyb,
          w_in_ssm, ssm_conv_w, ssm_conv_b, ssm_dt_bias, ssm_a_log, ssm_d, ssm_norm_w, w_out_ssm,
          mlp_w1, mlp_w2):
    new_gdn, new_gdn_conv, new_ret, new_ssm, new_ssm_conv = [], [], [], [], []
    for layer in range(DEPTH):
        i = layer // 2
        h = rms_norm(x, norm_mix[layer])
        if layer % 2 == 0:
            out, sg, cg, sr = hybrid_mixer(h, pos, s_gdn[i], c_gdn[i], s_ret[i], w_in_hyb[i], gdn_conv_w[i],
                                           gdn_a_log[i], gdn_dt_bias[i], gdn_norm_w[i], ret_norm_w[i],
                                           w_out_hyb[i])
            new_gdn.append(sg)
            new_gdn_conv.append(cg)
            new_ret.append(sr)
        else:
            out, ss, cs = ssd_mixer(h, s_ssm[i], c_ssm[i], w_in_ssm[i], ssm_conv_w[i], ssm_conv_b[i],
                                    ssm_dt_bias[i], ssm_a_log[i], ssm_d[i], ssm_norm_w[i], w_out_ssm[i])
            new_ssm.append(ss)
            new_ssm_conv.append(cs)
        x = x + out
        x = x + squared_relu_mlp(rms_norm(x, norm_mlp[layer]), mlp_w1[layer], mlp_w2[layer])
    return (rms_norm(x, norm_final), jnp.stack(new_gdn), jnp.stack(new_gdn_conv), jnp.stack(new_ret),
            jnp.stack(new_ssm), jnp.stack(new_ssm_conv))


def setup_inputs(seed: int = 0) -> dict:
    key = jax.random.key(seed)
    ks = jax.random.split(key, 32)
    f32 = jnp.float32

    def normal(k, shape, scale):
        return scale * jax.random.normal(k, shape, f32)

    def gain(k, shape):
        return 1.0 + 0.01 * jax.random.normal(k, shape, f32)

    def dt_bias(k, shape):
        dt = jnp.exp(jax.random.uniform(k, shape, f32, math.log(1e-3), math.log(1e-1)))
        return dt + jnp.log(-jnp.expm1(-dt))

    def a_log(k, shape):
        return jnp.log(jax.random.uniform(k, shape, f32, 1.0, 16.0))

    return {
        'x_prompt': normal(ks[0], (BATCH, SEQ, D_MODEL), 1.0),
        'x_sample': normal(ks[1], (DEC_BATCH, DEC_SEQ, D_MODEL), 1.0),
        'state_gdn': normal(ks[2], (N_HYB_LAYERS, DEC_BATCH, GDN_HEADS, GDN_KEY_DIM, GDN_VAL_DIM), 0.1),
        'state_gdn_conv': normal(ks[3], (N_HYB_LAYERS, DEC_BATCH, CONV_WIDTH - 1, GDN_QKV_DIM), 1.0),
        'state_ret': normal(ks[4], (N_HYB_LAYERS, DEC_BATCH, RET_HEADS, RET_KEY_DIM, RET_VAL_DIM), 0.5),
        'state_ssm': normal(ks[5], (N_SSM_LAYERS, DEC_BATCH, SSM_HEADS, SSM_STATE, SSM_HEAD_DIM), 0.1),
        'state_ssm_conv': normal(ks[6], (N_SSM_LAYERS, DEC_BATCH, CONV_WIDTH - 1, SSM_CONV_DIM), 1.0),
        'norm_mix': gain(ks[7], (DEPTH, D_MODEL)),
        'norm_mlp': gain(ks[8], (DEPTH, D_MODEL)),
        'norm_final': gain(ks[9], (D_MODEL,)),
        'w_in_hyb': normal(ks[10], (N_HYB_LAYERS, D_MODEL, HYB_IN_DIM), D_MODEL ** -0.5),
        'gdn_conv_w': normal(ks[11], (N_HYB_LAYERS, CONV_WIDTH, GDN_QKV_DIM), CONV_WIDTH ** -0.5),
        'gdn_a_log': a_log(ks[12], (N_HYB_LAYERS, GDN_HEADS)),
        'gdn_dt_bias': dt_bias(ks[13], (N_HYB_LAYERS, GDN_HEADS)),
        'gdn_norm_w': gain(ks[14], (N_HYB_LAYERS, GDN_VAL_DIM)),
        'ret_norm_w': gain(ks[15], (N_HYB_LAYERS, RET_OUT_DIM)),
        'w_out_hyb': normal(ks[16], (N_HYB_LAYERS, HYB_MIX_DIM, D_MODEL), HYB_MIX_DIM ** -0.5),
        'w_in_ssm': normal(ks[17], (N_SSM_LAYERS, D_MODEL, SSM_IN_DIM), D_MODEL ** -0.5),
        'ssm_conv_w': normal(ks[18], (N_SSM_LAYERS, CONV_WIDTH, SSM_CONV_DIM), CONV_WIDTH ** -0.5),
        'ssm_conv_b': normal(ks[19], (N_SSM_LAYERS, SSM_CONV_DIM), 0.02),
        'ssm_dt_bias': dt_bias(ks[20], (N_SSM_LAYERS, SSM_HEADS)),
        'ssm_a_log': a_log(ks[21], (N_SSM_LAYERS, SSM_HEADS)),
        'ssm_d': 1.0 + 0.1 * jax.random.normal(ks[22], (N_SSM_LAYERS, SSM_HEADS), f32),
        'ssm_norm_w': gain(ks[23], (N_SSM_LAYERS, SSM_INNER)),
        'w_out_ssm': normal(ks[24], (N_SSM_LAYERS, SSM_INNER, D_MODEL), SSM_INNER ** -0.5),
        'mlp_w1': normal(ks[25], (DEPTH, D_MODEL, MLP_HIDDEN), D_MODEL ** -0.5),
        'mlp_w2': normal(ks[26], (DEPTH, MLP_HIDDEN, D_MODEL), 0.5 * MLP_HIDDEN ** -0.5),
    }


def reference(x_prompt, x_sample, state_gdn, state_gdn_conv, state_ret, state_ssm, state_ssm_conv,
              norm_mix, norm_mlp, norm_final, w_in_hyb, gdn_conv_w, gdn_a_log, gdn_dt_bias, gdn_norm_w,
              ret_norm_w, w_out_hyb, w_in_ssm, ssm_conv_w, ssm_conv_b, ssm_dt_bias, ssm_a_log, ssm_d,
              ssm_norm_w, w_out_ssm, mlp_w1, mlp_w2):
    bp, lp, _ = x_prompt.shape
    ls = x_sample.shape[1]
    dt = x_prompt.dtype
    z_gdn = jnp.zeros((N_HYB_LAYERS, bp, GDN_HEADS, GDN_KEY_DIM, GDN_VAL_DIM), dt)
    z_gdn_conv = jnp.zeros((N_HYB_LAYERS, bp, CONV_WIDTH - 1, GDN_QKV_DIM), dt)
    z_ret = jnp.zeros((N_HYB_LAYERS, bp, RET_HEADS, RET_KEY_DIM, RET_VAL_DIM), dt)
    z_ssm = jnp.zeros((N_SSM_LAYERS, bp, SSM_HEADS, SSM_STATE, SSM_HEAD_DIM), dt)
    z_ssm_conv = jnp.zeros((N_SSM_LAYERS, bp, CONV_WIDTH - 1, SSM_CONV_DIM), dt)
    pos_prompt = jnp.arange(lp, dtype=jnp.int32)
    pos_sample = PAST_LEN + jnp.arange(ls, dtype=jnp.int32)

    y_prompt, p_gdn, p_gdn_conv, p_ret, p_ssm, p_ssm_conv = trunk(
        x_prompt, pos_prompt, z_gdn, z_gdn_conv, z_ret, z_ssm, z_ssm_conv, norm_mix, norm_mlp, norm_final,
        w_in_hyb, gdn_conv_w, gdn_a_log, gdn_dt_bias, gdn_norm_w, ret_norm_w, w_out_hyb,
        w_in_ssm, ssm_conv_w, ssm_conv_b, ssm_dt_bias, ssm_a_log, ssm_d, ssm_norm_w, w_out_ssm,
        mlp_w1, mlp_w2)
    y_sample, s_gdn, s_gdn_conv, s_ret, s_ssm, s_ssm_conv = trunk(
        x_sample, pos_sample, state_gdn, state_gdn_conv, state_ret, state_ssm, state_ssm_conv,
        norm_mix, norm_mlp, norm_final,
        w_in_hyb, gdn_conv_w, gdn_a_log, gdn_dt_bias, gdn_norm_w, ret_norm_w, w_out_hyb,
        w_in_ssm, ssm_conv_w, ssm_conv_b, ssm_dt_bias, ssm_a_log, ssm_d, ssm_norm_w, w_out_ssm,
        mlp_w1, mlp_w2)
    return (y_prompt, y_sample, p_gdn, p_gdn_conv, p_ret, p_ssm, p_ssm_conv,
            s_gdn, s_gdn_conv, s_ret, s_ssm, s_ssm_conv)
```

```python
import functools
import math

import jax
import jax.numpy as jnp
from jax import lax
from jax.experimental import pallas as pl
from jax.experimental.pallas import tpu as pltpu

F32 = jnp.float32
BF16 = jnp.bfloat16

D_MODEL = 1024
DEPTH = 4
CONV_WIDTH = 4
NORM_EPS = 1e-6
PAST_LEN = 16384

GDN_HEADS = 4
GDN_DIM = 128
GDN_QKV_DIM = 3 * GDN_HEADS * GDN_DIM
GDN_OUT_DIM = GDN_HEADS * GDN_DIM

RET_HEADS = 4
RET_KEY_DIM = 64
RET_VAL_DIM = 128
RET_QK_DIM = RET_HEADS * RET_KEY_DIM
RET_OUT_DIM = RET_HEADS * RET_VAL_DIM
ROPE_BASE = 10000.0

SSM_INNER = 2 * D_MODEL
SSM_HEAD_DIM = 64
SSM_HEADS = SSM_INNER // SSM_HEAD_DIM
SSM_GROUPS = 4
SSM_HPG = SSM_HEADS // SSM_GROUPS
SSM_STATE = 128
SSM_GROUP_COLS = SSM_HPG * SSM_HEAD_DIM
SSM_BC_DIM = SSM_GROUPS * SSM_STATE
SSM_CONV_DIM = SSM_INNER + 2 * SSM_BC_DIM
MLP_HIDDEN = 4 * D_MODEL

LANES = 128
SUBLANES = 8

HYB_OFF_QKV = 0
HYB_OFF_Z = GDN_QKV_DIM
HYB_OFF_RQ = HYB_OFF_Z + GDN_OUT_DIM
HYB_OFF_RK = HYB_OFF_RQ + RET_QK_DIM
HYB_OFF_RV = HYB_OFF_RK + RET_QK_DIM
HYB_OFF_RG = HYB_OFF_RV + RET_OUT_DIM
HYB_OFF_BA = HYB_OFF_RG + RET_OUT_DIM
HYB_N = 3840
SSM_OFF_Z = 0
SSM_OFF_X = SSM_INNER
SSM_OFF_B = SSM_OFF_X + SSM_INNER
SSM_OFF_C = SSM_OFF_B + SSM_BC_DIM
SSM_OFF_DT = SSM_OFF_C + SSM_BC_DIM
SSM_N = 5376
PROJ_TN = 768

VMEM_LIMIT = 56 * 1024 * 1024

_NN = (((1,), (0,)), ((), ()))
_NT = (((1,), (1,)), ((), ()))
_TN = (((0,), (0,)), ((), ()))


def _dot(a, b, dims=_NN):
    return lax.dot_general(a.astype(BF16), b.astype(BF16), dims, preferred_element_type=F32)


def _dot_f32(a, b, dims=_NN):
    return lax.dot_general(a, b, dims, precision=lax.Precision.HIGHEST, preferred_element_type=F32)


def _sigmoid(x):
    return 1.0 / (1.0 + jnp.exp(-x))


def _silu(x):
    return x * _sigmoid(x)


def _softplus(x):
    return jnp.maximum(x, 0.0) + jnp.log(1.0 + jnp.exp(-jnp.abs(x)))


def _iota(shape, dim):
    return lax.broadcasted_iota(jnp.int32, shape, dim)


def _eye(n):
    return (_iota((n, n), 0) == _iota((n, n), 1)).astype(F32)


def _transpose(x):
    return _dot_f32(_eye(x.shape[1]), x, _NT)


def _cparams(sem):
    return pltpu.CompilerParams(dimension_semantics=sem, vmem_limit_bytes=VMEM_LIMIT)


def _resident(shape):
    nd = len(shape)
    return pl.BlockSpec(shape, lambda *_: (0,) * nd, pipeline_mode=pl.Buffered(1))


def _rms_rows(x, gain):
    return x * lax.rsqrt(jnp.mean(x * x, axis=-1, keepdims=True) + NORM_EPS) * gain


def _norm_matmul_kernel(x_ref, g_ref, w_ref, o_ref, *, tn):
    xn = _rms_rows(x_ref[...], g_ref[...]).astype(BF16)
    for j in range(w_ref.shape[1] // tn):
        o_ref[:, j * tn:(j + 1) * tn] = jnp.dot(xn, w_ref[:, j * tn:(j + 1) * tn],
                                                preferred_element_type=F32)


def norm_matmul(x, gain, w, tm):
    m, d = x.shape
    n = w.shape[1]
    return pl.pallas_call(
        functools.partial(_norm_matmul_kernel, tn=PROJ_TN),
        grid=(m // tm,),
        in_specs=[pl.BlockSpec((tm, d), lambda i: (i, 0)), _resident((1, d)), _resident((d, n))],
        out_specs=pl.BlockSpec((tm, n), lambda i: (i, 0)),
        out_shape=jax.ShapeDtypeStruct((m, n), F32),
        compiler_params=_cparams(("parallel",)),
        name="norm_matmul",
    )(x, gain.reshape(1, d), w)


def _out_proj_kernel(*refs, n_in):
    a_refs, w_refs, r_ref, o_ref = refs[:n_in], refs[n_in:2 * n_in], refs[2 * n_in], refs[2 * n_in + 1]
    acc = r_ref[...]
    for a_ref, w_ref in zip(a_refs, w_refs):
        acc = acc + jnp.dot(a_ref[...], w_ref[...], preferred_element_type=F32)
    o_ref[...] = acc


def out_proj_residual(acts, ws, res, tm):
    m, d = res.shape
    n_in = len(acts)
    in_specs = ([pl.BlockSpec((tm, a.shape[1]), lambda i: (i, 0)) for a in acts]
                + [_resident(w.shape) for w in ws]
                + [pl.BlockSpec((tm, d), lambda i: (i, 0))])
    return pl.pallas_call(
        functools.partial(_out_proj_kernel, n_in=n_in),
        grid=(m // tm,),
        in_specs=in_specs,
        out_specs=pl.BlockSpec((tm, d), lambda i: (i, 0)),
        out_shape=jax.ShapeDtypeStruct((m, d), F32),
        compiler_params=_cparams(("parallel",)),
        name="out_proj",
    )(*acts, *ws, res)


def _mlp_kernel(x_ref, g_ref, w1_ref, w2_ref, gf_ref, o_ref, *, tf, final_norm):
    x = x_ref[...]
    xn = _rms_rows(x, g_ref[...]).astype(BF16)
    acc = x
    for j in range(w1_ref.shape[1] // tf):
        h = jnp.dot(xn, w1_ref[:, j * tf:(j + 1) * tf], preferred_element_type=F32)
        h = jnp.maximum(h, 0.0)
        acc = acc + jnp.dot((h * h).astype(BF16), w2_ref[j * tf:(j + 1) * tf, :],
                            preferred_element_type=F32)
    if final_norm:
        acc = _rms_rows(acc, gf_ref[...])
    o_ref[...] = acc


def mlp_residual(x, gain, w1, w2, final_gain, final_norm, tm):
    m, d = x.shape
    f = w1.shape[1]
    return pl.pallas_call(
        functools.partial(_mlp_kernel, tf=1024, final_norm=final_norm),
        grid=(m // tm,),
        in_specs=[pl.BlockSpec((tm, d), lambda i: (i, 0)), _resident((1, d)), _resident((d, f)),
                  _resident((f, d)), _resident((1, d))],
        out_specs=pl.BlockSpec((tm, d), lambda i: (i, 0)),
        out_shape=jax.ShapeDtypeStruct((m, d), F32),
        compiler_params=_cparams(("parallel",)),
        name="mlp",
    )(x, gain.reshape(1, d), w1, w2, final_gain.reshape(1, d))


def _causal_conv_chunk(x, tail, w, bias=None):
    c = x.shape[0]
    acc = x * w[CONV_WIDTH - 1:CONV_WIDTH, :]
    row = _iota((SUBLANES, x.shape[1]), 0)
    for k in range(1, CONV_WIDTH):
        xr = pltpu.roll(x, k, 0)
        tr = pltpu.roll(tail, k, 0)
        head = jnp.where(row < k, tr, xr[0:SUBLANES])
        shifted = jnp.concatenate([head, xr[SUBLANES:c]], axis=0)
        acc = acc + shifted * w[CONV_WIDTH - 1 - k:CONV_WIDTH - k, :]
    if bias is not None:
        acc = acc + bias
    return acc


def _cumsum_rows(g):
    c = g.shape[0]
    tri = (_iota((c, c), 0) >= _iota((c, c), 1)).astype(F32)
    return _dot_f32(tri, g)


def _decay_matrix(gc_col, gc_row):
    c = gc_col.shape[0]
    causal = _iota((c, c), 0) >= _iota((c, c), 1)
    return jnp.where(causal, jnp.exp(jnp.where(causal, gc_col - gc_row, 0.0)), 0.0)


def _unit_lower_inverse(n_strict):
    c = n_strict.shape[0]
    ri, ci = _iota((c, c), 0), _iota((c, c), 1)
    eye = (ri == ci).astype(F32)
    blk = 16
    same = (ri // blk) == (ci // blk)
    a = jnp.where(same, -n_strict, 0.0)
    inv = eye + a
    p = a
    for _ in range(3):
        p = _dot_f32(p, p)
        inv = inv + _dot_f32(inv, p)
    while blk < c:
        off = jnp.where(((ri // (2 * blk)) == (ci // (2 * blk))) & ((ri // blk) != (ci // blk)),
                        n_strict, 0.0)
        inv = inv - _dot_f32(inv, _dot_f32(off, inv))
        blk *= 2
    return inv


def _gdn_prompt_kernel(qkv_ref, z_ref, ba_ref, cw_ref, alog_ref, dtb_ref, nw_ref,
                       o_ref, s_out_ref, s_ref, tail_ref):
    c_idx = pl.program_id(1)

    @pl.when(c_idx == 0)
    def _():
        s_ref[...] = jnp.zeros_like(s_ref)
        tail_ref[...] = jnp.zeros_like(tail_ref)

    x = qkv_ref[...]
    c = x.shape[0]
    conv = _silu(_causal_conv_chunk(x, tail_ref[...], cw_ref[...]))
    tail_ref[...] = x[c - SUBLANES:c]

    ba = ba_ref[...]
    beta = _sigmoid(ba)
    g = -jnp.exp(alog_ref[...]) * _softplus(ba + dtb_ref[...])
    gcum = _cumsum_rows(g)
    gcum_t = _transpose(gcum)
    ri, ci = _iota((c, c), 0), _iota((c, c), 1)
    strict = ri > ci

    hd = GDN_DIM
    for h in range(GDN_HEADS):
        q = conv[:, h * hd:(h + 1) * hd]
        k = conv[:, GDN_OUT_DIM + h * hd:GDN_OUT_DIM + (h + 1) * hd]
        v = conv[:, 2 * GDN_OUT_DIM + h * hd:2 * GDN_OUT_DIM + (h + 1) * hd]
        q = q * lax.rsqrt(jnp.sum(q * q, axis=-1, keepdims=True) + NORM_EPS) * (hd ** -0.5)
        k = k * lax.rsqrt(jnp.sum(k * k, axis=-1, keepdims=True) + NORM_EPS)
        b_col = beta[:, h:h + 1]
        gc_col = gcum[:, GDN_HEADS + h:GDN_HEADS + h + 1]
        gc_row = gcum_t[GDN_HEADS + h:GDN_HEADS + h + 1, :]
        g_end = gcum[c - 1:c, GDN_HEADS + h:GDN_HEADS + h + 1]
        decay = _decay_matrix(gc_col, gc_row)
        kk = _dot(k, k, _NT)
        t_inv = _unit_lower_inverse(jnp.where(strict, b_col * kk * decay, 0.0))
        rhs = jnp.concatenate([v * b_col, k * (b_col * jnp.exp(gc_col))], axis=1)
        sol = _dot_f32(t_inv, rhs)
        u, w = sol[:, :hd], sol[:, hd:]
        s = s_ref[h]
        delta = u - _dot(w, s)
        qk = _dot(q, k, _NT) * decay
        y = _dot(q * jnp.exp(gc_col), s) + _dot(qk, delta)
        s_ref[h] = s * jnp.exp(g_end) + _dot(k * jnp.exp(g_end - gc_col), delta, _TN)
        y = _rms_rows(y, nw_ref[...]) * _silu(z_ref[:, h * hd:(h + 1) * hd])
        o_ref[:, h * hd:(h + 1) * hd] = y.astype(o_ref.dtype)

    @pl.when(c_idx == pl.num_programs(1) - 1)
    def _():
        s_out_ref[...] = s_ref[...]


def gdn_prompt(proj, conv_w, alog_row, dtb_row, norm_w, chunk=64):
    b, l, _ = proj.shape
    nc = l // chunk
    col = lambda width, off: pl.BlockSpec((None, chunk, width), lambda i, j: (i, j, off // width))
    return pl.pallas_call(
        _gdn_prompt_kernel,
        grid=(b, nc),
        in_specs=[col(GDN_QKV_DIM, HYB_OFF_QKV), col(GDN_OUT_DIM, HYB_OFF_Z), col(LANES, HYB_OFF_BA),
                  _resident((CONV_WIDTH, GDN_QKV_DIM)), _resident((1, LANES)), _resident((1, LANES)),
                  _resident((1, GDN_DIM))],
        out_specs=[pl.BlockSpec((None, chunk, GDN_OUT_DIM), lambda i, j: (i, j, 0)),
                   pl.BlockSpec((None, GDN_HEADS, GDN_DIM, GDN_DIM), lambda i, j: (i, 0, 0, 0))],
        out_shape=[jax.ShapeDtypeStruct((b, l, GDN_OUT_DIM), BF16),
                   jax.ShapeDtypeStruct((b, GDN_HEADS, GDN_DIM, GDN_DIM), F32)],
        scratch_shapes=[pltpu.VMEM((GDN_HEADS, GDN_DIM, GDN_DIM), F32),
                        pltpu.VMEM((SUBLANES, GDN_QKV_DIM), F32)],
        compiler_params=_cparams(("parallel", "arbitrary")),
        name="gdn_prompt",
    )(proj, proj, proj, conv_w, alog_row, dtb_row, norm_w.reshape(1, GDN_DIM))


def _ret_log_gamma(h):
    return math.log(1.0 - 2.0 ** (-5.0 - h))


def _rotary(t, cos_f, sin_s):
    n = t.shape[1]
    half = RET_KEY_DIM // 2
    first = (_iota(t.shape, 1) % RET_KEY_DIM) < half
    swapped = jnp.where(first, pltpu.roll(t, n - half, 1), pltpu.roll(t, half, 1))
    return t * cos_f + swapped * sin_s


def _group_norm_rows(x):
    xc = x - jnp.mean(x, axis=-1, keepdims=True)
    return xc * lax.rsqrt(jnp.mean(xc * xc, axis=-1, keepdims=True) + NORM_EPS)


def _ret_prompt_kernel(q_ref, k_ref, v_ref, gate_ref, cos_ref, sin_ref, nw_ref,
                       o_ref, s_out_ref, s_ref):
    c_idx = pl.program_id(1)

    @pl.when(c_idx == 0)
    def _():
        s_ref[...] = jnp.zeros_like(s_ref)

    q = _rotary(q_ref[...], cos_ref[...], sin_ref[...])
    k = _rotary(k_ref[...], cos_ref[...], sin_ref[...]) * (RET_KEY_DIM ** -0.5)
    c = q.shape[0]
    ri, ci = _iota((c, c), 0), _iota((c, c), 1)
    causal = ri >= ci
    pos_col = _iota((c, 1), 0).astype(F32)
    kd, vd = RET_KEY_DIM, RET_VAL_DIM
    for h in range(RET_HEADS):
        lg = _ret_log_gamma(h)
        decay = jnp.where(causal, jnp.exp(jnp.where(causal, (ri - ci).astype(F32) * lg, 0.0)), 0.0)
        qh = q[:, h * kd:(h + 1) * kd]
        kh = k[:, h * kd:(h + 1) * kd]
        vh = v_ref[:, h * vd:(h + 1) * vd]
        s = s_ref[h]
        y = _dot(_dot(qh, kh, _NT) * decay, vh) + _dot(qh, s) * jnp.exp((pos_col + 1.0) * lg)
        s_ref[h] = s * math.exp(c * lg) + _dot(kh * jnp.exp((c - 1.0 - pos_col) * lg), vh, _TN)
        y = (_group_norm_rows(y) * nw_ref[:, h * vd:(h + 1) * vd]
             * _silu(gate_ref[:, h * vd:(h + 1) * vd]))
        o_ref[:, h * vd:(h + 1) * vd] = y.astype(o_ref.dtype)

    @pl.when(c_idx == pl.num_programs(1) - 1)
    def _():
        s_out_ref[...] = s_ref[...]


def ret_prompt(proj, cos_f, sin_s, norm_w, chunk=128):
    b, l, _ = proj.shape
    nc = l // chunk
    col = lambda width, off: pl.BlockSpec((None, chunk, width), lambda i, j: (i, j, off // width))
    tab = pl.BlockSpec((chunk, RET_QK_DIM), lambda i, j: (j, 0))
    return pl.pallas_call(
        _ret_prompt_kernel,
        grid=(b, nc),
        in_specs=[col(RET_QK_DIM, HYB_OFF_RQ), col(RET_QK_DIM, HYB_OFF_RK), col(RET_OUT_DIM, HYB_OFF_RV),
                  col(RET_OUT_DIM, HYB_OFF_RG), tab, tab, _resident((1, RET_OUT_DIM))],
        out_specs=[pl.BlockSpec((None, chunk, RET_OUT_DIM), lambda i, j: (i, j, 0)),
                   pl.BlockSpec((None, RET_HEADS, RET_KEY_DIM, RET_VAL_DIM), lambda i, j: (i, 0, 0, 0))],
        out_shape=[jax.ShapeDtypeStruct((b, l, RET_OUT_DIM), BF16),
                   jax.ShapeDtypeStruct((b, RET_HEADS, RET_KEY_DIM, RET_VAL_DIM), F32)],
        scratch_shapes=[pltpu.VMEM((RET_HEADS, RET_KEY_DIM, RET_VAL_DIM), F32)],
        compiler_params=_cparams(("parallel", "arbitrary")),
        name="ret_prompt",
    )(proj, proj, proj, proj, cos_f, sin_s, norm_w.reshape(1, RET_OUT_DIM))


def _ssd_prompt_kernel(z_ref, x_ref, b_ref, c_ref, dt_ref, cwx_ref, cwb_ref, cwc_ref,
                       cbx_ref, cbb_ref, cbc_ref, dtb_ref, alog_ref, dskip_ref, nw_ref,
                       o_ref, s_out_ref, s_ref, tx_ref, tb_ref, tc_ref):
    grp = pl.program_id(1)
    c_idx = pl.program_id(2)

    @pl.when(c_idx == 0)
    def _():
        s_ref[...] = jnp.zeros_like(s_ref)
        tx_ref[...] = jnp.zeros_like(tx_ref)
        tb_ref[...] = jnp.zeros_like(tb_ref)
        tc_ref[...] = jnp.zeros_like(tc_ref)

    xr, br, cr = x_ref[...], b_ref[...], c_ref[...]
    c = xr.shape[0]
    x = _silu(_causal_conv_chunk(xr, tx_ref[...], cwx_ref[...], cbx_ref[...]))
    bm = _silu(_causal_conv_chunk(br, tb_ref[...], cwb_ref[...], cbb_ref[...]))
    cm = _silu(_causal_conv_chunk(cr, tc_ref[...], cwc_ref[...], cbc_ref[...]))
    tx_ref[...] = xr[c - SUBLANES:c]
    tb_ref[...] = br[c - SUBLANES:c]
    tc_ref[...] = cr[c - SUBLANES:c]

    dt = _softplus(dt_ref[...] + dtb_ref[...])
    g = dt * (-jnp.exp(alog_ref[...]))
    gcum = _cumsum_rows(g)
    gcum_t = _transpose(gcum)
    scores = _dot(cm, bm, _NT)
    s = s_ref[...]
    y_inter = _dot(cm, s)
    hd = SSM_HEAD_DIM
    lane_head = _iota((1, SSM_GROUP_COLS), 1) // hd
    ys, xdecs = [], []
    g_end_row = jnp.zeros((1, SSM_GROUP_COLS), F32)
    for hl in range(SSM_HPG):
        sel = _iota((1, LANES), 1) == grp * SSM_HPG + hl
        dt_col = jnp.sum(jnp.where(sel, dt, 0.0), axis=1, keepdims=True)
        gc_col = jnp.sum(jnp.where(sel, gcum, 0.0), axis=1, keepdims=True)
        gc_row = jnp.sum(jnp.where(_iota((LANES, 1), 0) == grp * SSM_HPG + hl, gcum_t, 0.0),
                         axis=0, keepdims=True)
        g_end = gc_col[c - 1:c, :]
        xh = x[:, hl * hd:(hl + 1) * hd]
        xdt = xh * dt_col
        decay = _decay_matrix(gc_col, gc_row)
        y = _dot(scores * decay, xdt) + y_inter[:, hl * hd:(hl + 1) * hd] * jnp.exp(gc_col)
        dsk = jnp.sum(jnp.where(sel, dskip_ref[...], 0.0), axis=1, keepdims=True)
        ys.append(y + dsk * xh)
        xdecs.append(xdt * jnp.exp(g_end - gc_col))
        g_end_row = jnp.where(lane_head == hl, g_end, g_end_row)
    y = jnp.concatenate(ys, axis=1)
    xdec = jnp.concatenate(xdecs, axis=1)
    s_ref[...] = s * jnp.exp(g_end_row) + _dot(bm, xdec, _TN)
    y = y * _silu(z_ref[...])
    o_ref[...] = _rms_rows(y, nw_ref[...]).astype(o_ref.dtype)

    @pl.when(c_idx == pl.num_programs(2) - 1)
    def _():
        s_new = s_ref[...]
        for hl in range(SSM_HPG):
            s_out_ref[hl] = s_new[:, hl * hd:(hl + 1) * hd]


def ssd_prompt(proj, conv_w, conv_b, dtb_row, alog_row, dskip_row, norm_w, chunk=128):
    b, l, _ = proj.shape
    nc = l // chunk
    gw = SSM_GROUP_COLS
    cwx, cwb, cwc = conv_w[:, :SSM_INNER], conv_w[:, SSM_INNER:SSM_INNER + SSM_BC_DIM], conv_w[:, SSM_INNER + SSM_BC_DIM:]
    cb = conv_b.reshape(1, SSM_CONV_DIM)
    cbx, cbb, cbc = cb[:, :SSM_INNER], cb[:, SSM_INNER:SSM_INNER + SSM_BC_DIM], cb[:, SSM_INNER + SSM_BC_DIM:]
    col = lambda width, off: pl.BlockSpec((None, chunk, width), lambda i, g, j: (i, j, off // width + g))
    par = lambda rows, width: pl.BlockSpec((rows, width), lambda i, g, j: (0, g))
    row = _resident((1, LANES))
    return pl.pallas_call(
        _ssd_prompt_kernel,
        grid=(b, SSM_GROUPS, nc),
        in_specs=[col(gw, SSM_OFF_Z), col(gw, SSM_OFF_X), col(SSM_STATE, SSM_OFF_B), col(SSM_STATE, SSM_OFF_C),
                  pl.BlockSpec((None, chunk, LANES), lambda i, g, j: (i, j, SSM_OFF_DT // LANES)),
                  par(CONV_WIDTH, gw), par(CONV_WIDTH, SSM_STATE), par(CONV_WIDTH, SSM_STATE),
                  par(1, gw), par(1, SSM_STATE), par(1, SSM_STATE),
                  row, row, row, par(1, gw)],
        out_specs=[pl.BlockSpec((None, chunk, gw), lambda i, g, j: (i, j, g)),
                   pl.BlockSpec((None, SSM_HPG, SSM_STATE, SSM_HEAD_DIM), lambda i, g, j: (i, g, 0, 0))],
        out_shape=[jax.ShapeDtypeStruct((b, l, SSM_INNER), BF16),
                   jax.ShapeDtypeStruct((b, SSM_HEADS, SSM_STATE, SSM_HEAD_DIM), F32)],
        scratch_shapes=[pltpu.VMEM((SSM_STATE, gw), F32), pltpu.VMEM((SUBLANES, gw), F32),
                        pltpu.VMEM((SUBLANES, SSM_STATE), F32), pltpu.VMEM((SUBLANES, SSM_STATE), F32)],
        compiler_params=_cparams(("parallel", "parallel", "arbitrary")),
        name="ssd_prompt",
    )(proj, proj, proj, proj, proj, cwx, cwb, cwc, cbx, cbb, cbc, dtb_row, alog_row, dskip_row,
      norm_w.reshape(1, SSM_INNER))


def _conv_step(x, buf_ref, w, bias=None):
    acc = x * w[CONV_WIDTH - 1:CONV_WIDTH, :]
    for i in range(CONV_WIDTH - 1):
        acc = acc + buf_ref[:, i, :] * w[i:i + 1, :]
    if bias is not None:
        acc = acc + bias
    return acc


def _conv_state_step(x, buf_ref, out_ref):
    for i in range(CONV_WIDTH - 2):
        out_ref[:, i, :] = buf_ref[:, i + 1, :]
    out_ref[:, CONV_WIDTH - 2, :] = x


def _gdn_decode_kernel(q_ref, k_ref, v_ref, z_ref, ba_ref, bq_ref, bk_ref, bv_ref, s_ref,
                       cwq_ref, cwk_ref, cwv_ref, alog_ref, dtb_ref, nw_ref,
                       o_ref, s_out_ref, cq_ref, ck_ref, cv_ref):
    h = pl.program_id(1)
    xq, xk, xv = q_ref[...], k_ref[...], v_ref[...]
    q = _silu(_conv_step(xq, bq_ref, cwq_ref[...]))
    k = _silu(_conv_step(xk, bk_ref, cwk_ref[...]))
    v = _silu(_conv_step(xv, bv_ref, cwv_ref[...]))
    _conv_state_step(xq, bq_ref, cq_ref)
    _conv_state_step(xk, bk_ref, ck_ref)
    _conv_state_step(xv, bv_ref, cv_ref)
    hd = GDN_DIM
    q = q * lax.rsqrt(jnp.sum(q * q, axis=-1, keepdims=True) + NORM_EPS) * (hd ** -0.5)
    k = k * lax.rsqrt(jnp.sum(k * k, axis=-1, keepdims=True) + NORM_EPS)
    ba = ba_ref[...]
    lane = _iota((1, LANES), 1)
    beta = jnp.sum(jnp.where(lane == h, _sigmoid(ba), 0.0), axis=1, keepdims=True)
    g = -jnp.exp(alog_ref[...]) * _softplus(ba + dtb_ref[...])
    eg = jnp.exp(jnp.sum(jnp.where(lane == GDN_HEADS + h, g, 0.0), axis=1, keepdims=True))
    qk = jnp.sum(q * k, axis=-1, keepdims=True)
    k_t = _transpose(k)
    q_t = _transpose(q)
    tb = xq.shape[0]
    ys = []
    for b in range(tb):
        s = s_ref[b, 0]
        k_col = k_t[:, b:b + 1]
        q_col = q_t[:, b:b + 1]
        ks = jnp.sum(k_col * s, axis=0, keepdims=True)
        egb = eg[b:b + 1, :]
        delta = beta[b:b + 1, :] * (v[b:b + 1, :] - egb * ks)
        s_new = egb * s + k_col * delta
        s_out_ref[b, 0] = s_new
        ys.append(jnp.sum(q_col * s_new, axis=0, keepdims=True))
    y = jnp.concatenate(ys, axis=0)
    y = _rms_rows(y, nw_ref[...]) * _silu(z_ref[...])
    o_ref[...] = y.astype(o_ref.dtype)


def gdn_decode(proj, state, conv_state, conv_w, alog_row, dtb_row, norm_w, tb=8):
    b = proj.shape[0]
    hd = GDN_DIM
    nh = GDN_HEADS
    pcol = lambda off: pl.BlockSpec((tb, hd), lambda i, h: (i, off // hd + h))
    ccol = lambda part: pl.BlockSpec((tb, CONV_WIDTH - 1, hd), lambda i, h: (i, 0, part * nh + h))
    wcol = lambda part: pl.BlockSpec((CONV_WIDTH, hd), lambda i, h: (0, part * nh + h))
    row = _resident((1, LANES))
    conv_out = jax.ShapeDtypeStruct((b, CONV_WIDTH - 1, GDN_OUT_DIM), F32)
    conv_out_spec = pl.BlockSpec((tb, CONV_WIDTH - 1, hd), lambda i, h: (i, 0, h))
    o, s_new, cq, ck, cv = pl.pallas_call(
        _gdn_decode_kernel,
        grid=(b // tb, nh),
        in_specs=[pcol(HYB_OFF_QKV), pcol(HYB_OFF_QKV + GDN_OUT_DIM), pcol(HYB_OFF_QKV + 2 * GDN_OUT_DIM),
                  pcol(HYB_OFF_Z), pl.BlockSpec((tb, LANES), lambda i, h: (i, HYB_OFF_BA // LANES)),
                  ccol(0), ccol(1), ccol(2),
                  pl.BlockSpec((tb, 1, hd, hd), lambda i, h: (i, h, 0, 0)),
                  wcol(0), wcol(1), wcol(2), row, row, _resident((1, hd))],
        out_specs=[pl.BlockSpec((tb, hd), lambda i, h: (i, h)),
                   pl.BlockSpec((tb, 1, hd, hd), lambda i, h: (i, h, 0, 0)),
                   conv_out_spec, conv_out_spec, conv_out_spec],
        out_shape=[jax.ShapeDtypeStruct((b, GDN_OUT_DIM), BF16),
                   jax.ShapeDtypeStruct(state.shape, F32), conv_out, conv_out, conv_out],
        compiler_params=_cparams(("parallel", "parallel")),
        name="gdn_decode",
    )(proj, proj, proj, proj, proj, conv_state, conv_state, conv_state, state,
      conv_w, conv_w, conv_w, alog_row, dtb_row, norm_w.reshape(1, hd))
    return o, s_new, jnp.concatenate([cq, ck, cv], axis=-1)


def _ret_decode_kernel(q_ref, k_ref, v_ref, gate_ref, s_ref, cos_ref, sin_ref, nw_ref, o_ref, s_out_ref):
    q = _rotary(q_ref[...], cos_ref[...], sin_ref[...])
    k = _rotary(k_ref[...], cos_ref[...], sin_ref[...]) * (RET_KEY_DIM ** -0.5)
    q_t = _transpose(q)
    k_t = _transpose(k)
    tb = q.shape[0]
    kd, vd = RET_KEY_DIM, RET_VAL_DIM
    rows = []
    for b in range(tb):
        ys = []
        for h in range(RET_HEADS):
            gamma = math.exp(_ret_log_gamma(h))
            s = s_ref[b, h]
            s_new = s * gamma + k_t[h * kd:(h + 1) * kd, b:b + 1] * v_ref[b:b + 1, h * vd:(h + 1) * vd]
            s_out_ref[b, h] = s_new
            ys.append(jnp.sum(q_t[h * kd:(h + 1) * kd, b:b + 1] * s_new, axis=0, keepdims=True))
        rows.append(jnp.concatenate(ys, axis=1))
    y = jnp.concatenate(rows, axis=0)
    for h in range(RET_HEADS):
        yh = (_group_norm_rows(y[:, h * vd:(h + 1) * vd]) * nw_ref[:, h * vd:(h + 1) * vd]
              * _silu(gate_ref[:, h * vd:(h + 1) * vd]))
        o_ref[:, h * vd:(h + 1) * vd] = yh.astype(o_ref.dtype)


def ret_decode(proj, state, cos_f, sin_s, norm_w, tb=8):
    b = proj.shape[0]
    pcol = lambda width, off: pl.BlockSpec((tb, width), lambda i: (i, off // width))
    sspec = pl.BlockSpec((tb, RET_HEADS, RET_KEY_DIM, RET_VAL_DIM), lambda i: (i, 0, 0, 0))
    return pl.pallas_call(
        _ret_decode_kernel,
        grid=(b // tb,),
        in_specs=[pcol(RET_QK_DIM, HYB_OFF_RQ), pcol(RET_QK_DIM, HYB_OFF_RK), pcol(RET_OUT_DIM, HYB_OFF_RV),
                  pcol(RET_OUT_DIM, HYB_OFF_RG), sspec, _resident((1, RET_QK_DIM)), _resident((1, RET_QK_DIM)),
                  _resident((1, RET_OUT_DIM))],
        out_specs=[pl.BlockSpec((tb, RET_OUT_DIM), lambda i: (i, 0)), sspec],
        out_shape=[jax.ShapeDtypeStruct((b, RET_OUT_DIM), BF16), jax.ShapeDtypeStruct(state.shape, F32)],
        compiler_params=_cparams(("parallel",)),
        name="ret_decode",
    )(proj, proj, proj, proj, state, cos_f, sin_s, norm_w.reshape(1, RET_OUT_DIM))


def _ssd_decode_kernel(z_ref, x_ref, b_ref, c_ref, dt_ref, bx_ref, bb_ref, bc_ref, s_ref,
                       cwx_ref, cwb_ref, cwc_ref, cbx_ref, cbb_ref, cbc_ref,
                       dtb_ref, alog_ref, dskip_ref, nw_ref,
                       o_ref, s_out_ref, cx_ref, cb_ref, cc_ref):
    grp = pl.program_id(1)
    xr, br, cr = x_ref[...], b_ref[...], c_ref[...]
    x = _silu(_conv_step(xr, bx_ref, cwx_ref[...], cbx_ref[...]))
    bm = _silu(_conv_step(br, bb_ref, cwb_ref[...], cbb_ref[...]))
    cm = _silu(_conv_step(cr, bc_ref, cwc_ref[...], cbc_ref[...]))
    _conv_state_step(xr, bx_ref, cx_ref)
    _conv_state_step(br, bb_ref, cb_ref)
    _conv_state_step(cr, bc_ref, cc_ref)
    dt = _softplus(dt_ref[...] + dtb_ref[...])
    eg = jnp.exp(dt * (-jnp.exp(alog_ref[...])))
    b_t = _transpose(bm)
    c_t = _transpose(cm)
    tb = xr.shape[0]
    hd = SSM_HEAD_DIM
    lane = _iota((1, LANES), 1)
    rows = []
    for b in range(tb):
        b_col = b_t[:, b:b + 1]
        c_col = c_t[:, b:b + 1]
        ys = []
        for hl in range(SSM_HPG):
            sel = lane == grp * SSM_HPG + hl
            dt_b = jnp.sum(jnp.where(sel, dt[b:b + 1, :], 0.0), axis=1, keepdims=True)
            eg_b = jnp.sum(jnp.where(sel, eg[b:b + 1, :], 0.0), axis=1, keepdims=True)
            dsk = jnp.sum(jnp.where(sel, dskip_ref[...], 0.0), axis=1, keepdims=True)
            xh = x[b:b + 1, hl * hd:(hl + 1) * hd]
            s_new = s_ref[b, hl] * eg_b + b_col * (xh * dt_b)
            s_out_ref[b, hl] = s_new
            ys.append(jnp.sum(c_col * s_new, axis=0, keepdims=True) + dsk * xh)
        rows.append(jnp.concatenate(ys, axis=1))
    y = jnp.concatenate(rows, axis=0) * _silu(z_ref[...])
    o_ref[...] = _rms_rows(y, nw_ref[...]).astype(o_ref.dtype)


def ssd_decode(proj, state, conv_state, conv_w, conv_b, dtb_row, alog_row, dskip_row, norm_w, tb=8):
    b = proj.shape[0]
    gw = SSM_GROUP_COLS
    ns = SSM_STATE
    cb = conv_b.reshape(1, SSM_CONV_DIM)
    pcol = lambda width, off: pl.BlockSpec((tb, width), lambda i, g: (i, off // width + g))
    ccol = lambda width, off: pl.BlockSpec((tb, CONV_WIDTH - 1, width), lambda i, g: (i, 0, off // width + g))
    wcol = lambda rows, width, off: pl.BlockSpec((rows, width), lambda i, g: (0, off // width + g))
    row = _resident((1, LANES))
    sspec = pl.BlockSpec((tb, SSM_HPG, ns, SSM_HEAD_DIM), lambda i, g: (i, g, 0, 0))
    off_b, off_c = SSM_INNER, SSM_INNER + SSM_BC_DIM
    o, s_new, cx, cbs, ccs = pl.pallas_call(
        _ssd_decode_kernel,
        grid=(b // tb, SSM_GROUPS),
        in_specs=[pcol(gw, SSM_OFF_Z), pcol(gw, SSM_OFF_X), pcol(ns, SSM_OFF_B), pcol(ns, SSM_OFF_C),
                  pl.BlockSpec((tb, LANES), lambda i, g: (i, SSM_OFF_DT // LANES)),
                  ccol(gw, 0), ccol(ns, off_b), ccol(ns, off_c), sspec,
                  wcol(CONV_WIDTH, gw, 0), wcol(CONV_WIDTH, ns, off_b), wcol(CONV_WIDTH, ns, off_c),
                  wcol(1, gw, 0), wcol(1, ns, off_b), wcol(1, ns, off_c),
                  row, row, row, wcol(1, gw, 0)],
        out_specs=[pl.BlockSpec((tb, gw), lambda i, g: (i, g)), sspec,
                   pl.BlockSpec((tb, CONV_WIDTH - 1, gw), lambda i, g: (i, 0, g)),
                   pl.BlockSpec((tb, CONV_WIDTH - 1, ns), lambda i, g: (i, 0, g)),
                   pl.BlockSpec((tb, CONV_WIDTH - 1, ns), lambda i, g: (i, 0, g))],
        out_shape=[jax.ShapeDtypeStruct((b, SSM_INNER), BF16), jax.ShapeDtypeStruct(state.shape, F32),
                   jax.ShapeDtypeStruct((b, CONV_WIDTH - 1, SSM_INNER), F32),
                   jax.ShapeDtypeStruct((b, CONV_WIDTH - 1, SSM_BC_DIM), F32),
                   jax.ShapeDtypeStruct((b, CONV_WIDTH - 1, SSM_BC_DIM), F32)],
        compiler_params=_cparams(("parallel", "parallel")),
        name="ssd_decode",
    )(proj, proj, proj, proj, proj, conv_state, conv_state, conv_state, state,
      conv_w, conv_w, conv_w, cb, cb, cb, dtb_row, alog_row, dskip_row, norm_w.reshape(1, SSM_INNER))
    return o, s_new, jnp.concatenate([cx, cbs, ccs], axis=-1)


def _lane_row(vals, offset):
    return jnp.zeros((1, LANES), F32).at[0, offset:offset + vals.shape[0]].set(vals.astype(F32))


def _prep_hyb_w_in(w):
    sizes = (GDN_QKV_DIM, GDN_OUT_DIM, GDN_HEADS, GDN_HEADS, RET_QK_DIM, RET_QK_DIM, RET_OUT_DIM, RET_OUT_DIM)
    offs = [0]
    for s in sizes:
        offs.append(offs[-1] + s)
    part = lambda i: w[:, offs[i]:offs[i + 1]]
    cols = [part(0), part(1), part(4), part(5), part(6), part(7), part(2), part(3)]
    out = jnp.concatenate(cols, axis=1)
    return jnp.pad(out, ((0, 0), (0, HYB_N - out.shape[1]))).astype(BF16)


def _prep_ssm_w_in(w):
    return jnp.pad(w, ((0, 0), (0, SSM_N - w.shape[1]))).astype(BF16)


def _rope_tables(pos):
    half = RET_KEY_DIM // 2
    inv_freq = ROPE_BASE ** (-jnp.arange(half, dtype=F32) / half)
    ang = pos.astype(F32)[:, None] * inv_freq[None, :]
    cos, sin = jnp.cos(ang), jnp.sin(ang)
    cos_f = jnp.tile(jnp.concatenate([cos, cos], axis=1), (1, RET_HEADS))
    sin_s = jnp.tile(jnp.concatenate([-sin, sin], axis=1), (1, RET_HEADS))
    return cos_f, sin_s


def _trunk(x, pos, states, params, prompt):
    bsz, l, d = x.shape
    m = bsz * l
    tm = 512 if m % 512 == 0 else m
    x2 = x.reshape(m, d)
    cos_f, sin_s = _rope_tables(pos)
    s_gdn, c_gdn, s_ret, s_ssm, c_ssm = states
    new_gdn, new_gdn_conv, new_ret, new_ssm, new_ssm_conv = [], [], [], [], []
    for layer in range(DEPTH):
        i = layer // 2
        p = params
        if layer % 2 == 0:
            proj = norm_matmul(x2, p["norm_mix"][layer], p["w_in_hyb"][i], tm)
            alog_row = _lane_row(p["gdn_a_log"][i], GDN_HEADS)
            dtb_row = _lane_row(p["gdn_dt_bias"][i], GDN_HEADS)
            if prompt:
                proj3 = proj.reshape(bsz, l, HYB_N)
                o_a, sg = gdn_prompt(proj3, p["gdn_conv_w"][i], alog_row, dtb_row, p["gdn_norm_w"][i])
                o_b, sr = ret_prompt(proj3, cos_f, sin_s, p["ret_norm_w"][i])
                cg = proj3[:, l - (CONV_WIDTH - 1):, :GDN_QKV_DIM]
                o_a, o_b = o_a.reshape(m, GDN_OUT_DIM), o_b.reshape(m, RET_OUT_DIM)
            else:
                o_a, sg, cg = gdn_decode(proj, s_gdn[i], c_gdn[i], p["gdn_conv_w"][i], alog_row, dtb_row,
                                         p["gdn_norm_w"][i])
                o_b, sr = ret_decode(proj, s_ret[i], cos_f, sin_s, p["ret_norm_w"][i])
            new_gdn.append(sg)
            new_gdn_conv.append(cg)
            new_ret.append(sr)
            w_out = p["w_out_hyb"][i]
            x2 = out_proj_residual([o_a, o_b], [w_out[:GDN_OUT_DIM], w_out[GDN_OUT_DIM:]], x2, tm)
        else:
            proj = norm_matmul(x2, p["norm_mix"][layer], p["w_in_ssm"][i], tm)
            dtb_row = _lane_row(p["ssm_dt_bias"][i], 0)
            alog_row = _lane_row(p["ssm_a_log"][i], 0)
            dskip_row = _lane_row(p["ssm_d"][i], 0)
            if prompt:
                proj3 = proj.reshape(bsz, l, SSM_N)
                y, ss = ssd_prompt(proj3, p["ssm_conv_w"][i], p["ssm_conv_b"][i], dtb_row, alog_row,
                                   dskip_row, p["ssm_norm_w"][i])
                cs = proj3[:, l - (CONV_WIDTH - 1):, SSM_OFF_X:SSM_OFF_X + SSM_CONV_DIM]
                y = y.reshape(m, SSM_INNER)
            else:
                y, ss, cs = ssd_decode(proj, s_ssm[i], c_ssm[i], p["ssm_conv_w"][i], p["ssm_conv_b"][i],
                                       dtb_row, alog_row, dskip_row, p["ssm_norm_w"][i])
            new_ssm.append(ss)
            new_ssm_conv.append(cs)
            x2 = out_proj_residual([y], [p["w_out_ssm"][i]], x2, tm)
        x2 = mlp_residual(x2, p["norm_mlp"][layer], p["mlp_w1"][layer], p["mlp_w2"][layer],
                          p["norm_final"], layer == DEPTH - 1, tm)
    return (x2.reshape(bsz, l, d), jnp.stack(new_gdn), jnp.stack(new_gdn_conv), jnp.stack(new_ret),
            jnp.stack(new_ssm), jnp.stack(new_ssm_conv))


def kernel(x_prompt, x_sample, state_gdn, state_gdn_conv, state_ret, state_ssm, state_ssm_conv, norm_mix, norm_mlp, norm_final, w_in_hyb, gdn_conv_w, gdn_a_log, gdn_dt_bias, gdn_norm_w, ret_norm_w, w_out_hyb, w_in_ssm, ssm_conv_w, ssm_conv_b, ssm_dt_bias, ssm_a_log, ssm_d, ssm_norm_w, w_out_ssm, mlp_w1, mlp_w2):
    n_hyb, n_ssm = w_in_hyb.shape[0], w_in_ssm.shape[0]
    params = dict(
        norm_mix=norm_mix, norm_mlp=norm_mlp, norm_final=norm_final,
        w_in_hyb=[_prep_hyb_w_in(w_in_hyb[i]) for i in range(n_hyb)],
        gdn_conv_w=gdn_conv_w, gdn_a_log=gdn_a_log, gdn_dt_bias=gdn_dt_bias, gdn_norm_w=gdn_norm_w,
        ret_norm_w=ret_norm_w, w_out_hyb=w_out_hyb.astype(BF16),
        w_in_ssm=[_prep_ssm_w_in(w_in_ssm[i]) for i in range(n_ssm)],
        ssm_conv_w=ssm_conv_w, ssm_conv_b=ssm_conv_b, ssm_dt_bias=ssm_dt_bias, ssm_a_log=ssm_a_log,
        ssm_d=ssm_d, ssm_norm_w=ssm_norm_w, w_out_ssm=w_out_ssm.astype(BF16),
        mlp_w1=mlp_w1.astype(BF16), mlp_w2=mlp_w2.astype(BF16))
    lp, ls = x_prompt.shape[1], x_sample.shape[1]
    pos_prompt = jnp.arange(lp, dtype=jnp.int32)
    pos_sample = PAST_LEN + jnp.arange(ls, dtype=jnp.int32)
    y_p, p_gdn, p_gdn_conv, p_ret, p_ssm, p_ssm_conv = _trunk(
        x_prompt, pos_prompt, (None,) * 5, params, prompt=True)
    y_s, s_gdn, s_gdn_conv, s_ret, s_ssm, s_ssm_conv = _trunk(
        x_sample, pos_sample, (state_gdn, state_gdn_conv, state_ret, state_ssm, state_ssm_conv),
        params, prompt=False)
    return (y_p, y_s, p_gdn, p_gdn_conv, p_ret, p_ssm, p_ssm_conv,
            s_gdn, s_gdn_conv, s_ret, s_ssm, s_ssm_conv)
```

```python
import functools
import math

import jax
import jax.numpy as jnp
from jax import lax
from jax.experimental import pallas as pl
from jax.experimental.pallas import tpu as pltpu

F32 = jnp.float32
BF16 = jnp.bfloat16

D_MODEL = 1024
DEPTH = 4
CONV_WIDTH = 4
NORM_EPS = 1e-6
PAST_LEN = 16384

GDN_HEADS = 4
GDN_DIM = 128
GDN_QKV_DIM = 3 * GDN_HEADS * GDN_DIM
GDN_OUT_DIM = GDN_HEADS * GDN_DIM

RET_HEADS = 4
RET_KEY_DIM = 64
RET_VAL_DIM = 128
RET_QK_DIM = RET_HEADS * RET_KEY_DIM
RET_OUT_DIM = RET_HEADS * RET_VAL_DIM
ROPE_BASE = 10000.0

SSM_INNER = 2 * D_MODEL
SSM_HEAD_DIM = 64
SSM_HEADS = SSM_INNER // SSM_HEAD_DIM
SSM_GROUPS = 4
SSM_HPG = SSM_HEADS // SSM_GROUPS
SSM_STATE = 128
SSM_GROUP_COLS = SSM_HPG * SSM_HEAD_DIM
SSM_BC_DIM = SSM_GROUPS * SSM_STATE
SSM_CONV_DIM = SSM_INNER + 2 * SSM_BC_DIM
MLP_HIDDEN = 4 * D_MODEL

LANES = 128
SUBLANES = 8

HYB_OFF_QKV = 0
HYB_OFF_Z = GDN_QKV_DIM
HYB_OFF_RQ = HYB_OFF_Z + GDN_OUT_DIM
HYB_OFF_RK = HYB_OFF_RQ + RET_QK_DIM
HYB_OFF_RV = HYB_OFF_RK + RET_QK_DIM
HYB_OFF_RG = HYB_OFF_RV + RET_OUT_DIM
HYB_OFF_BA = HYB_OFF_RG + RET_OUT_DIM
HYB_N = 3840
SSM_OFF_Z = 0
SSM_OFF_X = SSM_INNER
SSM_OFF_B = SSM_OFF_X + SSM_INNER
SSM_OFF_C = SSM_OFF_B + SSM_BC_DIM
SSM_OFF_DT = SSM_OFF_C + SSM_BC_DIM
SSM_N = 5376
PROJ_TN = 768

VMEM_LIMIT = 56 * 1024 * 1024

_NN = (((1,), (0,)), ((), ()))
_NT = (((1,), (1,)), ((), ()))
_TN = (((0,), (0,)), ((), ()))


def _dot(a, b, dims=_NN):
    return lax.dot_general(a.astype(BF16), b.astype(BF16), dims, preferred_element_type=F32)


def _dot_f32(a, b, dims=_NN):
    return lax.dot_general(a, b, dims, precision=lax.Precision.HIGHEST, preferred_element_type=F32)


def _sigmoid(x):
    return 1.0 / (1.0 + jnp.exp(-x))


def _silu(x):
    return x * _sigmoid(x)


def _softplus(x):
    return jnp.maximum(x, 0.0) + jnp.log(1.0 + jnp.exp(-jnp.abs(x)))


def _iota(shape, dim):
    return lax.broadcasted_iota(jnp.int32, shape, dim)


def _eye(n):
    return (_iota((n, n), 0) == _iota((n, n), 1)).astype(F32)


def _transpose(x):
    return _dot_f32(_eye(x.shape[1]), x, _NT)


def _cparams(sem):
    return pltpu.CompilerParams(dimension_semantics=sem, vmem_limit_bytes=VMEM_LIMIT)


def _resident(shape):
    nd = len(shape)
    return pl.BlockSpec(shape, lambda *_: (0,) * nd, pipeline_mode=pl.Buffered(1))


def _rms_rows(x, gain):
    return x * lax.rsqrt(jnp.mean(x * x, axis=-1, keepdims=True) + NORM_EPS) * gain


def _norm_matmul_kernel(x_ref, g_ref, w_ref, o_ref, *, tn):
    xn = _rms_rows(x_ref[...], g_ref[...]).astype(BF16)
    for j in range(w_ref.shape[1] // tn):
        o_ref[:, j * tn:(j + 1) * tn] = jnp.dot(xn, w_ref[:, j * tn:(j + 1) * tn],
                                                preferred_element_type=F32)


def norm_matmul(x, gain, w, tm):
    m, d = x.shape
    n = w.shape[1]
    return pl.pallas_call(
        functools.partial(_norm_matmul_kernel, tn=PROJ_TN),
        grid=(m // tm,),
        in_specs=[pl.BlockSpec((tm, d), lambda i: (i, 0)), _resident((1, d)), _resident((d, n))],
        out_specs=pl.BlockSpec((tm, n), lambda i: (i, 0)),
        out_shape=jax.ShapeDtypeStruct((m, n), F32),
        compiler_params=_cparams(("parallel",)),
        name="norm_matmul",
    )(x, gain.reshape(1, d), w)


def _out_proj_kernel(*refs, n_in):
    a_refs, w_refs, r_ref, o_ref = refs[:n_in], refs[n_in:2 * n_in], refs[2 * n_in], refs[2 * n_in + 1]
    acc = r_ref[...]
    for a_ref, w_ref in zip(a_refs, w_refs):
        acc = acc + jnp.dot(a_ref[...], w_ref[...], preferred_element_type=F32)
    o_ref[...] = acc


def out_proj_residual(acts, ws, res, tm):
    m, d = res.shape
    n_in = len(acts)
    in_specs = ([pl.BlockSpec((tm, a.shape[1]), lambda i: (i, 0)) for a in acts]
                + [_resident(w.shape) for w in ws]
                + [pl.BlockSpec((tm, d), lambda i: (i, 0))])
    return pl.pallas_call(
        functools.partial(_out_proj_kernel, n_in=n_in),
        grid=(m // tm,),
        in_specs=in_specs,
        out_specs=pl.BlockSpec((tm, d), lambda i: (i, 0)),
        out_shape=jax.ShapeDtypeStruct((m, d), F32),
        compiler_params=_cparams(("parallel",)),
        name="out_proj",
    )(*acts, *ws, res)


def _mlp_kernel(x_ref, g_ref, w1_ref, w2_ref, gf_ref, o_ref, *, tf, final_norm):
    x = x_ref[...]
    xn = _rms_rows(x, g_ref[...]).astype(BF16)
    acc = x
    for j in range(w1_ref.shape[1] // tf):
        h = jnp.dot(xn, w1_ref[:, j * tf:(j + 1) * tf], preferred_element_type=F32)
        h = jnp.maximum(h, 0.0)
        acc = acc + jnp.dot((h * h).astype(BF16), w2_ref[j * tf:(j + 1) * tf, :],
                            preferred_element_type=F32)
    if final_norm:
        acc = _rms_rows(acc, gf_ref[...])
    o_ref[...] = acc


def mlp_residual(x, gain, w1, w2, final_gain, final_norm, tm):
    m, d = x.shape
    f = w1.shape[1]
    return pl.pallas_call(
        functools.partial(_mlp_kernel, tf=1024, final_norm=final_norm),
        grid=(m // tm,),
        in_specs=[pl.BlockSpec((tm, d), lambda i: (i, 0)), _resident((1, d)), _resident((d, f)),
                  _resident((f, d)), _resident((1, d))],
        out_specs=pl.BlockSpec((tm, d), lambda i: (i, 0)),
        out_shape=jax.ShapeDtypeStruct((m, d), F32),
        compiler_params=_cparams(("parallel",)),
        name="mlp",
    )(x, gain.reshape(1, d), w1, w2, final_gain.reshape(1, d))


def _causal_conv_chunk(x, tail, w, bias=None):
    c = x.shape[0]
    acc = x * w[CONV_WIDTH - 1:CONV_WIDTH, :]
    row = _iota((SUBLANES, x.shape[1]), 0)
    for k in range(1, CONV_WIDTH):
        xr = pltpu.roll(x, k, 0)
        tr = pltpu.roll(tail, k, 0)
        head = jnp.where(row < k, tr, xr[0:SUBLANES])
        shifted = jnp.concatenate([head, xr[SUBLANES:c]], axis=0)
        acc = acc + shifted * w[CONV_WIDTH - 1 - k:CONV_WIDTH - k, :]
    if bias is not None:
        acc = acc + bias
    return acc


def _cumsum_rows(g):
    c = g.shape[0]
    tri = (_iota((c, c), 0) >= _iota((c, c), 1)).astype(F32)
    return _dot_f32(tri, g)


def _decay_matrix(gc_col, gc_row):
    c = gc_col.shape[0]
    causal = _iota((c, c), 0) >= _iota((c, c), 1)
    return jnp.where(causal, jnp.exp(jnp.where(causal, gc_col - gc_row, 0.0)), 0.0)


def _split(x):
    hi = x.astype(BF16)
    return hi, (x - hi.astype(F32)).astype(BF16)


def _dot3(a, b, dims=_NN):
    (ah, al), (bh, bl) = a, b
    d = lambda p, q: lax.dot_general(p, q, dims, preferred_element_type=F32)
    return d(ah, bh) + (d(ah, bl) + d(al, bh))


def _unit_lower_inverses(n_list):
    c = n_list[0].shape[0]
    ri, ci = _iota((c, c), 0), _iota((c, c), 1)
    eye = (ri == ci).astype(F32)
    blk = 16
    same = (ri // blk) == (ci // blk)
    a = [jnp.where(same, -n, 0.0) for n in n_list]
    inv = [eye + x for x in a]
    p = [_split(x) for x in a]
    for _ in range(3):
        p = [_split(_dot3(x, x)) for x in p]
        inv = [i + _dot3(_split(i), x) for i, x in zip(inv, p)]
    while blk < c:
        pair = ((ri // (2 * blk)) == (ci // (2 * blk))) & ((ri // blk) != (ci // blk))
        off = [_split(jnp.where(pair, n, 0.0)) for n in n_list]
        inv_s = [_split(i) for i in inv]
        t = [_split(_dot3(o, i)) for o, i in zip(off, inv_s)]
        inv = [i - _dot3(i_s, x) for i, i_s, x in zip(inv, inv_s, t)]
        blk *= 2
    return inv


def _gdn_prompt_kernel(qkv_ref, z_ref, ba_ref, cw_ref, alog_ref, dtb_ref, nw_ref,
                       o_ref, s_out_ref, s_ref, tail_ref):
    c_idx = pl.program_id(1)

    @pl.when(c_idx == 0)
    def _():
        s_ref[...] = jnp.zeros_like(s_ref)
        tail_ref[...] = jnp.zeros_like(tail_ref)

    nb, c = qkv_ref.shape[0], qkv_ref.shape[1]
    hd = GDN_DIM
    ri, ci = _iota((c, c), 0), _iota((c, c), 1)
    strict = ri > ci
    chains = [(bb, h) for bb in range(nb) for h in range(GDN_HEADS)]

    conv, beta, gcum, gcum_t = [], [], [], []
    for bb in range(nb):
        x = qkv_ref[bb]
        conv.append(_silu(_causal_conv_chunk(x, tail_ref[bb], cw_ref[...])))
        tail_ref[bb] = x[c - SUBLANES:c]
        ba = ba_ref[bb]
        beta.append(_sigmoid(ba))
        gc = _cumsum_rows(-jnp.exp(alog_ref[...]) * _softplus(ba + dtb_ref[...]))
        gcum.append(gc)
        gcum_t.append(_transpose(gc))

    q, k, v, b_col, gc_col, g_end, decay = [], [], [], [], [], [], []
    for bb, h in chains:
        qh = conv[bb][:, h * hd:(h + 1) * hd]
        kh = conv[bb][:, GDN_OUT_DIM + h * hd:GDN_OUT_DIM + (h + 1) * hd]
        q.append(qh * lax.rsqrt(jnp.sum(qh * qh, axis=-1, keepdims=True) + NORM_EPS) * (hd ** -0.5))
        k.append(kh * lax.rsqrt(jnp.sum(kh * kh, axis=-1, keepdims=True) + NORM_EPS))
        v.append(conv[bb][:, 2 * GDN_OUT_DIM + h * hd:2 * GDN_OUT_DIM + (h + 1) * hd])
        b_col.append(beta[bb][:, h:h + 1])
        gc_col.append(gcum[bb][:, GDN_HEADS + h:GDN_HEADS + h + 1])
        g_end.append(gcum[bb][c - 1:c, GDN_HEADS + h:GDN_HEADS + h + 1])
        decay.append(_decay_matrix(gc_col[-1], gcum_t[bb][GDN_HEADS + h:GDN_HEADS + h + 1, :]))
    kk = [_dot(x, x, _NT) for x in k]
    qk = [_dot(x, y, _NT) * d for x, y, d in zip(q, k, decay)]
    t_inv = _unit_lower_inverses([jnp.where(strict, b * x * d, 0.0) for b, x, d in zip(b_col, kk, decay)])
    rhs = [_split(jnp.concatenate([vv * b, kx * (b * jnp.exp(g))], axis=1))
           for vv, kx, b, g in zip(v, k, b_col, gc_col)]
    sol = [_dot3(_split(t), r) for t, r in zip(t_inv, rhs)]
    s_old = [s_ref[bb, h] for bb, h in chains]
    delta = [x[:, :hd] - _dot(x[:, hd:], s) for x, s in zip(sol, s_old)]
    y = [_dot(x * jnp.exp(g), s) + _dot(a, dl)
         for x, g, s, a, dl in zip(q, gc_col, s_old, qk, delta)]
    s_new = [s * jnp.exp(ge) + _dot(kx * jnp.exp(ge - g), dl, _TN)
             for s, ge, kx, g, dl in zip(s_old, g_end, k, gc_col, delta)]
    for (bb, h), sn, yy in zip(chains, s_new, y):
        s_ref[bb, h] = sn
        out = _rms_rows(yy, nw_ref[...]) * _silu(z_ref[bb, :, h * hd:(h + 1) * hd])
        o_ref[bb, :, h * hd:(h + 1) * hd] = out.astype(o_ref.dtype)

    @pl.when(c_idx == pl.num_programs(1) - 1)
    def _():
        s_out_ref[...] = s_ref[...]


def gdn_prompt(proj, conv_w, alog_row, dtb_row, norm_w, chunk=64, nb=2):
    b, l, _ = proj.shape
    nc = l // chunk
    col = lambda width, off: pl.BlockSpec((nb, chunk, width), lambda i, j: (i, j, off // width))
    return pl.pallas_call(
        _gdn_prompt_kernel,
        grid=(b // nb, nc),
        in_specs=[col(GDN_QKV_DIM, HYB_OFF_QKV), col(GDN_OUT_DIM, HYB_OFF_Z), col(LANES, HYB_OFF_BA),
                  _resident((CONV_WIDTH, GDN_QKV_DIM)), _resident((1, LANES)), _resident((1, LANES)),
                  _resident((1, GDN_DIM))],
        out_specs=[pl.BlockSpec((nb, chunk, GDN_OUT_DIM), lambda i, j: (i, j, 0)),
                   pl.BlockSpec((nb, GDN_HEADS, GDN_DIM, GDN_DIM), lambda i, j: (i, 0, 0, 0))],
        out_shape=[jax.ShapeDtypeStruct((b, l, GDN_OUT_DIM), BF16),
                   jax.ShapeDtypeStruct((b, GDN_HEADS, GDN_DIM, GDN_DIM), F32)],
        scratch_shapes=[pltpu.VMEM((nb, GDN_HEADS, GDN_DIM, GDN_DIM), F32),
                        pltpu.VMEM((nb, SUBLANES, GDN_QKV_DIM), F32)],
        compiler_params=_cparams(("parallel", "arbitrary")),
        name="gdn_prompt",
    )(proj, proj, proj, conv_w, alog_row, dtb_row, norm_w.reshape(1, GDN_DIM))


def _ret_log_gamma(h):
    return math.log(1.0 - 2.0 ** (-5.0 - h))


def _rotary(t, cos_f, sin_s):
    n = t.shape[1]
    half = RET_KEY_DIM // 2
    first = (_iota(t.shape, 1) % RET_KEY_DIM) < half
    swapped = jnp.where(first, pltpu.roll(t, n - half, 1), pltpu.roll(t, half, 1))
    return t * cos_f + swapped * sin_s


def _group_norm_rows(x):
    xc = x - jnp.mean(x, axis=-1, keepdims=True)
    return xc * lax.rsqrt(jnp.mean(xc * xc, axis=-1, keepdims=True) + NORM_EPS)


def _ret_prompt_kernel(q_ref, k_ref, v_ref, gate_ref, cos_ref, sin_ref, nw_ref,
                       o_ref, s_out_ref, s_ref):
    c_idx = pl.program_id(1)

    @pl.when(c_idx == 0)
    def _():
        s_ref[...] = jnp.zeros_like(s_ref)

    q = _rotary(q_ref[...], cos_ref[...], sin_ref[...])
    k = _rotary(k_ref[...], cos_ref[...], sin_ref[...]) * (RET_KEY_DIM ** -0.5)
    c = q.shape[0]
    ri, ci = _iota((c, c), 0), _iota((c, c), 1)
    causal = ri >= ci
    pos_col = _iota((c, 1), 0).astype(F32)
    kd, vd = RET_KEY_DIM, RET_VAL_DIM
    for h in range(RET_HEADS):
        lg = _ret_log_gamma(h)
        decay = jnp.where(causal, jnp.exp(jnp.where(causal, (ri - ci).astype(F32) * lg, 0.0)), 0.0)
        qh = q[:, h * kd:(h + 1) * kd]
        kh = k[:, h * kd:(h + 1) * kd]
        vh = v_ref[:, h * vd:(h + 1) * vd]
        s = s_ref[h]
        y = _dot(_dot(qh, kh, _NT) * decay, vh) + _dot(qh, s) * jnp.exp((pos_col + 1.0) * lg)
        s_ref[h] = s * math.exp(c * lg) + _dot(kh * jnp.exp((c - 1.0 - pos_col) * lg), vh, _TN)
        y = (_group_norm_rows(y) * nw_ref[:, h * vd:(h + 1) * vd]
             * _silu(gate_ref[:, h * vd:(h + 1) * vd]))
        o_ref[:, h * vd:(h + 1) * vd] = y.astype(o_ref.dtype)

    @pl.when(c_idx == pl.num_programs(1) - 1)
    def _():
        s_out_ref[...] = s_ref[...]


def ret_prompt(proj, cos_f, sin_s, norm_w, chunk=128):
    b, l, _ = proj.shape
    nc = l // chunk
    col = lambda width, off: pl.BlockSpec((None, chunk, width), lambda i, j: (i, j, off // width))
    tab = pl.BlockSpec((chunk, RET_QK_DIM), lambda i, j: (j, 0))
    return pl.pallas_call(
        _ret_prompt_kernel,
        grid=(b, nc),
        in_specs=[col(RET_QK_DIM, HYB_OFF_RQ), col(RET_QK_DIM, HYB_OFF_RK), col(RET_OUT_DIM, HYB_OFF_RV),
                  col(RET_OUT_DIM, HYB_OFF_RG), tab, tab, _resident((1, RET_OUT_DIM))],
        out_specs=[pl.BlockSpec((None, chunk, RET_OUT_DIM), lambda i, j: (i, j, 0)),
                   pl.BlockSpec((None, RET_HEADS, RET_KEY_DIM, RET_VAL_DIM), lambda i, j: (i, 0, 0, 0))],
        out_shape=[jax.ShapeDtypeStruct((b, l, RET_OUT_DIM), BF16),
                   jax.ShapeDtypeStruct((b, RET_HEADS, RET_KEY_DIM, RET_VAL_DIM), F32)],
        scratch_shapes=[pltpu.VMEM((RET_HEADS, RET_KEY_DIM, RET_VAL_DIM), F32)],
        compiler_params=_cparams(("parallel", "arbitrary")),
        name="ret_prompt",
    )(proj, proj, proj, proj, cos_f, sin_s, norm_w.reshape(1, RET_OUT_DIM))


def _split3(x):
    a = x.astype(BF16)
    r = x - a.astype(F32)
    b = r.astype(BF16)
    return a, b, (r - b.astype(F32)).astype(BF16)


def _spread(x, sel):
    return sum(jnp.dot(p, sel, preferred_element_type=F32) for p in _split3(x))


def _ssd_prompt_kernel(z_ref, x_ref, b_ref, c_ref, dt_ref, cw_ref, cb_ref, dtb_ref, alog_ref, dskip_ref,
                       nw_ref, sel_hd_ref, sel_c_ref, o_ref, s_out_ref, s_ref, tx_ref, tb_ref, tc_ref):
    c_idx = pl.program_id(1)

    @pl.when(c_idx == 0)
    def _():
        s_ref[...] = jnp.zeros_like(s_ref)
        tx_ref[...] = jnp.zeros_like(tx_ref)
        tb_ref[...] = jnp.zeros_like(tb_ref)
        tc_ref[...] = jnp.zeros_like(tc_ref)

    c = x_ref.shape[0]
    hd, gw, ns = SSM_HEAD_DIM, SSM_GROUP_COLS, SSM_STATE
    off_b, off_c = SSM_INNER, SSM_INNER + SSM_BC_DIM
    groups = range(SSM_GROUPS)

    dt = _softplus(dt_ref[...] + dtb_ref[...])
    gcum = _cumsum_rows(dt * (-jnp.exp(alog_ref[...])))
    gcum_t = _transpose(gcum)
    per_col = _spread(jnp.concatenate([dt, gcum], axis=0), sel_hd_ref[...])
    dt_x, gc_x = per_col[:c], per_col[c:]
    gc_colb = _spread(gcum, sel_c_ref[...])
    ge_x = gc_x[c - 1:c, :]

    xr, br, cr = x_ref[...], b_ref[...], c_ref[...]
    x = _silu(_causal_conv_chunk(xr, tx_ref[...], cw_ref[:, :off_b], cb_ref[:, :off_b]))
    bm = _silu(_causal_conv_chunk(br, tb_ref[...], cw_ref[:, off_b:off_c], cb_ref[:, off_b:off_c]))
    cm = _silu(_causal_conv_chunk(cr, tc_ref[...], cw_ref[:, off_c:], cb_ref[:, off_c:]))
    tx_ref[...] = xr[c - SUBLANES:c]
    tb_ref[...] = br[c - SUBLANES:c]
    tc_ref[...] = cr[c - SUBLANES:c]

    xdt = x * dt_x
    xdec = xdt * jnp.exp(ge_x - gc_x)
    bg = [bm[:, g * ns:(g + 1) * ns] for g in groups]
    cg = [cm[:, g * ns:(g + 1) * ns] for g in groups]
    s_old = [s_ref[g] for g in groups]
    scores = [_dot(cc, bb, _NT) for cc, bb in zip(cg, bg)]
    y_inter = jnp.concatenate([_dot(cc, s) for cc, s in zip(cg, s_old)], axis=1)
    att = [scores[h // SSM_HPG] * _decay_matrix(gc_colb[:, h * c:(h + 1) * c], gcum_t[h:h + 1, :])
           for h in range(SSM_HEADS)]
    y_intra = jnp.concatenate([_dot(a, xdt[:, h * hd:(h + 1) * hd]) for h, a in enumerate(att)], axis=1)
    for g in groups:
        cols = slice(g * gw, (g + 1) * gw)
        s_ref[g] = s_old[g] * jnp.exp(ge_x[:, cols]) + _dot(bg[g], xdec[:, cols], _TN)
    y = (y_intra + y_inter * jnp.exp(gc_x) + dskip_ref[...] * x) * _silu(z_ref[...])
    for g in groups:
        cols = slice(g * gw, (g + 1) * gw)
        o_ref[:, cols] = _rms_rows(y[:, cols], nw_ref[:, cols]).astype(o_ref.dtype)

    @pl.when(c_idx == pl.num_programs(1) - 1)
    def _():
        for g in range(SSM_GROUPS):
            for hl in range(SSM_HPG):
                s_out_ref[g * SSM_HPG + hl] = s_ref[g, :, hl * hd:(hl + 1) * hd]


def ssd_prompt(proj, conv_w, conv_b, dtb_row, alog_row, dskip_row, norm_w, chunk=128):
    b, l, _ = proj.shape
    assert chunk == LANES
    nc = l // chunk
    col = lambda width, off: pl.BlockSpec((None, chunk, width), lambda i, j: (i, j, off // width))
    row = _resident((1, LANES))
    lane = jnp.arange(LANES, dtype=jnp.int32)[:, None]
    sel_hd = (lane == jnp.arange(SSM_INNER, dtype=jnp.int32)[None, :] // SSM_HEAD_DIM).astype(BF16)
    sel_c = (lane == jnp.arange(SSM_HEADS * chunk, dtype=jnp.int32)[None, :] // chunk).astype(BF16)
    dskip_row = jnp.repeat(dskip_row[0, :SSM_HEADS], SSM_HEAD_DIM).reshape(1, SSM_INNER)
    return pl.pallas_call(
        _ssd_prompt_kernel,
        grid=(b, nc),
        in_specs=[col(SSM_INNER, SSM_OFF_Z), col(SSM_INNER, SSM_OFF_X), col(SSM_BC_DIM, SSM_OFF_B),
                  col(SSM_BC_DIM, SSM_OFF_C), col(LANES, SSM_OFF_DT),
                  _resident((CONV_WIDTH, SSM_CONV_DIM)), _resident((1, SSM_CONV_DIM)),
                  row, row, _resident((1, SSM_INNER)), _resident((1, SSM_INNER)),
                  _resident(sel_hd.shape), _resident(sel_c.shape)],
        out_specs=[pl.BlockSpec((None, chunk, SSM_INNER), lambda i, j: (i, j, 0)),
                   pl.BlockSpec((None, SSM_HEADS, SSM_STATE, SSM_HEAD_DIM), lambda i, j: (i, 0, 0, 0))],
        out_shape=[jax.ShapeDtypeStruct((b, l, SSM_INNER), BF16),
                   jax.ShapeDtypeStruct((b, SSM_HEADS, SSM_STATE, SSM_HEAD_DIM), F32)],
        scratch_shapes=[pltpu.VMEM((SSM_GROUPS, SSM_STATE, SSM_GROUP_COLS), F32),
                        pltpu.VMEM((SUBLANES, SSM_INNER), F32),
                        pltpu.VMEM((SUBLANES, SSM_BC_DIM), F32), pltpu.VMEM((SUBLANES, SSM_BC_DIM), F32)],
        compiler_params=_cparams(("parallel", "arbitrary")),
        name="ssd_prompt",
    )(proj, proj, proj, proj, proj, conv_w, conv_b.reshape(1, SSM_CONV_DIM), dtb_row, alog_row, dskip_row,
      norm_w.reshape(1, SSM_INNER), sel_hd, sel_c)


def _conv_step(x, buf_ref, w, bias=None):
    acc = x * w[CONV_WIDTH - 1:CONV_WIDTH, :]
    for i in range(CONV_WIDTH - 1):
        acc = acc + buf_ref[:, i, :] * w[i:i + 1, :]
    if bias is not None:
        acc = acc + bias
    return acc


def _conv_state_step(x, buf_ref, out_ref):
    for i in range(CONV_WIDTH - 2):
        out_ref[:, i, :] = buf_ref[:, i + 1, :]
    out_ref[:, CONV_WIDTH - 2, :] = x


def _gdn_decode_kernel(q_ref, k_ref, v_ref, z_ref, ba_ref, bq_ref, bk_ref, bv_ref, s_ref,
                       cwq_ref, cwk_ref, cwv_ref, alog_ref, dtb_ref, nw_ref,
                       o_ref, s_out_ref, cq_ref, ck_ref, cv_ref):
    h = pl.program_id(1)
    xq, xk, xv = q_ref[...], k_ref[...], v_ref[...]
    q = _silu(_conv_step(xq, bq_ref, cwq_ref[...]))
    k = _silu(_conv_step(xk, bk_ref, cwk_ref[...]))
    v = _silu(_conv_step(xv, bv_ref, cwv_ref[...]))
    _conv_state_step(xq, bq_ref, cq_ref)
    _conv_state_step(xk, bk_ref, ck_ref)
    _conv_state_step(xv, bv_ref, cv_ref)
    hd = GDN_DIM
    q = q * lax.rsqrt(jnp.sum(q * q, axis=-1, keepdims=True) + NORM_EPS) * (hd ** -0.5)
    k = k * lax.rsqrt(jnp.sum(k * k, axis=-1, keepdims=True) + NORM_EPS)
    ba = ba_ref[...]
    lane = _iota((1, LANES), 1)
    beta = jnp.sum(jnp.where(lane == h, _sigmoid(ba), 0.0), axis=1, keepdims=True)
    g = -jnp.exp(alog_ref[...]) * _softplus(ba + dtb_ref[...])
    eg = jnp.exp(jnp.sum(jnp.where(lane == GDN_HEADS + h, g, 0.0), axis=1, keepdims=True))
    qk = jnp.sum(q * k, axis=-1, keepdims=True)
    k_t = _transpose(k)
    q_t = _transpose(q)
    tb = xq.shape[0]
    ys = []
    for b in range(tb):
        s = s_ref[b, 0]
        k_col = k_t[:, b:b + 1]
        q_col = q_t[:, b:b + 1]
        ks = jnp.sum(k_col * s, axis=0, keepdims=True)
        egb = eg[b:b + 1, :]
        delta = beta[b:b + 1, :] * (v[b:b + 1, :] - egb * ks)
        s_new = egb * s + k_col * delta
        s_out_ref[b, 0] = s_new
        ys.append(jnp.sum(q_col * s_new, axis=0, keepdims=True))
    y = jnp.concatenate(ys, axis=0)
    y = _rms_rows(y, nw_ref[...]) * _silu(z_ref[...])
    o_ref[...] = y.astype(o_ref.dtype)


def gdn_decode(proj, state, conv_state, conv_w, alog_row, dtb_row, norm_w, tb=8):
    b = proj.shape[0]
    hd = GDN_DIM
    nh = GDN_HEADS
    pcol = lambda off: pl.BlockSpec((tb, hd), lambda i, h: (i, off // hd + h))
    ccol = lambda part: pl.BlockSpec((tb, CONV_WIDTH - 1, hd), lambda i, h: (i, 0, part * nh + h))
    wcol = lambda part: pl.BlockSpec((CONV_WIDTH, hd), lambda i, h: (0, part * nh + h))
    row = _resident((1, LANES))
    conv_out = jax.ShapeDtypeStruct((b, CONV_WIDTH - 1, GDN_OUT_DIM), F32)
    conv_out_spec = pl.BlockSpec((tb, CONV_WIDTH - 1, hd), lambda i, h: (i, 0, h))
    o, s_new, cq, ck, cv = pl.pallas_call(
        _gdn_decode_kernel,
        grid=(b // tb, nh),
        in_specs=[pcol(HYB_OFF_QKV), pcol(HYB_OFF_QKV + GDN_OUT_DIM), pcol(HYB_OFF_QKV + 2 * GDN_OUT_DIM),
                  pcol(HYB_OFF_Z), pl.BlockSpec((tb, LANES), lambda i, h: (i, HYB_OFF_BA // LANES)),
                  ccol(0), ccol(1), ccol(2),
                  pl.BlockSpec((tb, 1, hd, hd), lambda i, h: (i, h, 0, 0)),
                  wcol(0), wcol(1), wcol(2), row, row, _resident((1, hd))],
        out_specs=[pl.BlockSpec((tb, hd), lambda i, h: (i, h)),
                   pl.BlockSpec((tb, 1, hd, hd), lambda i, h: (i, h, 0, 0)),
                   conv_out_spec, conv_out_spec, conv_out_spec],
        out_shape=[jax.ShapeDtypeStruct((b, GDN_OUT_DIM), BF16),
                   jax.ShapeDtypeStruct(state.shape, F32), conv_out, conv_out, conv_out],
        compiler_params=_cparams(("parallel", "parallel")),
        name="gdn_decode",
    )(proj, proj, proj, proj, proj, conv_state, conv_state, conv_state, state,
      conv_w, conv_w, conv_w, alog_row, dtb_row, norm_w.reshape(1, hd))
    return o, s_new, jnp.concatenate([cq, ck, cv], axis=-1)


def _ret_decode_kernel(q_ref, k_ref, v_ref, gate_ref, s_ref, cos_ref, sin_ref, nw_ref, o_ref, s_out_ref):
    q = _rotary(q_ref[...], cos_ref[...], sin_ref[...])
    k = _rotary(k_ref[...], cos_ref[...], sin_ref[...]) * (RET_KEY_DIM ** -0.5)
    q_t = _transpose(q)
    k_t = _transpose(k)
    tb = q.shape[0]
    kd, vd = RET_KEY_DIM, RET_VAL_DIM
    rows = []
    for b in range(tb):
        ys = []
        for h in range(RET_HEADS):
            gamma = math.exp(_ret_log_gamma(h))
            s = s_ref[b, h]
            s_new = s * gamma + k_t[h * kd:(h + 1) * kd, b:b + 1] * v_ref[b:b + 1, h * vd:(h + 1) * vd]
            s_out_ref[b, h] = s_new
            ys.append(jnp.sum(q_t[h * kd:(h + 1) * kd, b:b + 1] * s_new, axis=0, keepdims=True))
        rows.append(jnp.concatenate(ys, axis=1))
    y = jnp.concatenate(rows, axis=0)
    for h in range(RET_HEADS):
        yh = (_group_norm_rows(y[:, h * vd:(h + 1) * vd]) * nw_ref[:, h * vd:(h + 1) * vd]
              * _silu(gate_ref[:, h * vd:(h + 1) * vd]))
        o_ref[:, h * vd:(h + 1) * vd] = yh.astype(o_ref.dtype)


def ret_decode(proj, state, cos_f, sin_s, norm_w, tb=8):
    b = proj.shape[0]
    pcol = lambda width, off: pl.BlockSpec((tb, width), lambda i: (i, off // width))
    sspec = pl.BlockSpec((tb, RET_HEADS, RET_KEY_DIM, RET_VAL_DIM), lambda i: (i, 0, 0, 0))
    return pl.pallas_call(
        _ret_decode_kernel,
        grid=(b // tb,),
        in_specs=[pcol(RET_QK_DIM, HYB_OFF_RQ), pcol(RET_QK_DIM, HYB_OFF_RK), pcol(RET_OUT_DIM, HYB_OFF_RV),
                  pcol(RET_OUT_DIM, HYB_OFF_RG), sspec, _resident((1, RET_QK_DIM)), _resident((1, RET_QK_DIM)),
                  _resident((1, RET_OUT_DIM))],
        out_specs=[pl.BlockSpec((tb, RET_OUT_DIM), lambda i: (i, 0)), sspec],
        out_shape=[jax.ShapeDtypeStruct((b, RET_OUT_DIM), BF16), jax.ShapeDtypeStruct(state.shape, F32)],
        compiler_params=_cparams(("parallel",)),
        name="ret_decode",
    )(proj, proj, proj, proj, state, cos_f, sin_s, norm_w.reshape(1, RET_OUT_DIM))


def _ssd_decode_kernel(z_ref, x_ref, b_ref, c_ref, dt_ref, bx_ref, bb_ref, bc_ref, s_ref,
                       cwx_ref, cwb_ref, cwc_ref, cbx_ref, cbb_ref, cbc_ref,
                       dtb_ref, alog_ref, dskip_ref, nw_ref,
                       o_ref, s_out_ref, cx_ref, cb_ref, cc_ref):
    grp = pl.program_id(1)
    xr, br, cr = x_ref[...], b_ref[...], c_ref[...]
    x = _silu(_conv_step(xr, bx_ref, cwx_ref[...], cbx_ref[...]))
    bm = _silu(_conv_step(br, bb_ref, cwb_ref[...], cbb_ref[...]))
    cm = _silu(_conv_step(cr, bc_ref, cwc_ref[...], cbc_ref[...]))
    _conv_state_step(xr, bx_ref, cx_ref)
    _conv_state_step(br, bb_ref, cb_ref)
    _conv_state_step(cr, bc_ref, cc_ref)
    tb = xr.shape[0]
    hd, gw = SSM_HEAD_DIM, SSM_GROUP_COLS
    dt = _softplus(dt_ref[...] + dtb_ref[...])
    spread = (_iota((LANES, gw), 0) == grp * SSM_HPG + _iota((LANES, gw), 1) // hd).astype(BF16)
    per_head = jnp.concatenate([dt, -jnp.exp(alog_ref[...]), dskip_ref[...],
                                jnp.zeros((2 * SUBLANES - tb - 2, LANES), F32)], axis=0)
    ph_hi, ph_lo = _split(per_head)
    per_col = (jnp.dot(ph_hi, spread, preferred_element_type=F32)
               + jnp.dot(ph_lo, spread, preferred_element_type=F32))
    dt_x, a_x, dskip_x = per_col[0:tb], per_col[tb:tb + 1], per_col[tb + 1:tb + 2]
    eg_x = jnp.exp(dt_x * a_x)
    xdt = x * dt_x
    b_t = _transpose(bm)
    c_t = _transpose(cm)
    rows = []
    for b in range(tb):
        b_col = b_t[:, b:b + 1]
        c_col = c_t[:, b:b + 1]
        s_new = [s_ref[b, hl] * eg_x[b:b + 1, hl * hd:(hl + 1) * hd]
                 + b_col * xdt[b:b + 1, hl * hd:(hl + 1) * hd] for hl in range(SSM_HPG)]
        for hl in range(SSM_HPG):
            s_out_ref[b, hl] = s_new[hl]
        rows.append(jnp.concatenate([jnp.sum(c_col * s, axis=0, keepdims=True) for s in s_new], axis=1))
    y = (jnp.concatenate(rows, axis=0) + dskip_x * x) * _silu(z_ref[...])
    o_ref[...] = _rms_rows(y, nw_ref[...]).astype(o_ref.dtype)


def ssd_decode(proj, state, conv_state, conv_w, conv_b, dtb_row, alog_row, dskip_row, norm_w, tb=8):
    b = proj.shape[0]
    gw = SSM_GROUP_COLS
    ns = SSM_STATE
    cb = conv_b.reshape(1, SSM_CONV_DIM)
    pcol = lambda width, off: pl.BlockSpec((tb, width), lambda i, g: (i, off // width + g))
    ccol = lambda width, off: pl.BlockSpec((tb, CONV_WIDTH - 1, width), lambda i, g: (i, 0, off // width + g))
    wcol = lambda rows, width, off: pl.BlockSpec((rows, width), lambda i, g: (0, off // width + g))
    row = _resident((1, LANES))
    sspec = pl.BlockSpec((tb, SSM_HPG, ns, SSM_HEAD_DIM), lambda i, g: (i, g, 0, 0))
    off_b, off_c = SSM_INNER, SSM_INNER + SSM_BC_DIM
    o, s_new, cx, cbs, ccs = pl.pallas_call(
        _ssd_decode_kernel,
        grid=(b // tb, SSM_GROUPS),
        in_specs=[pcol(gw, SSM_OFF_Z), pcol(gw, SSM_OFF_X), pcol(ns, SSM_OFF_B), pcol(ns, SSM_OFF_C),
                  pl.BlockSpec((tb, LANES), lambda i, g: (i, SSM_OFF_DT // LANES)),
                  ccol(gw, 0), ccol(ns, off_b), ccol(ns, off_c), sspec,
                  wcol(CONV_WIDTH, gw, 0), wcol(CONV_WIDTH, ns, off_b), wcol(CONV_WIDTH, ns, off_c),
                  wcol(1, gw, 0), wcol(1, ns, off_b), wcol(1, ns, off_c),
                  row, row, row, wcol(1, gw, 0)],
        out_specs=[pl.BlockSpec((tb, gw), lambda i, g: (i, g)), sspec,
                   pl.BlockSpec((tb, CONV_WIDTH - 1, gw), lambda i, g: (i, 0, g)),
                   pl.BlockSpec((tb, CONV_WIDTH - 1, ns), lambda i, g: (i, 0, g)),
                   pl.BlockSpec((tb, CONV_WIDTH - 1, ns), lambda i, g: (i, 0, g))],
        out_shape=[jax.ShapeDtypeStruct((b, SSM_INNER), BF16), jax.ShapeDtypeStruct(state.shape, F32),
                   jax.ShapeDtypeStruct((b, CONV_WIDTH - 1, SSM_INNER), F32),
                   jax.ShapeDtypeStruct((b, CONV_WIDTH - 1, SSM_BC_DIM), F32),
                   jax.ShapeDtypeStruct((b, CONV_WIDTH - 1, SSM_BC_DIM), F32)],
        compiler_params=_cparams(("parallel", "parallel")),
        name="ssd_decode",
    )(proj, proj, proj, proj, proj, conv_state, conv_state, conv_state, state,
      conv_w, conv_w, conv_w, cb, cb, cb, dtb_row, alog_row, dskip_row, norm_w.reshape(1, SSM_INNER))
    return o, s_new, jnp.concatenate([cx, cbs, ccs], axis=-1)


def _lane_row(vals, offset):
    return jnp.zeros((1, LANES), F32).at[0, offset:offset + vals.shape[0]].set(vals.astype(F32))


def _prep_hyb_w_in(w):
    sizes = (GDN_QKV_DIM, GDN_OUT_DIM, GDN_HEADS, GDN_HEADS, RET_QK_DIM, RET_QK_DIM, RET_OUT_DIM, RET_OUT_DIM)
    offs = [0]
    for s in sizes:
        offs.append(offs[-1] + s)
    part = lambda i: w[:, offs[i]:offs[i + 1]]
    cols = [part(0), part(1), part(4), part(5), part(6), part(7), part(2), part(3)]
    out = jnp.concatenate(cols, axis=1)
    return jnp.pad(out, ((0, 0), (0, HYB_N - out.shape[1]))).astype(BF16)


def _prep_ssm_w_in(w):
    return jnp.pad(w, ((0, 0), (0, SSM_N - w.shape[1]))).astype(BF16)


def _rope_tables(pos):
    half = RET_KEY_DIM // 2
    inv_freq = ROPE_BASE ** (-jnp.arange(half, dtype=F32) / half)
    ang = pos.astype(F32)[:, None] * inv_freq[None, :]
    cos, sin = jnp.cos(ang), jnp.sin(ang)
    cos_f = jnp.tile(jnp.concatenate([cos, cos], axis=1), (1, RET_HEADS))
    sin_s = jnp.tile(jnp.concatenate([-sin, sin], axis=1), (1, RET_HEADS))
    return cos_f, sin_s


def _trunk(x, pos, states, params, prompt):
    bsz, l, d = x.shape
    m = bsz * l
    tm = 512 if m % 512 == 0 else m
    x2 = x.reshape(m, d)
    cos_f, sin_s = _rope_tables(pos)
    s_gdn, c_gdn, s_ret, s_ssm, c_ssm = states
    new_gdn, new_gdn_conv, new_ret, new_ssm, new_ssm_conv = [], [], [], [], []
    for layer in range(DEPTH):
        i = layer // 2
        p = params
        if layer % 2 == 0:
            proj = norm_matmul(x2, p["norm_mix"][layer], p["w_in_hyb"][i], tm)
            alog_row = _lane_row(p["gdn_a_log"][i], GDN_HEADS)
            dtb_row = _lane_row(p["gdn_dt_bias"][i], GDN_HEADS)
            if prompt:
                proj3 = proj.reshape(bsz, l, HYB_N)
                o_a, sg = gdn_prompt(proj3, p["gdn_conv_w"][i], alog_row, dtb_row, p["gdn_norm_w"][i])
                o_b, sr = ret_prompt(proj3, cos_f, sin_s, p["ret_norm_w"][i])
                cg = proj3[:, l - (CONV_WIDTH - 1):, :GDN_QKV_DIM]
                o_a, o_b = o_a.reshape(m, GDN_OUT_DIM), o_b.reshape(m, RET_OUT_DIM)
            else:
                o_a, sg, cg = gdn_decode(proj, s_gdn[i], c_gdn[i], p["gdn_conv_w"][i], alog_row, dtb_row,
                                         p["gdn_norm_w"][i])
                o_b, sr = ret_decode(proj, s_ret[i], cos_f, sin_s, p["ret_norm_w"][i])
            new_gdn.append(sg)
            new_gdn_conv.append(cg)
            new_ret.append(sr)
            w_out = p["w_out_hyb"][i]
            x2 = out_proj_residual([o_a, o_b], [w_out[:GDN_OUT_DIM], w_out[GDN_OUT_DIM:]], x2, tm)
        else:
            proj = norm_matmul(x2, p["norm_mix"][layer], p["w_in_ssm"][i], tm)
            dtb_row = _lane_row(p["ssm_dt_bias"][i], 0)
            alog_row = _lane_row(p["ssm_a_log"][i], 0)
            dskip_row = _lane_row(p["ssm_d"][i], 0)
            if prompt:
                proj3 = proj.reshape(bsz, l, SSM_N)
                y, ss = ssd_prompt(proj3, p["ssm_conv_w"][i], p["ssm_conv_b"][i], dtb_row, alog_row,
                                   dskip_row, p["ssm_norm_w"][i])
                cs = proj3[:, l - (CONV_WIDTH - 1):, SSM_OFF_X:SSM_OFF_X + SSM_CONV_DIM]
                y = y.reshape(m, SSM_INNER)
            else:
                y, ss, cs = ssd_decode(proj, s_ssm[i], c_ssm[i], p["ssm_conv_w"][i], p["ssm_conv_b"][i],
                                       dtb_row, alog_row, dskip_row, p["ssm_norm_w"][i])
            new_ssm.append(ss)
            new_ssm_conv.append(cs)
            x2 = out_proj_residual([y], [p["w_out_ssm"][i]], x2, tm)
        x2 = mlp_residual(x2, p["norm_mlp"][layer], p["mlp_w1"][layer], p["mlp_w2"][layer],
                          p["norm_final"], layer == DEPTH - 1, tm)
    return (x2.reshape(bsz, l, d), jnp.stack(new_gdn), jnp.stack(new_gdn_conv), jnp.stack(new_ret),
            jnp.stack(new_ssm), jnp.stack(new_ssm_conv))


def kernel(x_prompt, x_sample, state_gdn, state_gdn_conv, state_ret, state_ssm, state_ssm_conv, norm_mix, norm_mlp, norm_final, w_in_hyb, gdn_conv_w, gdn_a_log, gdn_dt_bias, gdn_norm_w, ret_norm_w, w_out_hyb, w_in_ssm, ssm_conv_w, ssm_conv_b, ssm_dt_bias, ssm_a_log, ssm_d, ssm_norm_w, w_out_ssm, mlp_w1, mlp_w2):
    n_hyb, n_ssm = w_in_hyb.shape[0], w_in_ssm.shape[0]
    params = dict(
        norm_mix=norm_mix, norm_mlp=norm_mlp, norm_final=norm_final,
        w_in_hyb=[_prep_hyb_w_in(w_in_hyb[i]) for i in range(n_hyb)],
        gdn_conv_w=gdn_conv_w, gdn_a_log=gdn_a_log, gdn_dt_bias=gdn_dt_bias, gdn_norm_w=gdn_norm_w,
        ret_norm_w=ret_norm_w, w_out_hyb=w_out_hyb.astype(BF16),
        w_in_ssm=[_prep_ssm_w_in(w_in_ssm[i]) for i in range(n_ssm)],
        ssm_conv_w=ssm_conv_w, ssm_conv_b=ssm_conv_b, ssm_dt_bias=ssm_dt_bias, ssm_a_log=ssm_a_log,
        ssm_d=ssm_d, ssm_norm_w=ssm_norm_w, w_out_ssm=w_out_ssm.astype(BF16),
        mlp_w1=mlp_w1.astype(BF16), mlp_w2=mlp_w2.astype(BF16))
    lp, ls = x_prompt.shape[1], x_sample.shape[1]
    pos_prompt = jnp.arange(lp, dtype=jnp.int32)
    pos_sample = PAST_LEN + jnp.arange(ls, dtype=jnp.int32)
    y_p, p_gdn, p_gdn_conv, p_ret, p_ssm, p_ssm_conv = _trunk(
        x_prompt, pos_prompt, (None,) * 5, params, prompt=True)
    y_s, s_gdn, s_gdn_conv, s_ret, s_ssm, s_ssm_conv = _trunk(
        x_sample, pos_sample, (state_gdn, state_gdn_conv, state_ret, state_ssm, state_ssm_conv),
        params, prompt=False)
    return (y_p, y_s, p_gdn, p_gdn_conv, p_ret, p_ssm, p_ssm_conv,
            s_gdn, s_gdn_conv, s_ret, s_ssm, s_ssm_conv)
```

```python
import functools
import math

import jax
import jax.numpy as jnp
from jax import lax
from jax.experimental import pallas as pl
from jax.experimental.pallas import tpu as pltpu

F32 = jnp.float32
BF16 = jnp.bfloat16

D_MODEL = 1024
DEPTH = 4
CONV_WIDTH = 4
NORM_EPS = 1e-6
PAST_LEN = 16384

GDN_HEADS = 4
GDN_DIM = 128
GDN_QKV_DIM = 3 * GDN_HEADS * GDN_DIM
GDN_OUT_DIM = GDN_HEADS * GDN_DIM

RET_HEADS = 4
RET_KEY_DIM = 64
RET_VAL_DIM = 128
RET_QK_DIM = RET_HEADS * RET_KEY_DIM
RET_OUT_DIM = RET_HEADS * RET_VAL_DIM
ROPE_BASE = 10000.0

SSM_INNER = 2 * D_MODEL
SSM_HEAD_DIM = 64
SSM_HEADS = SSM_INNER // SSM_HEAD_DIM
SSM_GROUPS = 4
SSM_HPG = SSM_HEADS // SSM_GROUPS
SSM_STATE = 128
SSM_GROUP_COLS = SSM_HPG * SSM_HEAD_DIM
SSM_BC_DIM = SSM_GROUPS * SSM_STATE
SSM_CONV_DIM = SSM_INNER + 2 * SSM_BC_DIM
MLP_HIDDEN = 4 * D_MODEL

LANES = 128
SUBLANES = 8

HYB_OFF_QKV = 0
HYB_OFF_Z = GDN_QKV_DIM
HYB_OFF_RQ = HYB_OFF_Z + GDN_OUT_DIM
HYB_OFF_RK = HYB_OFF_RQ + RET_QK_DIM
HYB_OFF_RV = HYB_OFF_RK + RET_QK_DIM
HYB_OFF_RG = HYB_OFF_RV + RET_OUT_DIM
HYB_OFF_BA = HYB_OFF_RG + RET_OUT_DIM
HYB_N = 3840
SSM_OFF_Z = 0
SSM_OFF_X = SSM_INNER
SSM_OFF_B = SSM_OFF_X + SSM_INNER
SSM_OFF_C = SSM_OFF_B + SSM_BC_DIM
SSM_OFF_DT = SSM_OFF_C + SSM_BC_DIM
SSM_N = 5376
PROJ_TN = 768

VMEM_LIMIT = 56 * 1024 * 1024

_NN = (((1,), (0,)), ((), ()))
_NT = (((1,), (1,)), ((), ()))
_TN = (((0,), (0,)), ((), ()))


def _dot(a, b, dims=_NN):
    return lax.dot_general(a.astype(BF16), b.astype(BF16), dims, preferred_element_type=F32)


def _dot_f32(a, b, dims=_NN):
    return lax.dot_general(a, b, dims, precision=lax.Precision.HIGHEST, preferred_element_type=F32)


def _sigmoid(x):
    return 1.0 / (1.0 + jnp.exp(-x))


def _silu(x):
    return x * _sigmoid(x)


def _softplus(x):
    return jnp.maximum(x, 0.0) + jnp.log(1.0 + jnp.exp(-jnp.abs(x)))


def _iota(shape, dim):
    return lax.broadcasted_iota(jnp.int32, shape, dim)


def _eye(n):
    return (_iota((n, n), 0) == _iota((n, n), 1)).astype(F32)


def _transpose(x):
    return _dot_f32(_eye(x.shape[1]), x, _NT)


def _cparams(sem):
    return pltpu.CompilerParams(dimension_semantics=sem, vmem_limit_bytes=VMEM_LIMIT)


def _resident(shape):
    nd = len(shape)
    return pl.BlockSpec(shape, lambda *_: (0,) * nd, pipeline_mode=pl.Buffered(1))


def _rms_rows(x, gain):
    return x * lax.rsqrt(jnp.mean(x * x, axis=-1, keepdims=True) + NORM_EPS) * gain


def _norm_matmul_kernel(x_ref, g_ref, w_ref, o_ref, *, tn):
    xn = _rms_rows(x_ref[...], g_ref[...]).astype(BF16)
    for j in range(w_ref.shape[1] // tn):
        o_ref[:, j * tn:(j + 1) * tn] = jnp.dot(xn, w_ref[:, j * tn:(j + 1) * tn],
                                                preferred_element_type=F32)


def norm_matmul(x, gain, w, tm):
    m, d = x.shape
    n = w.shape[1]
    return pl.pallas_call(
        functools.partial(_norm_matmul_kernel, tn=PROJ_TN),
        grid=(m // tm,),
        in_specs=[pl.BlockSpec((tm, d), lambda i: (i, 0)), _resident((1, d)), _resident((d, n))],
        out_specs=pl.BlockSpec((tm, n), lambda i: (i, 0)),
        out_shape=jax.ShapeDtypeStruct((m, n), F32),
        compiler_params=_cparams(("parallel",)),
        name="norm_matmul",
    )(x, gain.reshape(1, d), w)


def _out_proj_kernel(*refs, n_in):
    a_refs, w_refs, r_ref, o_ref = refs[:n_in], refs[n_in:2 * n_in], refs[2 * n_in], refs[2 * n_in + 1]
    acc = r_ref[...]
    for a_ref, w_ref in zip(a_refs, w_refs):
        acc = acc + jnp.dot(a_ref[...], w_ref[...], preferred_element_type=F32)
    o_ref[...] = acc


def out_proj_residual(acts, ws, res, tm):
    m, d = res.shape
    n_in = len(acts)
    in_specs = ([pl.BlockSpec((tm, a.shape[1]), lambda i: (i, 0)) for a in acts]
                + [_resident(w.shape) for w in ws]
                + [pl.BlockSpec((tm, d), lambda i: (i, 0))])
    return pl.pallas_call(
        functools.partial(_out_proj_kernel, n_in=n_in),
        grid=(m // tm,),
        in_specs=in_specs,
        out_specs=pl.BlockSpec((tm, d), lambda i: (i, 0)),
        out_shape=jax.ShapeDtypeStruct((m, d), F32),
        compiler_params=_cparams(("parallel",)),
        name="out_proj",
    )(*acts, *ws, res)


def _mlp_kernel(x_ref, g_ref, w1_ref, w2_ref, gf_ref, o_ref, *, tf, final_norm):
    x = x_ref[...]
    xn = _rms_rows(x, g_ref[...]).astype(BF16)
    acc = x
    for j in range(w1_ref.shape[1] // tf):
        h = jnp.dot(xn, w1_ref[:, j * tf:(j + 1) * tf], preferred_element_type=F32)
        h = jnp.maximum(h, 0.0)
        acc = acc + jnp.dot((h * h).astype(BF16), w2_ref[j * tf:(j + 1) * tf, :],
                            preferred_element_type=F32)
    if final_norm:
        acc = _rms_rows(acc, gf_ref[...])
    o_ref[...] = acc


def mlp_residual(x, gain, w1, w2, final_gain, final_norm, tm):
    m, d = x.shape
    f = w1.shape[1]
    return pl.pallas_call(
        functools.partial(_mlp_kernel, tf=1024, final_norm=final_norm),
        grid=(m // tm,),
        in_specs=[pl.BlockSpec((tm, d), lambda i: (i, 0)), _resident((1, d)), _resident((d, f)),
                  _resident((f, d)), _resident((1, d))],
        out_specs=pl.BlockSpec((tm, d), lambda i: (i, 0)),
        out_shape=jax.ShapeDtypeStruct((m, d), F32),
        compiler_params=_cparams(("parallel",)),
        name="mlp",
    )(x, gain.reshape(1, d), w1, w2, final_gain.reshape(1, d))


def _causal_conv_chunk(x, tail, w, bias=None):
    c = x.shape[0]
    acc = x * w[CONV_WIDTH - 1:CONV_WIDTH, :]
    row = _iota((SUBLANES, x.shape[1]), 0)
    for k in range(1, CONV_WIDTH):
        xr = pltpu.roll(x, k, 0)
        tr = pltpu.roll(tail, k, 0)
        head = jnp.where(row < k, tr, xr[0:SUBLANES])
        shifted = jnp.concatenate([head, xr[SUBLANES:c]], axis=0)
        acc = acc + shifted * w[CONV_WIDTH - 1 - k:CONV_WIDTH - k, :]
    if bias is not None:
        acc = acc + bias
    return acc


def _cumsum_rows(g):
    c = g.shape[0]
    tri = (_iota((c, c), 0) >= _iota((c, c), 1)).astype(F32)
    return _dot_f32(tri, g)


def _decay_matrix(gc_col, gc_row):
    c = gc_col.shape[0]
    causal = _iota((c, c), 0) >= _iota((c, c), 1)
    return jnp.where(causal, jnp.exp(jnp.where(causal, gc_col - gc_row, 0.0)), 0.0)


def _split(x):
    hi = x.astype(BF16)
    return hi, (x - hi.astype(F32)).astype(BF16)


def _dot3(a, b, dims=_NN):
    (ah, al), (bh, bl) = a, b
    d = lambda p, q: lax.dot_general(p, q, dims, preferred_element_type=F32)
    return d(ah, bh) + (d(ah, bl) + d(al, bh))


def _unit_lower_inverses(n_list):
    c = n_list[0].shape[0]
    ri, ci = _iota((c, c), 0), _iota((c, c), 1)
    eye = (ri == ci).astype(F32)
    blk = 16
    same = (ri // blk) == (ci // blk)
    a = [jnp.where(same, -n, 0.0) for n in n_list]
    inv = [eye + x for x in a]
    p = [_split(x) for x in a]
    for _ in range(3):
        p = [_split(_dot3(x, x)) for x in p]
        inv = [i + _dot3(_split(i), x) for i, x in zip(inv, p)]
    while blk < c:
        pair = ((ri // (2 * blk)) == (ci // (2 * blk))) & ((ri // blk) != (ci // blk))
        off = [_split(jnp.where(pair, n, 0.0)) for n in n_list]
        inv_s = [_split(i) for i in inv]
        t = [_split(_dot3(o, i)) for o, i in zip(off, inv_s)]
        inv = [i - _dot3(i_s, x) for i, i_s, x in zip(inv, inv_s, t)]
        blk *= 2
    return inv


def _gdn_prompt_kernel(qkv_ref, z_ref, ba_ref, cw_ref, alog_ref, dtb_ref, nw_ref,
                       o_ref, s_out_ref, s_ref, tail_ref):
    c_idx = pl.program_id(1)

    @pl.when(c_idx == 0)
    def _():
        s_ref[...] = jnp.zeros_like(s_ref)
        tail_ref[...] = jnp.zeros_like(tail_ref)

    nb, c = qkv_ref.shape[0], qkv_ref.shape[1]
    hd = GDN_DIM
    ri, ci = _iota((c, c), 0), _iota((c, c), 1)
    strict = ri > ci
    chains = [(bb, h) for bb in range(nb) for h in range(GDN_HEADS)]

    conv, beta, gcum, gcum_t = [], [], [], []
    for bb in range(nb):
        x = qkv_ref[bb]
        conv.append(_silu(_causal_conv_chunk(x, tail_ref[bb], cw_ref[...])))
        tail_ref[bb] = x[c - SUBLANES:c]
        ba = ba_ref[bb]
        beta.append(_sigmoid(ba))
        gc = _cumsum_rows(-jnp.exp(alog_ref[...]) * _softplus(ba + dtb_ref[...]))
        gcum.append(gc)
        gcum_t.append(_transpose(gc))

    q, k, v, b_col, gc_col, g_end, decay = [], [], [], [], [], [], []
    for bb, h in chains:
        qh = conv[bb][:, h * hd:(h + 1) * hd]
        kh = conv[bb][:, GDN_OUT_DIM + h * hd:GDN_OUT_DIM + (h + 1) * hd]
        q.append(qh * lax.rsqrt(jnp.sum(qh * qh, axis=-1, keepdims=True) + NORM_EPS) * (hd ** -0.5))
        k.append(kh * lax.rsqrt(jnp.sum(kh * kh, axis=-1, keepdims=True) + NORM_EPS))
        v.append(conv[bb][:, 2 * GDN_OUT_DIM + h * hd:2 * GDN_OUT_DIM + (h + 1) * hd])
        b_col.append(beta[bb][:, h:h + 1])
        gc_col.append(gcum[bb][:, GDN_HEADS + h:GDN_HEADS + h + 1])
        g_end.append(gcum[bb][c - 1:c, GDN_HEADS + h:GDN_HEADS + h + 1])
        decay.append(_decay_matrix(gc_col[-1], gcum_t[bb][GDN_HEADS + h:GDN_HEADS + h + 1, :]))
    kk = [_dot(x, x, _NT) for x in k]
    qk = [_dot(x, y, _NT) * d for x, y, d in zip(q, k, decay)]
    t_inv = _unit_lower_inverses([jnp.where(strict, b * x * d, 0.0) for b, x, d in zip(b_col, kk, decay)])
    rhs = [_split(jnp.concatenate([vv * b, kx * (b * jnp.exp(g))], axis=1))
           for vv, kx, b, g in zip(v, k, b_col, gc_col)]
    sol = [_dot3(_split(t), r) for t, r in zip(t_inv, rhs)]
    s_old = [s_ref[bb, h] for bb, h in chains]
    delta = [x[:, :hd] - _dot(x[:, hd:], s) for x, s in zip(sol, s_old)]
    y = [_dot(x * jnp.exp(g), s) + _dot(a, dl)
         for x, g, s, a, dl in zip(q, gc_col, s_old, qk, delta)]
    s_new = [s * jnp.exp(ge) + _dot(kx * jnp.exp(ge - g), dl, _TN)
             for s, ge, kx, g, dl in zip(s_old, g_end, k, gc_col, delta)]
    for (bb, h), sn, yy in zip(chains, s_new, y):
        s_ref[bb, h] = sn
        out = _rms_rows(yy, nw_ref[...]) * _silu(z_ref[bb, :, h * hd:(h + 1) * hd])
        o_ref[bb, :, h * hd:(h + 1) * hd] = out.astype(o_ref.dtype)

    @pl.when(c_idx == pl.num_programs(1) - 1)
    def _():
        s_out_ref[...] = s_ref[...]


def gdn_prompt(proj, conv_w, alog_row, dtb_row, norm_w, chunk=64, nb=2):
    b, l, _ = proj.shape
    nc = l // chunk
    col = lambda width, off: pl.BlockSpec((nb, chunk, width), lambda i, j: (i, j, off // width))
    return pl.pallas_call(
        _gdn_prompt_kernel,
        grid=(b // nb, nc),
        in_specs=[col(GDN_QKV_DIM, HYB_OFF_QKV), col(GDN_OUT_DIM, HYB_OFF_Z), col(LANES, HYB_OFF_BA),
                  _resident((CONV_WIDTH, GDN_QKV_DIM)), _resident((1, LANES)), _resident((1, LANES)),
                  _resident((1, GDN_DIM))],
        out_specs=[pl.BlockSpec((nb, chunk, GDN_OUT_DIM), lambda i, j: (i, j, 0)),
                   pl.BlockSpec((nb, GDN_HEADS, GDN_DIM, GDN_DIM), lambda i, j: (i, 0, 0, 0))],
        out_shape=[jax.ShapeDtypeStruct((b, l, GDN_OUT_DIM), BF16),
                   jax.ShapeDtypeStruct((b, GDN_HEADS, GDN_DIM, GDN_DIM), F32)],
        scratch_shapes=[pltpu.VMEM((nb, GDN_HEADS, GDN_DIM, GDN_DIM), F32),
                        pltpu.VMEM((nb, SUBLANES, GDN_QKV_DIM), F32)],
        compiler_params=_cparams(("parallel", "arbitrary")),
        name="gdn_prompt",
    )(proj, proj, proj, conv_w, alog_row, dtb_row, norm_w.reshape(1, GDN_DIM))


def _ret_log_gamma(h):
    return math.log(1.0 - 2.0 ** (-5.0 - h))


def _rotary(t, cos_f, sin_s):
    n = t.shape[1]
    half = RET_KEY_DIM // 2
    first = (_iota(t.shape, 1) % RET_KEY_DIM) < half
    swapped = jnp.where(first, pltpu.roll(t, n - half, 1), pltpu.roll(t, half, 1))
    return t * cos_f + swapped * sin_s


def _group_norm_rows(x):
    xc = x - jnp.mean(x, axis=-1, keepdims=True)
    return xc * lax.rsqrt(jnp.mean(xc * xc, axis=-1, keepdims=True) + NORM_EPS)


def _ret_prompt_kernel(q_ref, k_ref, v_ref, gate_ref, cos_ref, sin_ref, nw_ref,
                       o_ref, s_out_ref, s_ref):
    c_idx = pl.program_id(1)

    @pl.when(c_idx == 0)
    def _():
        s_ref[...] = jnp.zeros_like(s_ref)

    q = _rotary(q_ref[...], cos_ref[...], sin_ref[...])
    k = _rotary(k_ref[...], cos_ref[...], sin_ref[...]) * (RET_KEY_DIM ** -0.5)
    c = q.shape[0]
    ri, ci = _iota((c, c), 0), _iota((c, c), 1)
    causal = ri >= ci
    pos_col = _iota((c, 1), 0).astype(F32)
    kd, vd = RET_KEY_DIM, RET_VAL_DIM
    for h in range(RET_HEADS):
        lg = _ret_log_gamma(h)
        decay = jnp.where(causal, jnp.exp(jnp.where(causal, (ri - ci).astype(F32) * lg, 0.0)), 0.0)
        qh = q[:, h * kd:(h + 1) * kd]
        kh = k[:, h * kd:(h + 1) * kd]
        vh = v_ref[:, h * vd:(h + 1) * vd]
        s = s_ref[h]
        y = _dot(_dot(qh, kh, _NT) * decay, vh) + _dot(qh, s) * jnp.exp((pos_col + 1.0) * lg)
        s_ref[h] = s * math.exp(c * lg) + _dot(kh * jnp.exp((c - 1.0 - pos_col) * lg), vh, _TN)
        y = (_group_norm_rows(y) * nw_ref[:, h * vd:(h + 1) * vd]
             * _silu(gate_ref[:, h * vd:(h + 1) * vd]))
        o_ref[:, h * vd:(h + 1) * vd] = y.astype(o_ref.dtype)

    @pl.when(c_idx == pl.num_programs(1) - 1)
    def _():
        s_out_ref[...] = s_ref[...]


def ret_prompt(proj, cos_f, sin_s, norm_w, chunk=128):
    b, l, _ = proj.shape
    nc = l // chunk
    col = lambda width, off: pl.BlockSpec((None, chunk, width), lambda i, j: (i, j, off // width))
    tab = pl.BlockSpec((chunk, RET_QK_DIM), lambda i, j: (j, 0))
    return pl.pallas_call(
        _ret_prompt_kernel,
        grid=(b, nc),
        in_specs=[col(RET_QK_DIM, HYB_OFF_RQ), col(RET_QK_DIM, HYB_OFF_RK), col(RET_OUT_DIM, HYB_OFF_RV),
                  col(RET_OUT_DIM, HYB_OFF_RG), tab, tab, _resident((1, RET_OUT_DIM))],
        out_specs=[pl.BlockSpec((None, chunk, RET_OUT_DIM), lambda i, j: (i, j, 0)),
                   pl.BlockSpec((None, RET_HEADS, RET_KEY_DIM, RET_VAL_DIM), lambda i, j: (i, 0, 0, 0))],
        out_shape=[jax.ShapeDtypeStruct((b, l, RET_OUT_DIM), BF16),
                   jax.ShapeDtypeStruct((b, RET_HEADS, RET_KEY_DIM, RET_VAL_DIM), F32)],
        scratch_shapes=[pltpu.VMEM((RET_HEADS, RET_KEY_DIM, RET_VAL_DIM), F32)],
        compiler_params=_cparams(("parallel", "arbitrary")),
        name="ret_prompt",
    )(proj, proj, proj, proj, cos_f, sin_s, norm_w.reshape(1, RET_OUT_DIM))


def _split3(x):
    a = x.astype(BF16)
    r = x - a.astype(F32)
    b = r.astype(BF16)
    return a, b, (r - b.astype(F32)).astype(BF16)


def _spread(x, sel):
    return sum(jnp.dot(p, sel, preferred_element_type=F32) for p in _split3(x))


def _ssd_prompt_kernel(z_ref, x_ref, b_ref, c_ref, dt_ref, cw_ref, cb_ref, dtb_ref, alog_ref, dskip_ref,
                       nw_ref, sel_hd_ref, sel_c_ref, o_ref, s_out_ref, s_ref, tx_ref, tb_ref, tc_ref):
    c_idx = pl.program_id(1)

    @pl.when(c_idx == 0)
    def _():
        s_ref[...] = jnp.zeros_like(s_ref)
        tx_ref[...] = jnp.zeros_like(tx_ref)
        tb_ref[...] = jnp.zeros_like(tb_ref)
        tc_ref[...] = jnp.zeros_like(tc_ref)

    c = x_ref.shape[0]
    hd, gw, ns = SSM_HEAD_DIM, SSM_GROUP_COLS, SSM_STATE
    off_b, off_c = SSM_INNER, SSM_INNER + SSM_BC_DIM
    groups = range(SSM_GROUPS)

    dt = _softplus(dt_ref[...] + dtb_ref[...])
    gcum = _cumsum_rows(dt * (-jnp.exp(alog_ref[...])))
    gcum_t = _transpose(gcum)
    per_col = _spread(jnp.concatenate([dt, gcum], axis=0), sel_hd_ref[...])
    dt_x, gc_x = per_col[:c], per_col[c:]
    gc_colb = _spread(gcum, sel_c_ref[...])
    ge_x = gc_x[c - 1:c, :]

    xr, br, cr = x_ref[...], b_ref[...], c_ref[...]
    x = _silu(_causal_conv_chunk(xr, tx_ref[...], cw_ref[:, :off_b], cb_ref[:, :off_b]))
    bm = _silu(_causal_conv_chunk(br, tb_ref[...], cw_ref[:, off_b:off_c], cb_ref[:, off_b:off_c]))
    cm = _silu(_causal_conv_chunk(cr, tc_ref[...], cw_ref[:, off_c:], cb_ref[:, off_c:]))
    tx_ref[...] = xr[c - SUBLANES:c]
    tb_ref[...] = br[c - SUBLANES:c]
    tc_ref[...] = cr[c - SUBLANES:c]

    xdt = x * dt_x
    xdec = xdt * jnp.exp(ge_x - gc_x)
    bg = [bm[:, g * ns:(g + 1) * ns] for g in groups]
    cg = [cm[:, g * ns:(g + 1) * ns] for g in groups]
    s_old = [s_ref[g] for g in groups]
    scores = [_dot(cc, bb, _NT) for cc, bb in zip(cg, bg)]
    y_inter = jnp.concatenate([_dot(cc, s) for cc, s in zip(cg, s_old)], axis=1)
    att = [scores[h // SSM_HPG] * _decay_matrix(gc_colb[:, h * c:(h + 1) * c], gcum_t[h:h + 1, :])
           for h in range(SSM_HEADS)]
    y_intra = jnp.concatenate([_dot(a, xdt[:, h * hd:(h + 1) * hd]) for h, a in enumerate(att)], axis=1)
    for g in groups:
        cols = slice(g * gw, (g + 1) * gw)
        s_ref[g] = s_old[g] * jnp.exp(ge_x[:, cols]) + _dot(bg[g], xdec[:, cols], _TN)
    y = (y_intra + y_inter * jnp.exp(gc_x) + dskip_ref[...] * x) * _silu(z_ref[...])
    for g in groups:
        cols = slice(g * gw, (g + 1) * gw)
        o_ref[:, cols] = _rms_rows(y[:, cols], nw_ref[:, cols]).astype(o_ref.dtype)

    @pl.when(c_idx == pl.num_programs(1) - 1)
    def _():
        for g in range(SSM_GROUPS):
            s_out_ref[g * SSM_HPG:(g + 1) * SSM_HPG] = _transpose(s_ref[g]).reshape(SSM_HPG, hd, ns)


def ssd_prompt(proj, conv_w, conv_b, dtb_row, alog_row, dskip_row, norm_w, chunk=128):
    b, l, _ = proj.shape
    assert chunk == LANES
    nc = l // chunk
    col = lambda width, off: pl.BlockSpec((None, chunk, width), lambda i, j: (i, j, off // width))
    row = _resident((1, LANES))
    lane = jnp.arange(LANES, dtype=jnp.int32)[:, None]
    sel_hd = (lane == jnp.arange(SSM_INNER, dtype=jnp.int32)[None, :] // SSM_HEAD_DIM).astype(BF16)
    sel_c = (lane == jnp.arange(SSM_HEADS * chunk, dtype=jnp.int32)[None, :] // chunk).astype(BF16)
    dskip_row = jnp.repeat(dskip_row[0, :SSM_HEADS], SSM_HEAD_DIM).reshape(1, SSM_INNER)
    return pl.pallas_call(
        _ssd_prompt_kernel,
        grid=(b, nc),
        in_specs=[col(SSM_INNER, SSM_OFF_Z), col(SSM_INNER, SSM_OFF_X), col(SSM_BC_DIM, SSM_OFF_B),
                  col(SSM_BC_DIM, SSM_OFF_C), col(LANES, SSM_OFF_DT),
                  _resident((CONV_WIDTH, SSM_CONV_DIM)), _resident((1, SSM_CONV_DIM)),
                  row, row, _resident((1, SSM_INNER)), _resident((1, SSM_INNER)),
                  _resident(sel_hd.shape), _resident(sel_c.shape)],
        out_specs=[pl.BlockSpec((None, chunk, SSM_INNER), lambda i, j: (i, j, 0)),
                   pl.BlockSpec((None, SSM_HEADS, SSM_HEAD_DIM, SSM_STATE), lambda i, j: (i, 0, 0, 0))],
        out_shape=[jax.ShapeDtypeStruct((b, l, SSM_INNER), BF16),
                   jax.ShapeDtypeStruct((b, SSM_HEADS, SSM_HEAD_DIM, SSM_STATE), F32)],
        scratch_shapes=[pltpu.VMEM((SSM_GROUPS, SSM_STATE, SSM_GROUP_COLS), F32),
                        pltpu.VMEM((SUBLANES, SSM_INNER), F32),
                        pltpu.VMEM((SUBLANES, SSM_BC_DIM), F32), pltpu.VMEM((SUBLANES, SSM_BC_DIM), F32)],
        compiler_params=_cparams(("parallel", "arbitrary")),
        name="ssd_prompt",
    )(proj, proj, proj, proj, proj, conv_w, conv_b.reshape(1, SSM_CONV_DIM), dtb_row, alog_row, dskip_row,
      norm_w.reshape(1, SSM_INNER), sel_hd, sel_c)


def _stacked_out(acc, n_inputs, out_index):
    return [pl.BlockSpec(memory_space=pl.ANY)], [acc], {n_inputs: out_index}


def _conv_step(x, buf_ref, w, bias=None):
    acc = x * w[CONV_WIDTH - 1:CONV_WIDTH, :]
    for i in range(CONV_WIDTH - 1):
        acc = acc + buf_ref[i] * w[i:i + 1, :]
    if bias is not None:
        acc = acc + bias
    return acc


def _conv_state_kernel(buf_ref, x_ref, *rest):
    out_ref = rest[-1]
    for i in range(CONV_WIDTH - 2):
        out_ref[i] = buf_ref[i + 1]
    out_ref[CONV_WIDTH - 2] = x_ref[...]


def conv_state_update(conv_t, proj, col_off, layer, acc, wc):
    _, nbuf, b, n = conv_t.shape
    spec = pl.BlockSpec((None, nbuf, b, wc), lambda j: (layer, 0, 0, j))
    acc_specs, acc_args, aliases = _stacked_out(acc, 2, 0)
    return pl.pallas_call(
        _conv_state_kernel,
        grid=(n // wc,),
        in_specs=[spec, pl.BlockSpec((b, wc), lambda j: (0, col_off // wc + j))] + acc_specs,
        out_specs=spec,
        out_shape=jax.ShapeDtypeStruct(conv_t.shape, F32),
        input_output_aliases=aliases,
        compiler_params=_cparams(("parallel",)),
        name="conv_state",
    )(conv_t, proj, *acc_args)


def _gdn_decode_kernel(q_ref, k_ref, v_ref, z_ref, ba_ref, bq_ref, bk_ref, bv_ref, s_ref,
                       cwq_ref, cwk_ref, cwv_ref, alog_ref, dtb_ref, nw_ref, *rest):
    o_ref, s_out_ref = rest[-2:]
    h = pl.program_id(1)
    xq, xk, xv = q_ref[...], k_ref[...], v_ref[...]
    q = _silu(_conv_step(xq, bq_ref, cwq_ref[...]))
    k = _silu(_conv_step(xk, bk_ref, cwk_ref[...]))
    v = _silu(_conv_step(xv, bv_ref, cwv_ref[...]))
    hd = GDN_DIM
    q = q * lax.rsqrt(jnp.sum(q * q, axis=-1, keepdims=True) + NORM_EPS) * (hd ** -0.5)
    k = k * lax.rsqrt(jnp.sum(k * k, axis=-1, keepdims=True) + NORM_EPS)
    ba = ba_ref[...]
    lane = _iota((1, LANES), 1)
    beta = jnp.sum(jnp.where(lane == h, _sigmoid(ba), 0.0), axis=1, keepdims=True)
    g = -jnp.exp(alog_ref[...]) * _softplus(ba + dtb_ref[...])
    eg = jnp.exp(jnp.sum(jnp.where(lane == GDN_HEADS + h, g, 0.0), axis=1, keepdims=True))
    qk = jnp.sum(q * k, axis=-1, keepdims=True)
    k_t = _transpose(k)
    q_t = _transpose(q)
    tb = xq.shape[0]
    ys = []
    for b in range(tb):
        s = s_ref[b, 0]
        k_col = k_t[:, b:b + 1]
        q_col = q_t[:, b:b + 1]
        ks = jnp.sum(k_col * s, axis=0, keepdims=True)
        egb = eg[b:b + 1, :]
        delta = beta[b:b + 1, :] * (v[b:b + 1, :] - egb * ks)
        s_new = egb * s + k_col * delta
        s_out_ref[b, 0] = s_new
        ys.append(jnp.sum(q_col * s_new, axis=0, keepdims=True))
    y = jnp.concatenate(ys, axis=0)
    y = _rms_rows(y, nw_ref[...]) * _silu(z_ref[...])
    o_ref[...] = y.astype(o_ref.dtype)


def gdn_decode(proj, state, conv_t, layer, acc, conv_w, alog_row, dtb_row, norm_w, tb=8):
    b = proj.shape[0]
    hd = GDN_DIM
    nh = GDN_HEADS
    pcol = lambda off: pl.BlockSpec((tb, hd), lambda i, h: (i, off // hd + h))
    ccol = lambda part: pl.BlockSpec((None, CONV_WIDTH - 1, tb, hd), lambda i, h: (layer, 0, i, part * nh + h))
    wcol = lambda part: pl.BlockSpec((CONV_WIDTH, hd), lambda i, h: (0, part * nh + h))
    row = _resident((1, LANES))
    sspec = pl.BlockSpec((None, tb, 1, hd, hd), lambda i, h: (layer, i, h, 0, 0))
    acc_specs, acc_args, aliases = _stacked_out(acc, 15, 1)
    return pl.pallas_call(
        _gdn_decode_kernel,
        grid=(b // tb, nh),
        in_specs=[pcol(HYB_OFF_QKV), pcol(HYB_OFF_QKV + GDN_OUT_DIM), pcol(HYB_OFF_QKV + 2 * GDN_OUT_DIM),
                  pcol(HYB_OFF_Z), pl.BlockSpec((tb, LANES), lambda i, h: (i, HYB_OFF_BA // LANES)),
                  ccol(0), ccol(1), ccol(2), sspec,
                  wcol(0), wcol(1), wcol(2), row, row, _resident((1, hd))] + acc_specs,
        out_specs=[pl.BlockSpec((tb, hd), lambda i, h: (i, h)), sspec],
        out_shape=[jax.ShapeDtypeStruct((b, GDN_OUT_DIM), BF16), jax.ShapeDtypeStruct(state.shape, F32)],
        input_output_aliases=aliases,
        compiler_params=_cparams(("parallel", "parallel")),
        name="gdn_decode",
    )(proj, proj, proj, proj, proj, conv_t, conv_t, conv_t, state,
      conv_w, conv_w, conv_w, alog_row, dtb_row, norm_w.reshape(1, hd), *acc_args)


def _ret_decode_kernel(q_ref, k_ref, v_ref, gate_ref, s_ref, cos_ref, sin_ref, nw_ref, *rest):
    o_ref, s_out_ref = rest[-2:]
    q = _rotary(q_ref[...], cos_ref[...], sin_ref[...])
    k = _rotary(k_ref[...], cos_ref[...], sin_ref[...]) * (RET_KEY_DIM ** -0.5)
    q_t = _transpose(q)
    k_t = _transpose(k)
    tb = q.shape[0]
    kd, vd = RET_KEY_DIM, RET_VAL_DIM
    rows = []
    for b in range(tb):
        ys = []
        for h in range(RET_HEADS):
            gamma = math.exp(_ret_log_gamma(h))
            s = s_ref[b, h]
            s_new = s * gamma + k_t[h * kd:(h + 1) * kd, b:b + 1] * v_ref[b:b + 1, h * vd:(h + 1) * vd]
            s_out_ref[b, h] = s_new
            ys.append(jnp.sum(q_t[h * kd:(h + 1) * kd, b:b + 1] * s_new, axis=0, keepdims=True))
        rows.append(jnp.concatenate(ys, axis=1))
    y = jnp.concatenate(rows, axis=0)
    for h in range(RET_HEADS):
        yh = (_group_norm_rows(y[:, h * vd:(h + 1) * vd]) * nw_ref[:, h * vd:(h + 1) * vd]
              * _silu(gate_ref[:, h * vd:(h + 1) * vd]))
        o_ref[:, h * vd:(h + 1) * vd] = yh.astype(o_ref.dtype)


def ret_decode(proj, state, layer, acc, cos_f, sin_s, norm_w, tb=8):
    b = proj.shape[0]
    pcol = lambda width, off: pl.BlockSpec((tb, width), lambda i: (i, off // width))
    sspec = pl.BlockSpec((None, tb, RET_HEADS, RET_KEY_DIM, RET_VAL_DIM), lambda i: (layer, i, 0, 0, 0))
    acc_specs, acc_args, aliases = _stacked_out(acc, 8, 1)
    return pl.pallas_call(
        _ret_decode_kernel,
        grid=(b // tb,),
        in_specs=[pcol(RET_QK_DIM, HYB_OFF_RQ), pcol(RET_QK_DIM, HYB_OFF_RK), pcol(RET_OUT_DIM, HYB_OFF_RV),
                  pcol(RET_OUT_DIM, HYB_OFF_RG), sspec, _resident((1, RET_QK_DIM)), _resident((1, RET_QK_DIM)),
                  _resident((1, RET_OUT_DIM))] + acc_specs,
        out_specs=[pl.BlockSpec((tb, RET_OUT_DIM), lambda i: (i, 0)), sspec],
        out_shape=[jax.ShapeDtypeStruct((b, RET_OUT_DIM), BF16), jax.ShapeDtypeStruct(state.shape, F32)],
        input_output_aliases=aliases,
        compiler_params=_cparams(("parallel",)),
        name="ret_decode",
    )(proj, proj, proj, proj, state, cos_f, sin_s, norm_w.reshape(1, RET_OUT_DIM), *acc_args)


def _ssd_decode_kernel(z_ref, x_ref, b_ref, c_ref, dt_ref, bx_ref, bb_ref, bc_ref, s_ref,
                       cwx_ref, cwb_ref, cwc_ref, cbx_ref, cbb_ref, cbc_ref,
                       dtb_ref, alog_ref, dskip_ref, nw_ref, *rest):
    o_ref, s_out_ref = rest[-2:]
    grp = pl.program_id(1)
    xr, br, cr = x_ref[...], b_ref[...], c_ref[...]
    x = _silu(_conv_step(xr, bx_ref, cwx_ref[...], cbx_ref[...]))
    bm = _silu(_conv_step(br, bb_ref, cwb_ref[...], cbb_ref[...]))
    cm = _silu(_conv_step(cr, bc_ref, cwc_ref[...], cbc_ref[...]))
    tb = xr.shape[0]
    hd, gw = SSM_HEAD_DIM, SSM_GROUP_COLS
    dt = _softplus(dt_ref[...] + dtb_ref[...])
    spread = (_iota((LANES, gw), 0) == grp * SSM_HPG + _iota((LANES, gw), 1) // hd).astype(BF16)
    per_head = jnp.concatenate([dt, -jnp.exp(alog_ref[...]), dskip_ref[...],
                                jnp.zeros((2 * SUBLANES - tb - 2, LANES), F32)], axis=0)
    ph_hi, ph_lo = _split(per_head)
    per_col = (jnp.dot(ph_hi, spread, preferred_element_type=F32)
               + jnp.dot(ph_lo, spread, preferred_element_type=F32))
    dt_x, a_x, dskip_x = per_col[0:tb], per_col[tb:tb + 1], per_col[tb + 1:tb + 2]
    eg_x = jnp.exp(dt_x * a_x)
    xdt_t = _split((x * dt_x).T)
    row_id = _iota(bm.shape, 0)
    outer = [_dot3(xdt_t, _split(jnp.where(row_id == b, bm, 0.0))) for b in range(tb)]
    for b in range(tb):
        for hl in range(SSM_HPG):
            s_out_ref[b, hl] = (s_ref[b, hl] * eg_x[b:b + 1, hl * hd:hl * hd + 1]
                                + outer[b][hl * hd:(hl + 1) * hd, :])
    rows = [_dot(cm[b:b + 1, :], s_out_ref[b].reshape(gw, SSM_STATE), _NT) for b in range(tb)]
    y = (jnp.concatenate(rows, axis=0) + dskip_x * x) * _silu(z_ref[...])
    o_ref[...] = _rms_rows(y, nw_ref[...]).astype(o_ref.dtype)


def ssd_decode(proj, state_t, conv_t, layer, acc, conv_w, conv_b, dtb_row, alog_row, dskip_row, norm_w, tb=8):
    b = proj.shape[0]
    gw = SSM_GROUP_COLS
    ns = SSM_STATE
    cb = conv_b.reshape(1, SSM_CONV_DIM)
    pcol = lambda width, off: pl.BlockSpec((tb, width), lambda i, g: (i, off // width + g))
    ccol = lambda width, off: pl.BlockSpec((None, CONV_WIDTH - 1, tb, width),
                                           lambda i, g: (layer, 0, i, off // width + g))
    wcol = lambda rows, width, off: pl.BlockSpec((rows, width), lambda i, g: (0, off // width + g))
    row = _resident((1, LANES))
    sspec = pl.BlockSpec((None, tb, SSM_HPG, SSM_HEAD_DIM, ns), lambda i, g: (layer, i, g, 0, 0))
    off_b, off_c = SSM_INNER, SSM_INNER + SSM_BC_DIM
    acc_specs, acc_args, aliases = _stacked_out(acc, 19, 1)
    return pl.pallas_call(
        _ssd_decode_kernel,
        grid=(b // tb, SSM_GROUPS),
        in_specs=[pcol(gw, SSM_OFF_Z), pcol(gw, SSM_OFF_X), pcol(ns, SSM_OFF_B), pcol(ns, SSM_OFF_C),
                  pl.BlockSpec((tb, LANES), lambda i, g: (i, SSM_OFF_DT // LANES)),
                  ccol(gw, 0), ccol(ns, off_b), ccol(ns, off_c), sspec,
                  wcol(CONV_WIDTH, gw, 0), wcol(CONV_WIDTH, ns, off_b), wcol(CONV_WIDTH, ns, off_c),
                  wcol(1, gw, 0), wcol(1, ns, off_b), wcol(1, ns, off_c),
                  row, row, row, wcol(1, gw, 0)] + acc_specs,
        out_specs=[pl.BlockSpec((tb, gw), lambda i, g: (i, g)), sspec],
        out_shape=[jax.ShapeDtypeStruct((b, SSM_INNER), BF16), jax.ShapeDtypeStruct(state_t.shape, F32)],
        input_output_aliases=aliases,
        compiler_params=_cparams(("parallel", "parallel")),
        name="ssd_decode",
    )(proj, proj, proj, proj, proj, conv_t, conv_t, conv_t, state_t,
      conv_w, conv_w, conv_w, cb, cb, cb, dtb_row, alog_row, dskip_row, norm_w.reshape(1, SSM_INNER), *acc_args)


def _lane_row(vals, offset):
    return jnp.zeros((1, LANES), F32).at[0, offset:offset + vals.shape[0]].set(vals.astype(F32))


def _prep_hyb_w_in(w):
    sizes = (GDN_QKV_DIM, GDN_OUT_DIM, GDN_HEADS, GDN_HEADS, RET_QK_DIM, RET_QK_DIM, RET_OUT_DIM, RET_OUT_DIM)
    offs = [0]
    for s in sizes:
        offs.append(offs[-1] + s)
    part = lambda i: w[:, offs[i]:offs[i + 1]]
    cols = [part(0), part(1), part(4), part(5), part(6), part(7), part(2), part(3)]
    out = jnp.concatenate(cols, axis=1)
    return jnp.pad(out, ((0, 0), (0, HYB_N - out.shape[1]))).astype(BF16)


def _prep_ssm_w_in(w):
    return jnp.pad(w, ((0, 0), (0, SSM_N - w.shape[1]))).astype(BF16)


def _rope_tables(pos):
    half = RET_KEY_DIM // 2
    inv_freq = ROPE_BASE ** (-jnp.arange(half, dtype=F32) / half)
    ang = pos.astype(F32)[:, None] * inv_freq[None, :]
    cos, sin = jnp.cos(ang), jnp.sin(ang)
    cos_f = jnp.tile(jnp.concatenate([cos, cos], axis=1), (1, RET_HEADS))
    sin_s = jnp.tile(jnp.concatenate([-sin, sin], axis=1), (1, RET_HEADS))
    return cos_f, sin_s


def _trunk(x, pos, states, params, prompt):
    bsz, l, d = x.shape
    m = bsz * l
    tm = 512 if m % 512 == 0 else m
    x2 = x.reshape(m, d)
    cos_f, sin_s = _rope_tables(pos)
    new_gdn, new_gdn_conv, new_ret, new_ssm, new_ssm_conv = [], [], [], [], []
    if not prompt:
        s_gdn, c_gdn, s_ret, s_ssm, c_ssm = states
        c_gdn_t, c_ssm_t = jnp.swapaxes(c_gdn, 1, 2), jnp.swapaxes(c_ssm, 1, 2)
        s_ssm_t = jnp.swapaxes(s_ssm, 3, 4)
        acc_gdn, acc_gdn_conv, acc_ret, acc_ssm, acc_ssm_conv = (
            jnp.zeros_like(a) for a in (s_gdn, c_gdn_t, s_ret, s_ssm_t, c_ssm_t))
    for layer in range(DEPTH):
        i = layer // 2
        p = params
        if layer % 2 == 0:
            proj = norm_matmul(x2, p["norm_mix"][layer], p["w_in_hyb"][i], tm)
            alog_row = _lane_row(p["gdn_a_log"][i], GDN_HEADS)
            dtb_row = _lane_row(p["gdn_dt_bias"][i], GDN_HEADS)
            if prompt:
                proj3 = proj.reshape(bsz, l, HYB_N)
                o_a, sg = gdn_prompt(proj3, p["gdn_conv_w"][i], alog_row, dtb_row, p["gdn_norm_w"][i])
                o_b, sr = ret_prompt(proj3, cos_f, sin_s, p["ret_norm_w"][i])
                cg = proj3[:, l - (CONV_WIDTH - 1):, :GDN_QKV_DIM]
                o_a, o_b = o_a.reshape(m, GDN_OUT_DIM), o_b.reshape(m, RET_OUT_DIM)
                new_gdn.append(sg)
                new_gdn_conv.append(cg)
                new_ret.append(sr)
            else:
                o_a, acc_gdn = gdn_decode(proj, s_gdn, c_gdn_t, i, acc_gdn, p["gdn_conv_w"][i], alog_row,
                                          dtb_row, p["gdn_norm_w"][i])
                acc_gdn_conv = conv_state_update(c_gdn_t, proj, HYB_OFF_QKV, i, acc_gdn_conv, GDN_OUT_DIM)
                o_b, acc_ret = ret_decode(proj, s_ret, i, acc_ret, cos_f, sin_s, p["ret_norm_w"][i])
            w_out = p["w_out_hyb"][i]
            x2 = out_proj_residual([o_a, o_b], [w_out[:GDN_OUT_DIM], w_out[GDN_OUT_DIM:]], x2, tm)
        else:
            proj = norm_matmul(x2, p["norm_mix"][layer], p["w_in_ssm"][i], tm)
            dtb_row = _lane_row(p["ssm_dt_bias"][i], 0)
            alog_row = _lane_row(p["ssm_a_log"][i], 0)
            dskip_row = _lane_row(p["ssm_d"][i], 0)
            if prompt:
                proj3 = proj.reshape(bsz, l, SSM_N)
                y, ss = ssd_prompt(proj3, p["ssm_conv_w"][i], p["ssm_conv_b"][i], dtb_row, alog_row,
                                   dskip_row, p["ssm_norm_w"][i])
                cs = proj3[:, l - (CONV_WIDTH - 1):, SSM_OFF_X:SSM_OFF_X + SSM_CONV_DIM]
                y = y.reshape(m, SSM_INNER)
                new_ssm.append(ss)
                new_ssm_conv.append(cs)
            else:
                y, acc_ssm = ssd_decode(proj, s_ssm_t, c_ssm_t, i, acc_ssm, p["ssm_conv_w"][i],
                                        p["ssm_conv_b"][i], dtb_row, alog_row, dskip_row, p["ssm_norm_w"][i])
                acc_ssm_conv = conv_state_update(c_ssm_t, proj, SSM_OFF_X, i, acc_ssm_conv, D_MODEL)
            x2 = out_proj_residual([y], [p["w_out_ssm"][i]], x2, tm)
        x2 = mlp_residual(x2, p["norm_mlp"][layer], p["mlp_w1"][layer], p["mlp_w2"][layer],
                          p["norm_final"], layer == DEPTH - 1, tm)
    y_out = x2.reshape(bsz, l, d)
    if prompt:
        return (y_out, jnp.stack(new_gdn), jnp.stack(new_gdn_conv), jnp.stack(new_ret),
                jnp.swapaxes(jnp.stack(new_ssm), 3, 4), jnp.stack(new_ssm_conv))
    return (y_out, acc_gdn, jnp.swapaxes(acc_gdn_conv, 1, 2), acc_ret,
            jnp.swapaxes(acc_ssm, 3, 4), jnp.swapaxes(acc_ssm_conv, 1, 2))


def kernel(x_prompt, x_sample, state_gdn, state_gdn_conv, state_ret, state_ssm, state_ssm_conv, norm_mix, norm_mlp, norm_final, w_in_hyb, gdn_conv_w, gdn_a_log, gdn_dt_bias, gdn_norm_w, ret_norm_w, w_out_hyb, w_in_ssm, ssm_conv_w, ssm_conv_b, ssm_dt_bias, ssm_a_log, ssm_d, ssm_norm_w, w_out_ssm, mlp_w1, mlp_w2):
    n_hyb, n_ssm = w_in_hyb.shape[0], w_in_ssm.shape[0]
    params = dict(
        norm_mix=norm_mix, norm_mlp=norm_mlp, norm_final=norm_final,
        w_in_hyb=[_prep_hyb_w_in(w_in_hyb[i]) for i in range(n_hyb)],
        gdn_conv_w=gdn_conv_w, gdn_a_log=gdn_a_log, gdn_dt_bias=gdn_dt_bias, gdn_norm_w=gdn_norm_w,
        ret_norm_w=ret_norm_w, w_out_hyb=w_out_hyb.astype(BF16),
        w_in_ssm=[_prep_ssm_w_in(w_in_ssm[i]) for i in range(n_ssm)],
        ssm_conv_w=ssm_conv_w, ssm_conv_b=ssm_conv_b, ssm_dt_bias=ssm_dt_bias, ssm_a_log=ssm_a_log,
        ssm_d=ssm_d, ssm_norm_w=ssm_norm_w, w_out_ssm=w_out_ssm.astype(BF16),
        mlp_w1=mlp_w1.astype(BF16), mlp_w2=mlp_w2.astype(BF16))
    lp, ls = x_prompt.shape[1], x_sample.shape[1]
    pos_prompt = jnp.arange(lp, dtype=jnp.int32)
    pos_sample = PAST_LEN + jnp.arange(ls, dtype=jnp.int32)
    y_p, p_gdn, p_gdn_conv, p_ret, p_ssm, p_ssm_conv = _trunk(
        x_prompt, pos_prompt, (None,) * 5, params, prompt=True)
    y_s, s_gdn, s_gdn_conv, s_ret, s_ssm, s_ssm_conv = _trunk(
        x_sample, pos_sample, (state_gdn, state_gdn_conv, state_ret, state_ssm, state_ssm_conv),
        params, prompt=False)
    return (y_p, y_s, p_gdn, p_gdn_conv, p_ret, p_ssm, p_ssm_conv,
            s_gdn, s_gdn_conv, s_ret, s_ssm, s_ssm_conv)
```

```python
import functools
import math

import jax
import jax.numpy as jnp
from jax import lax
from jax.experimental import pallas as pl
from jax.experimental.pallas import tpu as pltpu

F32 = jnp.float32
BF16 = jnp.bfloat16

D_MODEL = 1024
DEPTH = 4
CONV_WIDTH = 4
NORM_EPS = 1e-6
PAST_LEN = 16384

GDN_HEADS = 4
GDN_DIM = 128
GDN_QKV_DIM = 3 * GDN_HEADS * GDN_DIM
GDN_OUT_DIM = GDN_HEADS * GDN_DIM

RET_HEADS = 4
RET_KEY_DIM = 64
RET_VAL_DIM = 128
RET_QK_DIM = RET_HEADS * RET_KEY_DIM
RET_OUT_DIM = RET_HEADS * RET_VAL_DIM
ROPE_BASE = 10000.0

SSM_INNER = 2 * D_MODEL
SSM_HEAD_DIM = 64
SSM_HEADS = SSM_INNER // SSM_HEAD_DIM
SSM_GROUPS = 4
SSM_HPG = SSM_HEADS // SSM_GROUPS
SSM_STATE = 128
SSM_GROUP_COLS = SSM_HPG * SSM_HEAD_DIM
SSM_BC_DIM = SSM_GROUPS * SSM_STATE
SSM_CONV_DIM = SSM_INNER + 2 * SSM_BC_DIM
MLP_HIDDEN = 4 * D_MODEL

LANES = 128
SUBLANES = 8

HYB_OFF_QKV = 0
HYB_OFF_Z = GDN_QKV_DIM
HYB_OFF_RQ = HYB_OFF_Z + GDN_OUT_DIM
HYB_OFF_RK = HYB_OFF_RQ + RET_QK_DIM
HYB_OFF_RV = HYB_OFF_RK + RET_QK_DIM
HYB_OFF_RG = HYB_OFF_RV + RET_OUT_DIM
HYB_OFF_BA = HYB_OFF_RG + RET_OUT_DIM
HYB_N = 3840
SSM_OFF_Z = 0
SSM_OFF_X = SSM_INNER
SSM_OFF_B = SSM_OFF_X + SSM_INNER
SSM_OFF_C = SSM_OFF_B + SSM_BC_DIM
SSM_OFF_DT = SSM_OFF_C + SSM_BC_DIM
SSM_N = 5376
PROJ_TN = 768

VMEM_LIMIT = 56 * 1024 * 1024

_NN = (((1,), (0,)), ((), ()))
_NT = (((1,), (1,)), ((), ()))
_TN = (((0,), (0,)), ((), ()))


def _dot(a, b, dims=_NN):
    return lax.dot_general(a.astype(BF16), b.astype(BF16), dims, preferred_element_type=F32)


def _dot_f32(a, b, dims=_NN):
    return lax.dot_general(a, b, dims, precision=lax.Precision.HIGHEST, preferred_element_type=F32)


def _sigmoid(x):
    return 1.0 / (1.0 + jnp.exp(-x))


def _silu(x):
    return x * _sigmoid(x)


def _softplus(x):
    return jnp.maximum(x, 0.0) + jnp.log(1.0 + jnp.exp(-jnp.abs(x)))


def _iota(shape, dim):
    return lax.broadcasted_iota(jnp.int32, shape, dim)


def _eye(n):
    return (_iota((n, n), 0) == _iota((n, n), 1)).astype(F32)


def _transpose(x):
    return x.T


def _cparams(sem):
    return pltpu.CompilerParams(dimension_semantics=sem, vmem_limit_bytes=VMEM_LIMIT)


def _resident(shape):
    nd = len(shape)
    return pl.BlockSpec(shape, lambda *_: (0,) * nd, pipeline_mode=pl.Buffered(1))


def _rep8(v):
    v = v.reshape(1, -1)
    return jnp.broadcast_to(v, (SUBLANES, v.shape[1]))


def _rep8_rows(w):
    return jnp.broadcast_to(w[:, None, :], (w.shape[0], SUBLANES, w.shape[1]))


def _mul8(x, r8):
    r, n = x.shape
    return (x.reshape(r // SUBLANES, SUBLANES, n) * r8).reshape(r, n)


def _add8(x, r8):
    r, n = x.shape
    return (x.reshape(r // SUBLANES, SUBLANES, n) + r8).reshape(r, n)


def _rms_rows(x, gain8):
    return _mul8(x * lax.rsqrt(jnp.mean(x * x, axis=-1, keepdims=True) + NORM_EPS), gain8)


def _norm_matmul_kernel(x_ref, g_ref, w_ref, o_ref, *, tn):
    xn = _rms_rows(x_ref[...], g_ref[...]).astype(BF16)
    for j in range(w_ref.shape[1] // tn):
        o_ref[:, j * tn:(j + 1) * tn] = jnp.dot(xn, w_ref[:, j * tn:(j + 1) * tn],
                                                preferred_element_type=F32)


def norm_matmul(x, gain, w, tm):
    m, d = x.shape
    n = w.shape[1]
    return pl.pallas_call(
        functools.partial(_norm_matmul_kernel, tn=PROJ_TN),
        grid=(m // tm,),
        in_specs=[pl.BlockSpec((tm, d), lambda i: (i, 0)), _resident((SUBLANES, d)), _resident((d, n))],
        out_specs=pl.BlockSpec((tm, n), lambda i: (i, 0)),
        out_shape=jax.ShapeDtypeStruct((m, n), F32),
        compiler_params=_cparams(("parallel",)),
        name="norm_matmul",
    )(x, _rep8(gain), w)


def _out_proj_kernel(*refs, n_in):
    a_refs, w_refs, r_ref, o_ref = refs[:n_in], refs[n_in:2 * n_in], refs[2 * n_in], refs[2 * n_in + 1]
    acc = r_ref[...]
    for a_ref, w_ref in zip(a_refs, w_refs):
        acc = acc + jnp.dot(a_ref[...], w_ref[...], preferred_element_type=F32)
    o_ref[...] = acc


def out_proj_residual(acts, ws, res, tm):
    m, d = res.shape
    n_in = len(acts)
    in_specs = ([pl.BlockSpec((tm, a.shape[1]), lambda i: (i, 0)) for a in acts]
                + [_resident(w.shape) for w in ws]
                + [pl.BlockSpec((tm, d), lambda i: (i, 0))])
    return pl.pallas_call(
        functools.partial(_out_proj_kernel, n_in=n_in),
        grid=(m // tm,),
        in_specs=in_specs,
        out_specs=pl.BlockSpec((tm, d), lambda i: (i, 0)),
        out_shape=jax.ShapeDtypeStruct((m, d), F32),
        compiler_params=_cparams(("parallel",)),
        name="out_proj",
    )(*acts, *ws, res)


def _mlp_kernel(x_ref, g_ref, w1_ref, w2_ref, gf_ref, o_ref, *, tf, final_norm):
    x = x_ref[...]
    xn = _rms_rows(x, g_ref[...]).astype(BF16)
    acc = x
    for j in range(w1_ref.shape[1] // tf):
        h = jnp.dot(xn, w1_ref[:, j * tf:(j + 1) * tf], preferred_element_type=F32)
        h = jnp.maximum(h, 0.0)
        acc = acc + jnp.dot((h * h).astype(BF16), w2_ref[j * tf:(j + 1) * tf, :],
                            preferred_element_type=F32)
    if final_norm:
        acc = _rms_rows(acc, gf_ref[...])
    o_ref[...] = acc


def mlp_residual(x, gain, w1, w2, final_gain, final_norm, tm):
    m, d = x.shape
    f = w1.shape[1]
    return pl.pallas_call(
        functools.partial(_mlp_kernel, tf=1024, final_norm=final_norm),
        grid=(m // tm,),
        in_specs=[pl.BlockSpec((tm, d), lambda i: (i, 0)), _resident((SUBLANES, d)), _resident((d, f)),
                  _resident((f, d)), _resident((SUBLANES, d))],
        out_specs=pl.BlockSpec((tm, d), lambda i: (i, 0)),
        out_shape=jax.ShapeDtypeStruct((m, d), F32),
        compiler_params=_cparams(("parallel",)),
        name="mlp",
    )(x, _rep8(gain), w1, w2, _rep8(final_gain))


def _causal_conv_chunk(x, tail, w8, bias8=None):
    c, n = x.shape
    g = c // SUBLANES
    x3 = x.reshape(g, SUBLANES, n)
    acc = x3 * w8[CONV_WIDTH - 1]
    row = _iota((SUBLANES, n), 0)
    for k in range(1, CONV_WIDTH):
        r = pltpu.roll(x3, k, 1)
        prev = jnp.concatenate([pltpu.roll(tail, k, 0)[None], r[:g - 1]], axis=0)
        acc = acc + jnp.where(row < k, prev, r) * w8[CONV_WIDTH - 1 - k]
    if bias8 is not None:
        acc = acc + bias8
    return acc.reshape(c, n)


def _cumsum_rows(g):
    c = g.shape[0]
    tri = (_iota((c, c), 0) >= _iota((c, c), 1)).astype(F32)
    return _dot_f32(tri, g)


def _decay_matrix(gc_col, gc_row):
    c = gc_col.shape[0]
    causal = _iota((c, c), 0) >= _iota((c, c), 1)
    return jnp.where(causal, jnp.exp(gc_col - gc_row), 0.0)


def _split(x):
    hi = x.astype(BF16)
    return hi, (x - hi.astype(F32)).astype(BF16)


def _dot3(a, b, dims=_NN):
    (ah, al), (bh, bl) = a, b
    d = lambda p, q: lax.dot_general(p, q, dims, preferred_element_type=F32)
    return d(ah, bh) + (d(ah, bl) + d(al, bh))


def _unit_lower_inverses(n_list):
    c = n_list[0].shape[0]
    ri, ci = _iota((c, c), 0), _iota((c, c), 1)
    eye = (ri == ci).astype(F32)
    blk = 16
    same = (ri // blk) == (ci // blk)
    p = [jnp.where(same, -n, 0.0).astype(BF16) for n in n_list]
    inv = [eye + x for x in p]
    for _ in range(3):
        p = [_dot(x, x).astype(BF16) for x in p]
        inv = [i + _dot(i, x) for i, x in zip(inv, p)]
    while blk < c:
        pair = ((ri // (2 * blk)) == (ci // (2 * blk))) & ((ri // blk) != (ci // blk))
        inv_b = [i.astype(BF16) for i in inv]
        t = [_dot(jnp.where(pair, n, 0.0), i) for n, i in zip(n_list, inv_b)]
        inv = [i - _dot(i_b, x) for i, i_b, x in zip(inv, inv_b, t)]
        blk *= 2
    return inv


def _gdn_prompt_kernel(qkv_ref, z_ref, ba_ref, cw_ref, alog_ref, dtb_ref, nw_ref,
                       o_ref, s_out_ref, s_ref, tail_ref):
    c_idx = pl.program_id(1)

    @pl.when(c_idx == 0)
    def _():
        s_ref[...] = jnp.zeros_like(s_ref)
        tail_ref[...] = jnp.zeros_like(tail_ref)

    nb, c = qkv_ref.shape[0], qkv_ref.shape[1]
    hd = GDN_DIM
    ri, ci = _iota((c, c), 0), _iota((c, c), 1)
    strict = ri > ci
    chains = [(bb, h) for bb in range(nb) for h in range(GDN_HEADS)]

    conv, beta, gcum, gcum_t = [], [], [], []
    for bb in range(nb):
        x = qkv_ref[bb]
        conv.append(_silu(_causal_conv_chunk(x, tail_ref[bb], cw_ref[...])))
        tail_ref[bb] = x[c - SUBLANES:c]
        ba = ba_ref[bb]
        beta.append(_sigmoid(ba))
        gc = _cumsum_rows(-jnp.exp(alog_ref[...]) * _softplus(ba + dtb_ref[...]))
        gcum.append(gc)
        gcum_t.append(_transpose(gc))

    q, k, v, b_col, gc_col, g_end, decay = [], [], [], [], [], [], []
    for bb, h in chains:
        qh = conv[bb][:, h * hd:(h + 1) * hd]
        kh = conv[bb][:, GDN_OUT_DIM + h * hd:GDN_OUT_DIM + (h + 1) * hd]
        q.append(qh * lax.rsqrt(jnp.sum(qh * qh, axis=-1, keepdims=True) + NORM_EPS) * (hd ** -0.5))
        k.append(kh * lax.rsqrt(jnp.sum(kh * kh, axis=-1, keepdims=True) + NORM_EPS))
        v.append(conv[bb][:, 2 * GDN_OUT_DIM + h * hd:2 * GDN_OUT_DIM + (h + 1) * hd])
        b_col.append(beta[bb][:, h:h + 1])
        gc_col.append(gcum[bb][:, GDN_HEADS + h:GDN_HEADS + h + 1])
        g_end.append(gcum[bb][c - 1:c, GDN_HEADS + h:GDN_HEADS + h + 1])
        decay.append(_decay_matrix(gc_col[-1], gcum_t[bb][GDN_HEADS + h:GDN_HEADS + h + 1, :]))
    kk = [_dot(x, x, _NT) for x in k]
    qk = [_dot(x, y, _NT) * d for x, y, d in zip(q, k, decay)]
    t_inv = _unit_lower_inverses([jnp.where(strict, b * x * d, 0.0) for b, x, d in zip(b_col, kk, decay)])
    rhs = [jnp.concatenate([vv * b, kx * (b * jnp.exp(g))], axis=1)
           for vv, kx, b, g in zip(v, k, b_col, gc_col)]
    sol = [_dot(t, r) for t, r in zip(t_inv, rhs)]
    s_old = [s_ref[bb, h] for bb, h in chains]
    delta = [x[:, :hd] - _dot(x[:, hd:], s) for x, s in zip(sol, s_old)]
    y = [_dot(x * jnp.exp(g), s) + _dot(a, dl)
         for x, g, s, a, dl in zip(q, gc_col, s_old, qk, delta)]
    s_new = [s * jnp.exp(ge) + _dot(kx * jnp.exp(ge - g), dl, _TN)
             for s, ge, kx, g, dl in zip(s_old, g_end, k, gc_col, delta)]
    for (bb, h), sn, yy in zip(chains, s_new, y):
        s_ref[bb, h] = sn
        out = _rms_rows(yy, nw_ref[...]) * _silu(z_ref[bb, :, h * hd:(h + 1) * hd])
        o_ref[bb, :, h * hd:(h + 1) * hd] = out.astype(o_ref.dtype)

    @pl.when(c_idx == pl.num_programs(1) - 1)
    def _():
        s_out_ref[...] = s_ref[...]


def gdn_prompt(proj, conv_w, alog_row, dtb_row, norm_w, chunk=64, nb=4):
    b, l, _ = proj.shape
    nc = l // chunk
    col = lambda width, off: pl.BlockSpec((nb, chunk, width), lambda i, j: (i, j, off // width))
    return pl.pallas_call(
        _gdn_prompt_kernel,
        grid=(b // nb, nc),
        in_specs=[col(GDN_QKV_DIM, HYB_OFF_QKV), col(GDN_OUT_DIM, HYB_OFF_Z), col(LANES, HYB_OFF_BA),
                  _resident((CONV_WIDTH, SUBLANES, GDN_QKV_DIM)), _resident((1, LANES)), _resident((1, LANES)),
                  _resident((SUBLANES, GDN_DIM))],
        out_specs=[pl.BlockSpec((nb, chunk, GDN_OUT_DIM), lambda i, j: (i, j, 0)),
                   pl.BlockSpec((nb, GDN_HEADS, GDN_DIM, GDN_DIM), lambda i, j: (i, 0, 0, 0))],
        out_shape=[jax.ShapeDtypeStruct((b, l, GDN_OUT_DIM), BF16),
                   jax.ShapeDtypeStruct((b, GDN_HEADS, GDN_DIM, GDN_DIM), F32)],
        scratch_shapes=[pltpu.VMEM((nb, GDN_HEADS, GDN_DIM, GDN_DIM), F32),
                        pltpu.VMEM((nb, SUBLANES, GDN_QKV_DIM), F32)],
        compiler_params=_cparams(("parallel", "arbitrary")),
        name="gdn_prompt",
    )(proj, proj, proj, _rep8_rows(conv_w), alog_row, dtb_row, _rep8(norm_w))


def _ret_log_gamma(h):
    return math.log(1.0 - 2.0 ** (-5.0 - h))


def _rotary(t, cos_f, sin_s):
    n = t.shape[1]
    half = RET_KEY_DIM // 2
    first = (_iota(t.shape, 1) % RET_KEY_DIM) < half
    swapped = jnp.where(first, pltpu.roll(t, n - half, 1), pltpu.roll(t, half, 1))
    return t * cos_f + swapped * sin_s


def _group_norm_rows(x):
    xc = x - jnp.mean(x, axis=-1, keepdims=True)
    return xc * lax.rsqrt(jnp.mean(xc * xc, axis=-1, keepdims=True) + NORM_EPS)


def _ret_prompt_kernel(q_ref, k_ref, v_ref, gate_ref, cos_ref, sin_ref, nw_ref,
                       o_ref, s_out_ref, s_ref):
    c_idx = pl.program_id(1)

    @pl.when(c_idx == 0)
    def _():
        s_ref[...] = jnp.zeros_like(s_ref)

    q = _rotary(q_ref[...], cos_ref[...], sin_ref[...])
    k = _rotary(k_ref[...], cos_ref[...], sin_ref[...]) * (RET_KEY_DIM ** -0.5)
    c = q.shape[0]
    ri, ci = _iota((c, c), 0), _iota((c, c), 1)
    causal = ri >= ci
    pos_col = _iota((c, 1), 0).astype(F32)
    kd, vd = RET_KEY_DIM, RET_VAL_DIM
    for h in range(RET_HEADS):
        lg = _ret_log_gamma(h)
        decay = jnp.where(causal, jnp.exp((ri - ci).astype(F32) * lg), 0.0)
        qh = q[:, h * kd:(h + 1) * kd]
        kh = k[:, h * kd:(h + 1) * kd]
        vh = v_ref[:, h * vd:(h + 1) * vd]
        s = s_ref[h]
        y = _dot(_dot(qh, kh, _NT) * decay, vh) + _dot(qh, s) * jnp.exp((pos_col + 1.0) * lg)
        s_ref[h] = s * math.exp(c * lg) + _dot(kh * jnp.exp((c - 1.0 - pos_col) * lg), vh, _TN)
        y = (_mul8(_group_norm_rows(y), nw_ref[:, h * vd:(h + 1) * vd])
             * _silu(gate_ref[:, h * vd:(h + 1) * vd]))
        o_ref[:, h * vd:(h + 1) * vd] = y.astype(o_ref.dtype)

    @pl.when(c_idx == pl.num_programs(1) - 1)
    def _():
        s_out_ref[...] = s_ref[...]


def ret_prompt(proj, cos_f, sin_s, norm_w, chunk=128):
    b, l, _ = proj.shape
    nc = l // chunk
    col = lambda width, off: pl.BlockSpec((None, chunk, width), lambda i, j: (i, j, off // width))
    tab = pl.BlockSpec((chunk, RET_QK_DIM), lambda i, j: (j, 0))
    return pl.pallas_call(
        _ret_prompt_kernel,
        grid=(b, nc),
        in_specs=[col(RET_QK_DIM, HYB_OFF_RQ), col(RET_QK_DIM, HYB_OFF_RK), col(RET_OUT_DIM, HYB_OFF_RV),
                  col(RET_OUT_DIM, HYB_OFF_RG), tab, tab, _resident((SUBLANES, RET_OUT_DIM))],
        out_specs=[pl.BlockSpec((None, chunk, RET_OUT_DIM), lambda i, j: (i, j, 0)),
                   pl.BlockSpec((None, RET_HEADS, RET_KEY_DIM, RET_VAL_DIM), lambda i, j: (i, 0, 0, 0))],
        out_shape=[jax.ShapeDtypeStruct((b, l, RET_OUT_DIM), BF16),
                   jax.ShapeDtypeStruct((b, RET_HEADS, RET_KEY_DIM, RET_VAL_DIM), F32)],
        scratch_shapes=[pltpu.VMEM((RET_HEADS, RET_KEY_DIM, RET_VAL_DIM), F32)],
        compiler_params=_cparams(("parallel", "arbitrary")),
        name="ret_prompt",
    )(proj, proj, proj, proj, cos_f, sin_s, _rep8(norm_w))


def _split3(x):
    a = x.astype(BF16)
    r = x - a.astype(F32)
    b = r.astype(BF16)
    return a, b, (r - b.astype(F32)).astype(BF16)


def _spread(x, sel):
    return sum(jnp.dot(p, sel, preferred_element_type=F32) for p in _split3(x))


def _ssd_prompt_kernel(z_ref, x_ref, b_ref, c_ref, dt_ref, cw_ref, cb_ref, dtb_ref, alog_ref, dskip_ref,
                       nw_ref, sel_hd_ref, sel_c_ref, o_ref, s_out_ref, s_ref, tx_ref, tb_ref, tc_ref):
    c_idx = pl.program_id(1)

    @pl.when(c_idx == 0)
    def _():
        s_ref[...] = jnp.zeros_like(s_ref)
        tx_ref[...] = jnp.zeros_like(tx_ref)
        tb_ref[...] = jnp.zeros_like(tb_ref)
        tc_ref[...] = jnp.zeros_like(tc_ref)

    c = x_ref.shape[0]
    hd, gw, ns = SSM_HEAD_DIM, SSM_GROUP_COLS, SSM_STATE
    off_b, off_c = SSM_INNER, SSM_INNER + SSM_BC_DIM
    groups = range(SSM_GROUPS)

    dt = _softplus(dt_ref[...] + dtb_ref[...])
    gcum = _cumsum_rows(dt * (-jnp.exp(alog_ref[...])))
    gcum_t = _transpose(gcum)
    per_col = _spread(jnp.concatenate([dt, gcum], axis=0), sel_hd_ref[...])
    dt_x, gc_x = per_col[:c], per_col[c:]
    gc_colb = _spread(gcum, sel_c_ref[...])
    ge8 = jnp.broadcast_to(gc_x[c - 1:c, :], (SUBLANES, SSM_INNER))

    xr, br, cr = x_ref[...], b_ref[...], c_ref[...]
    x = _silu(_causal_conv_chunk(xr, tx_ref[...], cw_ref[:, :, :off_b], cb_ref[:, :off_b]))
    bm = _silu(_causal_conv_chunk(br, tb_ref[...], cw_ref[:, :, off_b:off_c], cb_ref[:, off_b:off_c]))
    cm = _silu(_causal_conv_chunk(cr, tc_ref[...], cw_ref[:, :, off_c:], cb_ref[:, off_c:]))
    tx_ref[...] = xr[c - SUBLANES:c]
    tb_ref[...] = br[c - SUBLANES:c]
    tc_ref[...] = cr[c - SUBLANES:c]

    xdt = x * dt_x
    xdec = xdt * jnp.exp(_add8(-gc_x, ge8))
    bg = [bm[:, g * ns:(g + 1) * ns] for g in groups]
    cg = [cm[:, g * ns:(g + 1) * ns] for g in groups]
    s_old = [s_ref[g] for g in groups]
    scores = [_dot(cc, bb, _NT) for cc, bb in zip(cg, bg)]
    y_inter = jnp.concatenate([_dot(cc, s) for cc, s in zip(cg, s_old)], axis=1)
    att = [scores[h // SSM_HPG] * _decay_matrix(gc_colb[:, h * c:(h + 1) * c], gcum_t[h:h + 1, :])
           for h in range(SSM_HEADS)]
    y_intra = jnp.concatenate([_dot(a, xdt[:, h * hd:(h + 1) * hd]) for h, a in enumerate(att)], axis=1)
    for g in groups:
        cols = slice(g * gw, (g + 1) * gw)
        s_ref[g] = _mul8(s_old[g], jnp.exp(ge8[:, cols])) + _dot(bg[g], xdec[:, cols], _TN)
    y = (y_intra + y_inter * jnp.exp(gc_x) + _mul8(x, dskip_ref[...])) * _silu(z_ref[...])
    for g in groups:
        cols = slice(g * gw, (g + 1) * gw)
        o_ref[:, cols] = _rms_rows(y[:, cols], nw_ref[:, cols]).astype(o_ref.dtype)

    @pl.when(c_idx == pl.num_programs(1) - 1)
    def _():
        for g in range(SSM_GROUPS):
            s_out_ref[g * SSM_HPG:(g + 1) * SSM_HPG] = _transpose(s_ref[g]).reshape(SSM_HPG, hd, ns)


def ssd_prompt(proj, conv_w, conv_b, dtb_row, alog_row, dskip_row, norm_w, chunk=128):
    b, l, _ = proj.shape
    assert chunk == LANES
    nc = l // chunk
    col = lambda width, off: pl.BlockSpec((None, chunk, width), lambda i, j: (i, j, off // width))
    row = _resident((1, LANES))
    lane = jnp.arange(LANES, dtype=jnp.int32)[:, None]
    sel_hd = (lane == jnp.arange(SSM_INNER, dtype=jnp.int32)[None, :] // SSM_HEAD_DIM).astype(BF16)
    sel_c = (lane == jnp.arange(SSM_HEADS * chunk, dtype=jnp.int32)[None, :] // chunk).astype(BF16)
    dskip8 = _rep8(jnp.repeat(dskip_row[0, :SSM_HEADS], SSM_HEAD_DIM))
    return pl.pallas_call(
        _ssd_prompt_kernel,
        grid=(b, nc),
        in_specs=[col(SSM_INNER, SSM_OFF_Z), col(SSM_INNER, SSM_OFF_X), col(SSM_BC_DIM, SSM_OFF_B),
                  col(SSM_BC_DIM, SSM_OFF_C), col(LANES, SSM_OFF_DT),
                  _resident((CONV_WIDTH, SUBLANES, SSM_CONV_DIM)), _resident((SUBLANES, SSM_CONV_DIM)),
                  row, row, _resident((SUBLANES, SSM_INNER)), _resident((SUBLANES, SSM_INNER)),
                  _resident(sel_hd.shape), _resident(sel_c.shape)],
        out_specs=[pl.BlockSpec((None, chunk, SSM_INNER), lambda i, j: (i, j, 0)),
                   pl.BlockSpec((None, SSM_HEADS, SSM_HEAD_DIM, SSM_STATE), lambda i, j: (i, 0, 0, 0))],
        out_shape=[jax.ShapeDtypeStruct((b, l, SSM_INNER), BF16),
                   jax.ShapeDtypeStruct((b, SSM_HEADS, SSM_HEAD_DIM, SSM_STATE), F32)],
        scratch_shapes=[pltpu.VMEM((SSM_GROUPS, SSM_STATE, SSM_GROUP_COLS), F32),
                        pltpu.VMEM((SUBLANES, SSM_INNER), F32),
                        pltpu.VMEM((SUBLANES, SSM_BC_DIM), F32), pltpu.VMEM((SUBLANES, SSM_BC_DIM), F32)],
        compiler_params=_cparams(("parallel", "arbitrary")),
        name="ssd_prompt",
    )(proj, proj, proj, proj, proj, _rep8_rows(conv_w), _rep8(conv_b), dtb_row, alog_row, dskip8,
      _rep8(norm_w), sel_hd, sel_c)


def _stacked_out(acc, n_inputs, out_index):
    return [pl.BlockSpec(memory_space=pl.ANY)], [acc], {n_inputs: out_index}


def _conv_step(x, buf_ref, w8_ref, bias8=None):
    acc = x * w8_ref[CONV_WIDTH - 1]
    for i in range(CONV_WIDTH - 1):
        acc = acc + buf_ref[i] * w8_ref[i]
    if bias8 is not None:
        acc = acc + bias8
    return acc


def _conv_state_kernel(buf_ref, x_ref, *rest):
    out_ref = rest[-1]
    for i in range(CONV_WIDTH - 2):
        out_ref[i] = buf_ref[i + 1]
    out_ref[CONV_WIDTH - 2] = x_ref[...]


def conv_state_update(conv_t, proj, col_off, layer, acc, wc):
    _, nbuf, b, n = conv_t.shape
    spec = pl.BlockSpec((None, nbuf, b, wc), lambda j: (layer, 0, 0, j))
    acc_specs, acc_args, aliases = _stacked_out(acc, 2, 0)
    return pl.pallas_call(
        _conv_state_kernel,
        grid=(n // wc,),
        in_specs=[spec, pl.BlockSpec((b, wc), lambda j: (0, col_off // wc + j))] + acc_specs,
        out_specs=spec,
        out_shape=jax.ShapeDtypeStruct(conv_t.shape, F32),
        input_output_aliases=aliases,
        compiler_params=_cparams(("parallel",)),
        name="conv_state",
    )(conv_t, proj, *acc_args)


def _gdn_decode_kernel(q_ref, k_ref, v_ref, z_ref, ba_ref, bq_ref, bk_ref, bv_ref, s_ref,
                       cwq_ref, cwk_ref, cwv_ref, alog_ref, dtb_ref, nw_ref, *rest):
    o_ref, s_out_ref = rest[-2:]
    h = pl.program_id(1)
    xq, xk, xv = q_ref[...], k_ref[...], v_ref[...]
    q = _silu(_conv_step(xq, bq_ref, cwq_ref))
    k = _silu(_conv_step(xk, bk_ref, cwk_ref))
    v = _silu(_conv_step(xv, bv_ref, cwv_ref))
    hd = GDN_DIM
    q = q * lax.rsqrt(jnp.sum(q * q, axis=-1, keepdims=True) + NORM_EPS) * (hd ** -0.5)
    k = k * lax.rsqrt(jnp.sum(k * k, axis=-1, keepdims=True) + NORM_EPS)
    ba = ba_ref[...]
    lane = _iota((1, LANES), 1)
    beta = jnp.sum(jnp.where(lane == h, _sigmoid(ba), 0.0), axis=1, keepdims=True)
    g = -jnp.exp(alog_ref[...]) * _softplus(ba + dtb_ref[...])
    eg = jnp.exp(jnp.sum(jnp.where(lane == GDN_HEADS + h, g, 0.0), axis=1, keepdims=True))
    qk = jnp.sum(q * k, axis=-1, keepdims=True)
    k_t = _transpose(k)
    q_t = _transpose(q)
    tb = xq.shape[0]
    ys = []
    for b in range(tb):
        s = s_ref[b, 0]
        k_col = k_t[:, b:b + 1]
        q_col = q_t[:, b:b + 1]
        ks = jnp.sum(k_col * s, axis=0, keepdims=True)
        egb = eg[b:b + 1, :]
        delta = beta[b:b + 1, :] * (v[b:b + 1, :] - egb * ks)
        s_new = egb * s + k_col * delta
        s_out_ref[b, 0] = s_new
        ys.append(jnp.sum(q_col * s_new, axis=0, keepdims=True))
    y = jnp.concatenate(ys, axis=0)
    y = _rms_rows(y, nw_ref[...]) * _silu(z_ref[...])
    o_ref[...] = y.astype(o_ref.dtype)


def gdn_decode(proj, state, conv_t, layer, acc, conv_w, alog_row, dtb_row, norm_w, tb=8):
    b = proj.shape[0]
    hd = GDN_DIM
    nh = GDN_HEADS
    pcol = lambda off: pl.BlockSpec((tb, hd), lambda i, h: (i, off // hd + h))
    ccol = lambda part: pl.BlockSpec((None, CONV_WIDTH - 1, tb, hd), lambda i, h: (layer, 0, i, part * nh + h))
    wcol = lambda part: pl.BlockSpec((CONV_WIDTH, SUBLANES, hd), lambda i, h: (0, 0, part * nh + h))
    row = _resident((1, LANES))
    sspec = pl.BlockSpec((None, tb, 1, hd, hd), lambda i, h: (layer, i, h, 0, 0))
    acc_specs, acc_args, aliases = _stacked_out(acc, 15, 1)
    assert tb == SUBLANES
    conv_w8 = _rep8_rows(conv_w)
    return pl.pallas_call(
        _gdn_decode_kernel,
        grid=(b // tb, nh),
        in_specs=[pcol(HYB_OFF_QKV), pcol(HYB_OFF_QKV + GDN_OUT_DIM), pcol(HYB_OFF_QKV + 2 * GDN_OUT_DIM),
                  pcol(HYB_OFF_Z), pl.BlockSpec((tb, LANES), lambda i, h: (i, HYB_OFF_BA // LANES)),
                  ccol(0), ccol(1), ccol(2), sspec,
                  wcol(0), wcol(1), wcol(2), row, row, _resident((SUBLANES, hd))] + acc_specs,
        out_specs=[pl.BlockSpec((tb, hd), lambda i, h: (i, h)), sspec],
        out_shape=[jax.ShapeDtypeStruct((b, GDN_OUT_DIM), BF16), jax.ShapeDtypeStruct(state.shape, F32)],
        input_output_aliases=aliases,
        compiler_params=_cparams(("parallel", "parallel")),
        name="gdn_decode",
    )(proj, proj, proj, proj, proj, conv_t, conv_t, conv_t, state,
      conv_w8, conv_w8, conv_w8, alog_row, dtb_row, _rep8(norm_w), *acc_args)


def _ret_decode_kernel(q_ref, k_ref, v_ref, gate_ref, s_ref, cos_ref, sin_ref, nw_ref, *rest):
    o_ref, s_out_ref = rest[-2:]
    q = _rotary(q_ref[...], cos_ref[...], sin_ref[...])
    k = _rotary(k_ref[...], cos_ref[...], sin_ref[...]) * (RET_KEY_DIM ** -0.5)
    q_t = _transpose(q)
    k_t = _transpose(k)
    tb = q.shape[0]
    kd, vd = RET_KEY_DIM, RET_VAL_DIM
    rows = []
    for b in range(tb):
        ys = []
        for h in range(RET_HEADS):
            gamma = math.exp(_ret_log_gamma(h))
            s = s_ref[b, h]
            s_new = s * gamma + k_t[h * kd:(h + 1) * kd, b:b + 1] * v_ref[b:b + 1, h * vd:(h + 1) * vd]
            s_out_ref[b, h] = s_new
            ys.append(jnp.sum(q_t[h * kd:(h + 1) * kd, b:b + 1] * s_new, axis=0, keepdims=True))
        rows.append(jnp.concatenate(ys, axis=1))
    y = jnp.concatenate(rows, axis=0)
    for h in range(RET_HEADS):
        yh = (_mul8(_group_norm_rows(y[:, h * vd:(h + 1) * vd]), nw_ref[:, h * vd:(h + 1) * vd])
              * _silu(gate_ref[:, h * vd:(h + 1) * vd]))
        o_ref[:, h * vd:(h + 1) * vd] = yh.astype(o_ref.dtype)


def ret_decode(proj, state, layer, acc, cos_f, sin_s, norm_w, tb=8):
    b = proj.shape[0]
    pcol = lambda width, off: pl.BlockSpec((tb, width), lambda i: (i, off // width))
    sspec = pl.BlockSpec((None, tb, RET_HEADS, RET_KEY_DIM, RET_VAL_DIM), lambda i: (layer, i, 0, 0, 0))
    acc_specs, acc_args, aliases = _stacked_out(acc, 8, 1)
    return pl.pallas_call(
        _ret_decode_kernel,
        grid=(b // tb,),
        in_specs=[pcol(RET_QK_DIM, HYB_OFF_RQ), pcol(RET_QK_DIM, HYB_OFF_RK), pcol(RET_OUT_DIM, HYB_OFF_RV),
                  pcol(RET_OUT_DIM, HYB_OFF_RG), sspec, _resident((1, RET_QK_DIM)), _resident((1, RET_QK_DIM)),
                  _resident((SUBLANES, RET_OUT_DIM))] + acc_specs,
        out_specs=[pl.BlockSpec((tb, RET_OUT_DIM), lambda i: (i, 0)), sspec],
        out_shape=[jax.ShapeDtypeStruct((b, RET_OUT_DIM), BF16), jax.ShapeDtypeStruct(state.shape, F32)],
        input_output_aliases=aliases,
        compiler_params=_cparams(("parallel",)),
        name="ret_decode",
    )(proj, proj, proj, proj, state, cos_f, sin_s, _rep8(norm_w), *acc_args)


def _ssd_decode_kernel(z_ref, x_ref, b_ref, c_ref, dt_ref, bx_ref, bb_ref, bc_ref, s_ref,
                       cwx_ref, cwb_ref, cwc_ref, cbx_ref, cbb_ref, cbc_ref,
                       dtb_ref, alog_ref, dskip_ref, nw_ref, *rest):
    o_ref, s_out_ref = rest[-2:]
    grp = pl.program_id(1)
    xr, br, cr = x_ref[...], b_ref[...], c_ref[...]
    x = _silu(_conv_step(xr, bx_ref, cwx_ref, cbx_ref[...]))
    bm = _silu(_conv_step(br, bb_ref, cwb_ref, cbb_ref[...]))
    cm = _silu(_conv_step(cr, bc_ref, cwc_ref, cbc_ref[...]))
    tb = xr.shape[0]
    hd, gw = SSM_HEAD_DIM, SSM_GROUP_COLS
    dt = _softplus(dt_ref[...] + dtb_ref[...])
    spread = (_iota((LANES, gw), 0) == grp * SSM_HPG + _iota((LANES, gw), 1) // hd).astype(BF16)
    per_head = jnp.concatenate([dt, -jnp.exp(alog_ref[...]), dskip_ref[...],
                                jnp.zeros((2 * SUBLANES - tb - 2, LANES), F32)], axis=0)
    ph_hi, ph_lo = _split(per_head)
    per_col = (jnp.dot(ph_hi, spread, preferred_element_type=F32)
               + jnp.dot(ph_lo, spread, preferred_element_type=F32))
    dt_x, a_x, dskip_x = per_col[0:tb], per_col[tb:tb + 1], per_col[tb + 1:tb + 2]
    eg_x = jnp.exp(dt_x * a_x)
    xdt_t = _split((x * dt_x).T)
    row_id = _iota(bm.shape, 0)
    outer = [_dot3(xdt_t, _split(jnp.where(row_id == b, bm, 0.0))) for b in range(tb)]
    for b in range(tb):
        for hl in range(SSM_HPG):
            s_out_ref[b, hl] = (s_ref[b, hl] * eg_x[b:b + 1, hl * hd:hl * hd + 1]
                                + outer[b][hl * hd:(hl + 1) * hd, :])
    rows = [_dot(cm[b:b + 1, :], s_out_ref[b].reshape(gw, SSM_STATE), _NT) for b in range(tb)]
    y = (jnp.concatenate(rows, axis=0) + dskip_x * x) * _silu(z_ref[...])
    o_ref[...] = _rms_rows(y, nw_ref[...]).astype(o_ref.dtype)


def ssd_decode(proj, state_t, conv_t, layer, acc, conv_w, conv_b, dtb_row, alog_row, dskip_row, norm_w, tb=8):
    b = proj.shape[0]
    gw = SSM_GROUP_COLS
    ns = SSM_STATE
    assert tb == SUBLANES
    cw8, cb8, nw8 = _rep8_rows(conv_w), _rep8(conv_b), _rep8(norm_w)
    pcol = lambda width, off: pl.BlockSpec((tb, width), lambda i, g: (i, off // width + g))
    ccol = lambda width, off: pl.BlockSpec((None, CONV_WIDTH - 1, tb, width),
                                           lambda i, g: (layer, 0, i, off // width + g))
    wcol = lambda width, off: pl.BlockSpec((CONV_WIDTH, SUBLANES, width), lambda i, g: (0, 0, off // width + g))
    rcol = lambda width, off: pl.BlockSpec((SUBLANES, width), lambda i, g: (0, off // width + g))
    row = _resident((1, LANES))
    sspec = pl.BlockSpec((None, tb, SSM_HPG, SSM_HEAD_DIM, ns), lambda i, g: (layer, i, g, 0, 0))
    off_b, off_c = SSM_INNER, SSM_INNER + SSM_BC_DIM
    acc_specs, acc_args, aliases = _stacked_out(acc, 19, 1)
    return pl.pallas_call(
        _ssd_decode_kernel,
        grid=(b // tb, SSM_GROUPS),
        in_specs=[pcol(gw, SSM_OFF_Z), pcol(gw, SSM_OFF_X), pcol(ns, SSM_OFF_B), pcol(ns, SSM_OFF_C),
                  pl.BlockSpec((tb, LANES), lambda i, g: (i, SSM_OFF_DT // LANES)),
                  ccol(gw, 0), ccol(ns, off_b), ccol(ns, off_c), sspec,
                  wcol(gw, 0), wcol(ns, off_b), wcol(ns, off_c),
                  rcol(gw, 0), rcol(ns, off_b), rcol(ns, off_c),
                  row, row, row, rcol(gw, 0)] + acc_specs,
        out_specs=[pl.BlockSpec((tb, gw), lambda i, g: (i, g)), sspec],
        out_shape=[jax.ShapeDtypeStruct((b, SSM_INNER), BF16), jax.ShapeDtypeStruct(state_t.shape, F32)],
        input_output_aliases=aliases,
        compiler_params=_cparams(("parallel", "parallel")),
        name="ssd_decode",
    )(proj, proj, proj, proj, proj, conv_t, conv_t, conv_t, state_t,
      cw8, cw8, cw8, cb8, cb8, cb8, dtb_row, alog_row, dskip_row, nw8, *acc_args)


def _lane_row(vals, offset):
    return jnp.zeros((1, LANES), F32).at[0, offset:offset + vals.shape[0]].set(vals.astype(F32))


def _prep_hyb_w_in(w):
    sizes = (GDN_QKV_DIM, GDN_OUT_DIM, GDN_HEADS, GDN_HEADS, RET_QK_DIM, RET_QK_DIM, RET_OUT_DIM, RET_OUT_DIM)
    offs = [0]
    for s in sizes:
        offs.append(offs[-1] + s)
    part = lambda i: w[:, offs[i]:offs[i + 1]]
    cols = [part(0), part(1), part(4), part(5), part(6), part(7), part(2), part(3)]
    out = jnp.concatenate(cols, axis=1)
    return jnp.pad(out, ((0, 0), (0, HYB_N - out.shape[1]))).astype(BF16)


def _prep_ssm_w_in(w):
    return jnp.pad(w, ((0, 0), (0, SSM_N - w.shape[1]))).astype(BF16)


def _rope_tables(pos):
    half = RET_KEY_DIM // 2
    inv_freq = ROPE_BASE ** (-jnp.arange(half, dtype=F32) / half)
    ang = pos.astype(F32)[:, None] * inv_freq[None, :]
    cos, sin = jnp.cos(ang), jnp.sin(ang)
    cos_f = jnp.tile(jnp.concatenate([cos, cos], axis=1), (1, RET_HEADS))
    sin_s = jnp.tile(jnp.concatenate([-sin, sin], axis=1), (1, RET_HEADS))
    return cos_f, sin_s


def _trunk(x, pos, states, params, prompt):
    bsz, l, d = x.shape
    m = bsz * l
    tm = 512 if m % 512 == 0 else m
    x2 = x.reshape(m, d)
    cos_f, sin_s = _rope_tables(pos)
    new_gdn, new_gdn_conv, new_ret, new_ssm, new_ssm_conv = [], [], [], [], []
    if not prompt:
        s_gdn, c_gdn, s_ret, s_ssm, c_ssm = states
        c_gdn_t, c_ssm_t = jnp.swapaxes(c_gdn, 1, 2), jnp.swapaxes(c_ssm, 1, 2)
        s_ssm_t = jnp.swapaxes(s_ssm, 3, 4)
        acc_gdn, acc_gdn_conv, acc_ret, acc_ssm, acc_ssm_conv = (
            jnp.zeros_like(a) for a in (s_gdn, c_gdn_t, s_ret, s_ssm_t, c_ssm_t))
    for layer in range(DEPTH):
        i = layer // 2
        p = params
        if layer % 2 == 0:
            proj = norm_matmul(x2, p["norm_mix"][layer], p["w_in_hyb"][i], tm)
            alog_row = _lane_row(p["gdn_a_log"][i], GDN_HEADS)
            dtb_row = _lane_row(p["gdn_dt_bias"][i], GDN_HEADS)
            if prompt:
                proj3 = proj.reshape(bsz, l, HYB_N)
                o_a, sg = gdn_prompt(proj3, p["gdn_conv_w"][i], alog_row, dtb_row, p["gdn_norm_w"][i])
                o_b, sr = ret_prompt(proj3, cos_f, sin_s, p["ret_norm_w"][i])
                cg = proj3[:, l - (CONV_WIDTH - 1):, :GDN_QKV_DIM]
                o_a, o_b = o_a.reshape(m, GDN_OUT_DIM), o_b.reshape(m, RET_OUT_DIM)
                new_gdn.append(sg)
                new_gdn_conv.append(cg)
                new_ret.append(sr)
            else:
                o_a, acc_gdn = gdn_decode(proj, s_gdn, c_gdn_t, i, acc_gdn, p["gdn_conv_w"][i], alog_row,
                                          dtb_row, p["gdn_norm_w"][i])
                acc_gdn_conv = conv_state_update(c_gdn_t, proj, HYB_OFF_QKV, i, acc_gdn_conv, GDN_OUT_DIM)
                o_b, acc_ret = ret_decode(proj, s_ret, i, acc_ret, cos_f, sin_s, p["ret_norm_w"][i])
            w_out = p["w_out_hyb"][i]
            x2 = out_proj_residual([o_a, o_b], [w_out[:GDN_OUT_DIM], w_out[GDN_OUT_DIM:]], x2, tm)
        else:
            proj = norm_matmul(x2, p["norm_mix"][layer], p["w_in_ssm"][i], tm)
            dtb_row = _lane_row(p["ssm_dt_bias"][i], 0)
            alog_row = _lane_row(p["ssm_a_log"][i], 0)
            dskip_row = _lane_row(p["ssm_d"][i], 0)
            if prompt:
                proj3 = proj.reshape(bsz, l, SSM_N)
                y, ss = ssd_prompt(proj3, p["ssm_conv_w"][i], p["ssm_conv_b"][i], dtb_row, alog_row,
                                   dskip_row, p["ssm_norm_w"][i])
                cs = proj3[:, l - (CONV_WIDTH - 1):, SSM_OFF_X:SSM_OFF_X + SSM_CONV_DIM]
                y = y.reshape(m, SSM_INNER)
                new_ssm.append(ss)
                new_ssm_conv.append(cs)
            else:
                y, acc_ssm = ssd_decode(proj, s_ssm_t, c_ssm_t, i, acc_ssm, p["ssm_conv_w"][i],
                                        p["ssm_conv_b"][i], dtb_row, alog_row, dskip_row, p["ssm_norm_w"][i])
                acc_ssm_conv = conv_state_update(c_ssm_t, proj, SSM_OFF_X, i, acc_ssm_conv, D_MODEL)
            x2 = out_proj_residual([y], [p["w_out_ssm"][i]], x2, tm)
        x2 = mlp_residual(x2, p["norm_mlp"][layer], p["mlp_w1"][layer], p["mlp_w2"][layer],
                          p["norm_final"], layer == DEPTH - 1, tm)
    y_out = x2.reshape(bsz, l, d)
    if prompt:
        return (y_out, jnp.stack(new_gdn), jnp.stack(new_gdn_conv), jnp.stack(new_ret),
                jnp.swapaxes(jnp.stack(new_ssm), 3, 4), jnp.stack(new_ssm_conv))
    return (y_out, acc_gdn, jnp.swapaxes(acc_gdn_conv, 1, 2), acc_ret,
            jnp.swapaxes(acc_ssm, 3, 4), jnp.swapaxes(acc_ssm_conv, 1, 2))


def kernel(x_prompt, x_sample, state_gdn, state_gdn_conv, state_ret, state_ssm, state_ssm_conv, norm_mix, norm_mlp, norm_final, w_in_hyb, gdn_conv_w, gdn_a_log, gdn_dt_bias, gdn_norm_w, ret_norm_w, w_out_hyb, w_in_ssm, ssm_conv_w, ssm_conv_b, ssm_dt_bias, ssm_a_log, ssm_d, ssm_norm_w, w_out_ssm, mlp_w1, mlp_w2):
    n_hyb, n_ssm = w_in_hyb.shape[0], w_in_ssm.shape[0]
    params = dict(
        norm_mix=norm_mix, norm_mlp=norm_mlp, norm_final=norm_final,
        w_in_hyb=[_prep_hyb_w_in(w_in_hyb[i]) for i in range(n_hyb)],
        gdn_conv_w=gdn_conv_w, gdn_a_log=gdn_a_log, gdn_dt_bias=gdn_dt_bias, gdn_norm_w=gdn_norm_w,
        ret_norm_w=ret_norm_w, w_out_hyb=w_out_hyb.astype(BF16),
        w_in_ssm=[_prep_ssm_w_in(w_in_ssm[i]) for i in range(n_ssm)],
        ssm_conv_w=ssm_conv_w, ssm_conv_b=ssm_conv_b, ssm_dt_bias=ssm_dt_bias, ssm_a_log=ssm_a_log,
        ssm_d=ssm_d, ssm_norm_w=ssm_norm_w, w_out_ssm=w_out_ssm.astype(BF16),
        mlp_w1=mlp_w1.astype(BF16), mlp_w2=mlp_w2.astype(BF16))
    lp, ls = x_prompt.shape[1], x_sample.shape[1]
    pos_prompt = jnp.arange(lp, dtype=jnp.int32)
    pos_sample = PAST_LEN + jnp.arange(ls, dtype=jnp.int32)
    y_p, p_gdn, p_gdn_conv, p_ret, p_ssm, p_ssm_conv = _trunk(
        x_prompt, pos_prompt, (None,) * 5, params, prompt=True)
    y_s, s_gdn, s_gdn_conv, s_ret, s_ssm, s_ssm_conv = _trunk(
        x_sample, pos_sample, (state_gdn, state_gdn_conv, state_ret, state_ssm, state_ssm_conv),
        params, prompt=False)
    return (y_p, y_s, p_gdn, p_gdn_conv, p_ret, p_ssm, p_ssm_conv,
            s_gdn, s_gdn_conv, s_ret, s_ssm, s_ssm_conv)
```

```python
import functools
import math

import jax
import jax.numpy as jnp
from jax import lax
from jax.experimental import pallas as pl
from jax.experimental.pallas import tpu as pltpu

F32 = jnp.float32
BF16 = jnp.bfloat16

D_MODEL = 1024
DEPTH = 4
CONV_WIDTH = 4
NORM_EPS = 1e-6
PAST_LEN = 16384

GDN_HEADS = 4
GDN_DIM = 128
GDN_QKV_DIM = 3 * GDN_HEADS * GDN_DIM
GDN_OUT_DIM = GDN_HEADS * GDN_DIM

RET_HEADS = 4
RET_KEY_DIM = 64
RET_VAL_DIM = 128
RET_QK_DIM = RET_HEADS * RET_KEY_DIM
RET_OUT_DIM = RET_HEADS * RET_VAL_DIM
ROPE_BASE = 10000.0

SSM_INNER = 2 * D_MODEL
SSM_HEAD_DIM = 64
SSM_HEADS = SSM_INNER // SSM_HEAD_DIM
SSM_GROUPS = 4
SSM_HPG = SSM_HEADS // SSM_GROUPS
SSM_STATE = 128
SSM_GROUP_COLS = SSM_HPG * SSM_HEAD_DIM
SSM_BC_DIM = SSM_GROUPS * SSM_STATE
SSM_CONV_DIM = SSM_INNER + 2 * SSM_BC_DIM
MLP_HIDDEN = 4 * D_MODEL

LANES = 128
SUBLANES = 8

HYB_OFF_QKV = 0
HYB_OFF_Z = GDN_QKV_DIM
HYB_OFF_RQ = HYB_OFF_Z + GDN_OUT_DIM
HYB_OFF_RK = HYB_OFF_RQ + RET_QK_DIM
HYB_OFF_RV = HYB_OFF_RK + RET_QK_DIM
HYB_OFF_RG = HYB_OFF_RV + RET_OUT_DIM
HYB_OFF_BA = HYB_OFF_RG + RET_OUT_DIM
HYB_N = 3840
SSM_OFF_Z = 0
SSM_OFF_X = SSM_INNER
SSM_OFF_B = SSM_OFF_X + SSM_INNER
SSM_OFF_C = SSM_OFF_B + SSM_BC_DIM
SSM_OFF_DT = SSM_OFF_C + SSM_BC_DIM
SSM_N = 5376
PROJ_TN = 768

VMEM_LIMIT = 56 * 1024 * 1024

_NN = (((1,), (0,)), ((), ()))
_NT = (((1,), (1,)), ((), ()))
_TN = (((0,), (0,)), ((), ()))


def _dot(a, b, dims=_NN):
    return lax.dot_general(a.astype(BF16), b.astype(BF16), dims, preferred_element_type=F32)


def _dot_f32(a, b, dims=_NN):
    return lax.dot_general(a, b, dims, precision=lax.Precision.HIGHEST, preferred_element_type=F32)


def _sigmoid(x):
    return 0.5 + 0.5 * jnp.tanh(0.5 * x)


def _silu(x):
    t = 0.5 * x
    return t + t * jnp.tanh(t)


def _softplus(x):
    return jnp.maximum(x, 0.0) + jnp.log(1.0 + jnp.exp(-jnp.abs(x)))


def _iota(shape, dim):
    return lax.broadcasted_iota(jnp.int32, shape, dim)


def _eye(n):
    return (_iota((n, n), 0) == _iota((n, n), 1)).astype(F32)


def _transpose(x):
    return x.T


def _cparams(sem):
    return pltpu.CompilerParams(dimension_semantics=sem, vmem_limit_bytes=VMEM_LIMIT)


def _resident(shape):
    nd = len(shape)
    return pl.BlockSpec(shape, lambda *_: (0,) * nd, pipeline_mode=pl.Buffered(1))


def _rep8(v):
    v = v.reshape(1, -1)
    return jnp.broadcast_to(v, (SUBLANES, v.shape[1]))


def _rep8_rows(w):
    return jnp.broadcast_to(w[:, None, :], (w.shape[0], SUBLANES, w.shape[1]))


def _mul8(x, r8):
    r, n = x.shape
    return (x.reshape(r // SUBLANES, SUBLANES, n) * r8).reshape(r, n)


def _add8(x, r8):
    r, n = x.shape
    return (x.reshape(r // SUBLANES, SUBLANES, n) + r8).reshape(r, n)


def _rms_rows(x, gain8):
    return _mul8(x * lax.rsqrt(jnp.mean(x * x, axis=-1, keepdims=True) + NORM_EPS), gain8)


def _norm_matmul_kernel(x_ref, g_ref, w_ref, o_ref, *, tn):
    xn = _rms_rows(x_ref[...], g_ref[...]).astype(BF16)
    for j in range(w_ref.shape[1] // tn):
        o_ref[:, j * tn:(j + 1) * tn] = jnp.dot(xn, w_ref[:, j * tn:(j + 1) * tn],
                                                preferred_element_type=F32)


def norm_matmul(x, gain, w, tm):
    m, d = x.shape
    n = w.shape[1]
    return pl.pallas_call(
        functools.partial(_norm_matmul_kernel, tn=PROJ_TN),
        grid=(m // tm,),
        in_specs=[pl.BlockSpec((tm, d), lambda i: (i, 0)), _resident((SUBLANES, d)), _resident((d, n))],
        out_specs=pl.BlockSpec((tm, n), lambda i: (i, 0)),
        out_shape=jax.ShapeDtypeStruct((m, n), F32),
        compiler_params=_cparams(("parallel",)),
        name="norm_matmul",
    )(x, _rep8(gain), w)


def _post_mixer_kernel(*refs, n_in, tf, final_norm):
    a_refs, w_refs = refs[:n_in], refs[n_in:2 * n_in]
    r_ref, g_ref, w1_ref, w2_ref, gf_ref, o_ref = refs[2 * n_in:]
    x = r_ref[...]
    for a_ref, w_ref in zip(a_refs, w_refs):
        x = x + jnp.dot(a_ref[...], w_ref[...], preferred_element_type=F32)
    o_ref[...] = x
    x = o_ref[...]
    xn = _rms_rows(x, g_ref[...]).astype(BF16)
    acc = x
    for j in range(w1_ref.shape[1] // tf):
        h = jnp.dot(xn, w1_ref[:, j * tf:(j + 1) * tf], preferred_element_type=F32)
        h = jnp.maximum(h, 0.0)
        acc = acc + jnp.dot((h * h).astype(BF16), w2_ref[j * tf:(j + 1) * tf, :],
                            preferred_element_type=F32)
    if final_norm:
        acc = _rms_rows(acc, gf_ref[...])
    o_ref[...] = acc


def post_mixer(acts, ws, res, gain, w1, w2, final_gain, final_norm, tm):
    m, d = res.shape
    f = w1.shape[1]
    n_in = len(acts)
    in_specs = ([pl.BlockSpec((tm, a.shape[1]), lambda i: (i, 0)) for a in acts]
                + [_resident(w.shape) for w in ws]
                + [pl.BlockSpec((tm, d), lambda i: (i, 0)), _resident((SUBLANES, d)), _resident((d, f)),
                   _resident((f, d)), _resident((SUBLANES, d))])
    return pl.pallas_call(
        functools.partial(_post_mixer_kernel, n_in=n_in, tf=1024, final_norm=final_norm),
        grid=(m // tm,),
        in_specs=in_specs,
        out_specs=pl.BlockSpec((tm, d), lambda i: (i, 0)),
        out_shape=jax.ShapeDtypeStruct((m, d), F32),
        compiler_params=_cparams(("parallel",)),
        name="post_mixer",
    )(*acts, *ws, res, _rep8(gain), w1, w2, _rep8(final_gain))


def _causal_conv_chunk(x, tail, w8, bias8=None):
    c, n = x.shape
    g = c // SUBLANES
    x3 = x.reshape(g, SUBLANES, n)
    acc = x3 * w8[CONV_WIDTH - 1]
    row = _iota((SUBLANES, n), 0)
    for k in range(1, CONV_WIDTH):
        r = pltpu.roll(x3, k, 1)
        prev = jnp.concatenate([pltpu.roll(tail, k, 0)[None], r[:g - 1]], axis=0)
        acc = acc + jnp.where(row < k, prev, r) * w8[CONV_WIDTH - 1 - k]
    if bias8 is not None:
        acc = acc + bias8
    return acc.reshape(c, n)


def _cumsum_rows(g):
    c = g.shape[0]
    tri = (_iota((c, c), 0) >= _iota((c, c), 1)).astype(F32)
    return _dot_f32(tri, g)


def _decay_matrix(gc_col, gc_row):
    c = gc_col.shape[0]
    causal = _iota((c, c), 0) >= _iota((c, c), 1)
    return jnp.where(causal, jnp.exp(gc_col - gc_row), 0.0)


def _split(x):
    hi = x.astype(BF16)
    return hi, (x - hi.astype(F32)).astype(BF16)


def _dot3(a, b, dims=_NN):
    (ah, al), (bh, bl) = a, b
    d = lambda p, q: lax.dot_general(p, q, dims, preferred_element_type=F32)
    return d(ah, bh) + (d(ah, bl) + d(al, bh))


def _unit_lower_inverses(n_list):
    c = n_list[0].shape[0]
    ri, ci = _iota((c, c), 0), _iota((c, c), 1)
    eye = (ri == ci).astype(F32)
    blk = 16
    same = (ri // blk) == (ci // blk)
    p = [jnp.where(same, -n, 0.0).astype(BF16) for n in n_list]
    inv = [eye + x for x in p]
    for _ in range(3):
        p = [_dot(x, x).astype(BF16) for x in p]
        inv = [i + _dot(i, x) for i, x in zip(inv, p)]
    while blk < c:
        pair = ((ri // (2 * blk)) == (ci // (2 * blk))) & ((ri // blk) != (ci // blk))
        inv_b = [i.astype(BF16) for i in inv]
        t = [_dot(jnp.where(pair, n, 0.0), i) for n, i in zip(n_list, inv_b)]
        inv = [i - _dot(i_b, x) for i, i_b, x in zip(inv, inv_b, t)]
        blk *= 2
    return inv


def _gdn_prompt_kernel(qkv_ref, z_ref, ba_ref, cw_ref, alog_ref, dtb_ref, nw_ref,
                       o_ref, s_out_ref, s_ref, tail_ref):
    c_idx = pl.program_id(1)

    @pl.when(c_idx == 0)
    def _():
        s_ref[...] = jnp.zeros_like(s_ref)
        tail_ref[...] = jnp.zeros_like(tail_ref)

    nb, c = qkv_ref.shape[0], qkv_ref.shape[1]
    hd = GDN_DIM
    ri, ci = _iota((c, c), 0), _iota((c, c), 1)
    strict = ri > ci
    chains = [(bb, h) for bb in range(nb) for h in range(GDN_HEADS)]

    conv, beta, gcum, gcum_t = [], [], [], []
    for bb in range(nb):
        x = qkv_ref[bb]
        conv.append(_silu(_causal_conv_chunk(x, tail_ref[bb], cw_ref[...])))
        tail_ref[bb] = x[c - SUBLANES:c]
        ba = ba_ref[bb]
        beta.append(_sigmoid(ba))
        gc = _cumsum_rows(-jnp.exp(alog_ref[...]) * _softplus(ba + dtb_ref[...]))
        gcum.append(gc)
        gcum_t.append(_transpose(gc))

    q, k, v, b_col, gc_col, g_end, decay = [], [], [], [], [], [], []
    for bb, h in chains:
        qh = conv[bb][:, h * hd:(h + 1) * hd]
        kh = conv[bb][:, GDN_OUT_DIM + h * hd:GDN_OUT_DIM + (h + 1) * hd]
        q.append(qh * lax.rsqrt(jnp.sum(qh * qh, axis=-1, keepdims=True) + NORM_EPS) * (hd ** -0.5))
        k.append(kh * lax.rsqrt(jnp.sum(kh * kh, axis=-1, keepdims=True) + NORM_EPS))
        v.append(conv[bb][:, 2 * GDN_OUT_DIM + h * hd:2 * GDN_OUT_DIM + (h + 1) * hd])
        b_col.append(beta[bb][:, h:h + 1])
        gc_col.append(gcum[bb][:, GDN_HEADS + h:GDN_HEADS + h + 1])
        g_end.append(gcum[bb][c - 1:c, GDN_HEADS + h:GDN_HEADS + h + 1])
        decay.append(_decay_matrix(gc_col[-1], gcum_t[bb][GDN_HEADS + h:GDN_HEADS + h + 1, :]))
    kk = [_dot(x, x, _NT) for x in k]
    qk = [_dot(x, y, _NT) * d for x, y, d in zip(q, k, decay)]
    t_inv = _unit_lower_inverses([jnp.where(strict, b * x * d, 0.0) for b, x, d in zip(b_col, kk, decay)])
    rhs = [jnp.concatenate([vv * b, kx * (b * jnp.exp(g))], axis=1)
           for vv, kx, b, g in zip(v, k, b_col, gc_col)]
    sol = [_dot(t, r) for t, r in zip(t_inv, rhs)]
    s_old = [s_ref[bb, h] for bb, h in chains]
    delta = [x[:, :hd] - _dot(x[:, hd:], s) for x, s in zip(sol, s_old)]
    y = [_dot(x * jnp.exp(g), s) + _dot(a, dl)
         for x, g, s, a, dl in zip(q, gc_col, s_old, qk, delta)]
    s_new = [s * jnp.exp(ge) + _dot(kx * jnp.exp(ge - g), dl, _TN)
             for s, ge, kx, g, dl in zip(s_old, g_end, k, gc_col, delta)]
    for (bb, h), sn, yy in zip(chains, s_new, y):
        s_ref[bb, h] = sn
        out = _rms_rows(yy, nw_ref[...]) * _silu(z_ref[bb, :, h * hd:(h + 1) * hd])
        o_ref[bb, :, h * hd:(h + 1) * hd] = out.astype(o_ref.dtype)

    @pl.when(c_idx == pl.num_programs(1) - 1)
    def _():
        s_out_ref[...] = s_ref[...]


def gdn_prompt(proj, conv_w, alog_row, dtb_row, norm_w, chunk=64, nb=4):
    b, l, _ = proj.shape
    nc = l // chunk
    col = lambda width, off: pl.BlockSpec((nb, chunk, width), lambda i, j: (i, j, off // width))
    return pl.pallas_call(
        _gdn_prompt_kernel,
        grid=(b // nb, nc),
        in_specs=[col(GDN_QKV_DIM, HYB_OFF_QKV), col(GDN_OUT_DIM, HYB_OFF_Z), col(LANES, HYB_OFF_BA),
                  _resident((CONV_WIDTH, SUBLANES, GDN_QKV_DIM)), _resident((1, LANES)), _resident((1, LANES)),
                  _resident((SUBLANES, GDN_DIM))],
        out_specs=[pl.BlockSpec((nb, chunk, GDN_OUT_DIM), lambda i, j: (i, j, 0)),
                   pl.BlockSpec((nb, GDN_HEADS, GDN_DIM, GDN_DIM), lambda i, j: (i, 0, 0, 0))],
        out_shape=[jax.ShapeDtypeStruct((b, l, GDN_OUT_DIM), BF16),
                   jax.ShapeDtypeStruct((b, GDN_HEADS, GDN_DIM, GDN_DIM), F32)],
        scratch_shapes=[pltpu.VMEM((nb, GDN_HEADS, GDN_DIM, GDN_DIM), F32),
                        pltpu.VMEM((nb, SUBLANES, GDN_QKV_DIM), F32)],
        compiler_params=_cparams(("parallel", "arbitrary")),
        name="gdn_prompt",
    )(proj, proj, proj, _rep8_rows(conv_w), alog_row, dtb_row, _rep8(norm_w))


def _ret_log_gamma(h):
    return math.log(1.0 - 2.0 ** (-5.0 - h))


def _rotary(t, cos_f, sin_s):
    n = t.shape[1]
    half = RET_KEY_DIM // 2
    first = (_iota(t.shape, 1) % RET_KEY_DIM) < half
    swapped = jnp.where(first, pltpu.roll(t, n - half, 1), pltpu.roll(t, half, 1))
    return t * cos_f + swapped * sin_s


def _group_norm_rows(x):
    xc = x - jnp.mean(x, axis=-1, keepdims=True)
    return xc * lax.rsqrt(jnp.mean(xc * xc, axis=-1, keepdims=True) + NORM_EPS)


def _ret_prompt_kernel(q_ref, k_ref, v_ref, gate_ref, cos_ref, sin_ref, nw_ref,
                       o_ref, s_out_ref, s_ref):
    c_idx = pl.program_id(1)

    @pl.when(c_idx == 0)
    def _():
        s_ref[...] = jnp.zeros_like(s_ref)

    nb, c = q_ref.shape[0], q_ref.shape[1]
    kd, vd = RET_KEY_DIM, RET_VAL_DIM
    ri, ci = _iota((c, c), 0), _iota((c, c), 1)
    causal = ri >= ci
    pos_col = _iota((c, 1), 0).astype(F32)
    lgs = [_ret_log_gamma(h) for h in range(RET_HEADS)]
    decay = [jnp.where(causal, jnp.exp((ri - ci).astype(F32) * lg), 0.0) for lg in lgs]
    grow = [jnp.exp((pos_col + 1.0) * lg) for lg in lgs]
    gtail = [jnp.exp((c - 1.0 - pos_col) * lg) for lg in lgs]
    chains = [(bb, h) for bb in range(nb) for h in range(RET_HEADS)]
    q = [_rotary(q_ref[bb], cos_ref[...], sin_ref[...]) for bb in range(nb)]
    k = [_rotary(k_ref[bb], cos_ref[...], sin_ref[...]) * (kd ** -0.5) for bb in range(nb)]
    qh = [q[bb][:, h * kd:(h + 1) * kd] for bb, h in chains]
    kh = [k[bb][:, h * kd:(h + 1) * kd] for bb, h in chains]
    vh = [v_ref[bb, :, h * vd:(h + 1) * vd] for bb, h in chains]
    s_old = [s_ref[bb, h] for bb, h in chains]
    att = [_dot(a, b, _NT) * decay[h] for a, b, (_, h) in zip(qh, kh, chains)]
    y = [_dot(a, v) + _dot(qq, s) * grow[h] for a, v, qq, s, (_, h) in zip(att, vh, qh, s_old, chains)]
    s_new = [s * math.exp(c * lgs[h]) + _dot(kk * gtail[h], v, _TN)
             for s, kk, v, (_, h) in zip(s_old, kh, vh, chains)]
    for (bb, h), yy, sn in zip(chains, y, s_new):
        s_ref[bb, h] = sn
        out = (_mul8(_group_norm_rows(yy), nw_ref[:, h * vd:(h + 1) * vd])
               * _silu(gate_ref[bb, :, h * vd:(h + 1) * vd]))
        o_ref[bb, :, h * vd:(h + 1) * vd] = out.astype(o_ref.dtype)

    @pl.when(c_idx == pl.num_programs(1) - 1)
    def _():
        s_out_ref[...] = s_ref[...]


def ret_prompt(proj, cos_f, sin_s, norm_w, chunk=128, nb=2):
    b, l, _ = proj.shape
    nc = l // chunk
    col = lambda width, off: pl.BlockSpec((nb, chunk, width), lambda i, j: (i, j, off // width))
    tab = pl.BlockSpec((chunk, RET_QK_DIM), lambda i, j: (j, 0))
    return pl.pallas_call(
        _ret_prompt_kernel,
        grid=(b // nb, nc),
        in_specs=[col(RET_QK_DIM, HYB_OFF_RQ), col(RET_QK_DIM, HYB_OFF_RK), col(RET_OUT_DIM, HYB_OFF_RV),
                  col(RET_OUT_DIM, HYB_OFF_RG), tab, tab, _resident((SUBLANES, RET_OUT_DIM))],
        out_specs=[pl.BlockSpec((nb, chunk, RET_OUT_DIM), lambda i, j: (i, j, 0)),
                   pl.BlockSpec((nb, RET_HEADS, RET_KEY_DIM, RET_VAL_DIM), lambda i, j: (i, 0, 0, 0))],
        out_shape=[jax.ShapeDtypeStruct((b, l, RET_OUT_DIM), BF16),
                   jax.ShapeDtypeStruct((b, RET_HEADS, RET_KEY_DIM, RET_VAL_DIM), F32)],
        scratch_shapes=[pltpu.VMEM((nb, RET_HEADS, RET_KEY_DIM, RET_VAL_DIM), F32)],
        compiler_params=_cparams(("parallel", "arbitrary")),
        name="ret_prompt",
    )(proj, proj, proj, proj, cos_f, sin_s, _rep8(norm_w))


def _split3(x):
    a = x.astype(BF16)
    r = x - a.astype(F32)
    b = r.astype(BF16)
    return a, b, (r - b.astype(F32)).astype(BF16)


def _spread(x, sel):
    return sum(jnp.dot(p, sel, preferred_element_type=F32) for p in _split3(x))


def _ssd_prompt_kernel(z_ref, x_ref, b_ref, c_ref, dt_ref, cw_ref, cb_ref, dtb_ref, alog_ref, dskip_ref,
                       nw_ref, sel_hd_ref, sel_c_ref, o_ref, s_out_ref, s_ref, tx_ref, tb_ref, tc_ref):
    c_idx = pl.program_id(1)

    @pl.when(c_idx == 0)
    def _():
        s_ref[...] = jnp.zeros_like(s_ref)
        tx_ref[...] = jnp.zeros_like(tx_ref)
        tb_ref[...] = jnp.zeros_like(tb_ref)
        tc_ref[...] = jnp.zeros_like(tc_ref)

    c = x_ref.shape[0]
    hd, gw, ns = SSM_HEAD_DIM, SSM_GROUP_COLS, SSM_STATE
    off_b, off_c = SSM_INNER, SSM_INNER + SSM_BC_DIM
    groups = range(SSM_GROUPS)

    br, cr = b_ref[...], c_ref[...]
    bm = _silu(_causal_conv_chunk(br, tb_ref[...], cw_ref[:, :, off_b:off_c], cb_ref[:, off_b:off_c]))
    cm = _silu(_causal_conv_chunk(cr, tc_ref[...], cw_ref[:, :, off_c:], cb_ref[:, off_c:]))
    tb_ref[...] = br[c - SUBLANES:c]
    tc_ref[...] = cr[c - SUBLANES:c]
    bg = [bm[:, g * ns:(g + 1) * ns] for g in groups]
    cg = [cm[:, g * ns:(g + 1) * ns] for g in groups]
    s_old = [s_ref[g] for g in groups]
    scores = [_dot(cc, bb, _NT) for cc, bb in zip(cg, bg)]
    y_inter = jnp.concatenate([_dot(cc, s) for cc, s in zip(cg, s_old)], axis=1)

    dt = _softplus(dt_ref[...] + dtb_ref[...])
    gcum = _cumsum_rows(dt * (-jnp.exp(alog_ref[...])))
    gcum_t = _transpose(gcum)
    per_col = _spread(jnp.concatenate([dt, gcum], axis=0), sel_hd_ref[...])
    dt_x, gc_x = per_col[:c], per_col[c:]
    gc_colb = _spread(gcum, sel_c_ref[...])
    ge8 = jnp.broadcast_to(gc_x[c - 1:c, :], (SUBLANES, SSM_INNER))

    xr = x_ref[...]
    x = _silu(_causal_conv_chunk(xr, tx_ref[...], cw_ref[:, :, :off_b], cb_ref[:, :off_b]))
    tx_ref[...] = xr[c - SUBLANES:c]

    xdt = x * dt_x
    xdec = xdt * jnp.exp(_add8(-gc_x, ge8))
    att =[scores[h // SSM_HPG] * _decay_matrix(gc_colb[:, h * c:(h + 1) * c], gcum_t[h:h + 1, :])
           for h in range(SSM_HEADS)]
    y_intra = jnp.concatenate([_dot(a, xdt[:, h * hd:(h + 1) * hd]) for h, a in enumerate(att)], axis=1)
    for g in groups:
        cols = slice(g * gw, (g + 1) * gw)
        s_ref[g] = _mul8(s_old[g], jnp.exp(ge8[:, cols])) + _dot(bg[g], xdec[:, cols], _TN)
    y = (y_intra + y_inter * jnp.exp(gc_x) + _mul8(x, dskip_ref[...])) * _silu(z_ref[...])
    for g in groups:
        cols = slice(g * gw, (g + 1) * gw)
        o_ref[:, cols] = _rms_rows(y[:, cols], nw_ref[:, cols]).astype(o_ref.dtype)

    @pl.when(c_idx == pl.num_programs(1) - 1)
    def _():
        for g in range(SSM_GROUPS):
            s_out_ref[g * SSM_HPG:(g + 1) * SSM_HPG] = _transpose(s_ref[g]).reshape(SSM_HPG, hd, ns)


def ssd_prompt(proj, conv_w, conv_b, dtb_row, alog_row, dskip_row, norm_w, chunk=128):
    b, l, _ = proj.shape
    assert chunk == LANES
    nc = l // chunk
    col = lambda width, off: pl.BlockSpec((None, chunk, width), lambda i, j: (i, j, off // width))
    row = _resident((1, LANES))
    lane = jnp.arange(LANES, dtype=jnp.int32)[:, None]
    sel_hd = (lane == jnp.arange(SSM_INNER, dtype=jnp.int32)[None, :] // SSM_HEAD_DIM).astype(BF16)
    sel_c = (lane == jnp.arange(SSM_HEADS * chunk, dtype=jnp.int32)[None, :] // chunk).astype(BF16)
    dskip8 = _rep8(jnp.repeat(dskip_row[0, :SSM_HEADS], SSM_HEAD_DIM))
    return pl.pallas_call(
        _ssd_prompt_kernel,
        grid=(b, nc),
        in_specs=[col(SSM_INNER, SSM_OFF_Z), col(SSM_INNER, SSM_OFF_X), col(SSM_BC_DIM, SSM_OFF_B),
                  col(SSM_BC_DIM, SSM_OFF_C), col(LANES, SSM_OFF_DT),
                  _resident((CONV_WIDTH, SUBLANES, SSM_CONV_DIM)), _resident((SUBLANES, SSM_CONV_DIM)),
                  row, row, _resident((SUBLANES, SSM_INNER)), _resident((SUBLANES, SSM_INNER)),
                  _resident(sel_hd.shape), _resident(sel_c.shape)],
        out_specs=[pl.BlockSpec((None, chunk, SSM_INNER), lambda i, j: (i, j, 0)),
                   pl.BlockSpec((None, SSM_HEADS, SSM_HEAD_DIM, SSM_STATE), lambda i, j: (i, 0, 0, 0))],
        out_shape=[jax.ShapeDtypeStruct((b, l, SSM_INNER), BF16),
                   jax.ShapeDtypeStruct((b, SSM_HEADS, SSM_HEAD_DIM, SSM_STATE), F32)],
        scratch_shapes=[pltpu.VMEM((SSM_GROUPS, SSM_STATE, SSM_GROUP_COLS), F32),
                        pltpu.VMEM((SUBLANES, SSM_INNER), F32),
                        pltpu.VMEM((SUBLANES, SSM_BC_DIM), F32), pltpu.VMEM((SUBLANES, SSM_BC_DIM), F32)],
        compiler_params=_cparams(("parallel", "arbitrary")),
        name="ssd_prompt",
    )(proj, proj, proj, proj, proj, _rep8_rows(conv_w), _rep8(conv_b), dtb_row, alog_row, dskip8,
      _rep8(norm_w), sel_hd, sel_c)


def _stacked_out(acc, n_inputs, out_index):
    return [pl.BlockSpec(memory_space=pl.ANY)], [acc], {n_inputs: out_index}


def _conv_step(x, buf_ref, w8_ref, bias8=None):
    acc = x * w8_ref[CONV_WIDTH - 1]
    for i in range(CONV_WIDTH - 1):
        acc = acc + buf_ref[i] * w8_ref[i]
    if bias8 is not None:
        acc = acc + bias8
    return acc


def _conv_state_kernel(buf_ref, x_ref, *rest):
    out_ref = rest[-1]
    for i in range(CONV_WIDTH - 2):
        out_ref[i] = buf_ref[i + 1]
    out_ref[CONV_WIDTH - 2] = x_ref[...]


def conv_state_update(conv_t, proj, col_off, layer, acc, wc):
    _, nbuf, b, n = conv_t.shape
    spec = pl.BlockSpec((None, nbuf, b, wc), lambda j: (layer, 0, 0, j))
    acc_specs, acc_args, aliases = _stacked_out(acc, 2, 0)
    return pl.pallas_call(
        _conv_state_kernel,
        grid=(n // wc,),
        in_specs=[spec, pl.BlockSpec((b, wc), lambda j: (0, col_off // wc + j))] + acc_specs,
        out_specs=spec,
        out_shape=jax.ShapeDtypeStruct(conv_t.shape, F32),
        input_output_aliases=aliases,
        compiler_params=_cparams(("parallel",)),
        name="conv_state",
    )(conv_t, proj, *acc_args)


def _gdn_decode_kernel(q_ref, k_ref, v_ref, z_ref, ba_ref, bq_ref, bk_ref, bv_ref, s_ref,
                       cwq_ref, cwk_ref, cwv_ref, alog_ref, dtb_ref, nw_ref, *rest):
    o_ref, s_out_ref = rest[-2:]
    h = pl.program_id(1)
    xq, xk, xv = q_ref[...], k_ref[...], v_ref[...]
    q = _silu(_conv_step(xq, bq_ref, cwq_ref))
    k = _silu(_conv_step(xk, bk_ref, cwk_ref))
    v = _silu(_conv_step(xv, bv_ref, cwv_ref))
    hd = GDN_DIM
    q = q * lax.rsqrt(jnp.sum(q * q, axis=-1, keepdims=True) + NORM_EPS) * (hd ** -0.5)
    k = k * lax.rsqrt(jnp.sum(k * k, axis=-1, keepdims=True) + NORM_EPS)
    ba = ba_ref[...]
    lane = _iota((1, LANES), 1)
    beta = jnp.sum(jnp.where(lane == h, _sigmoid(ba), 0.0), axis=1, keepdims=True)
    g = -jnp.exp(alog_ref[...]) * _softplus(ba + dtb_ref[...])
    eg = jnp.exp(jnp.sum(jnp.where(lane == GDN_HEADS + h, g, 0.0), axis=1, keepdims=True))
    qk = jnp.sum(q * k, axis=-1, keepdims=True)
    k_t = _transpose(k)
    q_t = _transpose(q)
    tb = xq.shape[0]
    ys = []
    for b in range(tb):
        s = s_ref[b, 0]
        k_col = k_t[:, b:b + 1]
        q_col = q_t[:, b:b + 1]
        ks = jnp.sum(k_col * s, axis=0, keepdims=True)
        egb = eg[b:b + 1, :]
        delta = beta[b:b + 1, :] * (v[b:b + 1, :] - egb * ks)
        s_new = egb * s + k_col * delta
        s_out_ref[b, 0] = s_new
        ys.append(jnp.sum(q_col * s_new, axis=0, keepdims=True))
    y = jnp.concatenate(ys, axis=0)
    y = _rms_rows(y, nw_ref[...]) * _silu(z_ref[...])
    o_ref[...] = y.astype(o_ref.dtype)


def gdn_decode(proj, state, conv_t, layer, acc, conv_w, alog_row, dtb_row, norm_w, tb=8):
    b = proj.shape[0]
    hd = GDN_DIM
    nh = GDN_HEADS
    pcol = lambda off: pl.BlockSpec((tb, hd), lambda i, h: (i, off // hd + h))
    ccol = lambda part: pl.BlockSpec((None, CONV_WIDTH - 1, tb, hd), lambda i, h: (layer, 0, i, part * nh + h))
    wcol = lambda part: pl.BlockSpec((CONV_WIDTH, SUBLANES, hd), lambda i, h: (0, 0, part * nh + h))
    row = _resident((1, LANES))
    sspec = pl.BlockSpec((None, tb, 1, hd, hd), lambda i, h: (layer, i, h, 0, 0))
    acc_specs, acc_args, aliases = _stacked_out(acc, 15, 1)
    assert tb == SUBLANES
    conv_w8 = _rep8_rows(conv_w)
    return pl.pallas_call(
        _gdn_decode_kernel,
        grid=(b // tb, nh),
        in_specs=[pcol(HYB_OFF_QKV), pcol(HYB_OFF_QKV + GDN_OUT_DIM), pcol(HYB_OFF_QKV + 2 * GDN_OUT_DIM),
                  pcol(HYB_OFF_Z), pl.BlockSpec((tb, LANES), lambda i, h: (i, HYB_OFF_BA // LANES)),
                  ccol(0), ccol(1), ccol(2), sspec,
                  wcol(0), wcol(1), wcol(2), row, row, _resident((SUBLANES, hd))] + acc_specs,
        out_specs=[pl.BlockSpec((tb, hd), lambda i, h: (i, h)), sspec],
        out_shape=[jax.ShapeDtypeStruct((b, GDN_OUT_DIM), BF16), jax.ShapeDtypeStruct(state.shape, F32)],
        input_output_aliases=aliases,
        compiler_params=_cparams(("parallel", "parallel")),
        name="gdn_decode",
    )(proj, proj, proj, proj, proj, conv_t, conv_t, conv_t, state,
      conv_w8, conv_w8, conv_w8, alog_row, dtb_row, _rep8(norm_w), *acc_args)


def _ret_decode_kernel(q_ref, k_ref, v_ref, gate_ref, s_ref, cos_ref, sin_ref, nw_ref, *rest):
    o_ref, s_out_ref = rest[-2:]
    q = _rotary(q_ref[...], cos_ref[...], sin_ref[...])
    k = _rotary(k_ref[...], cos_ref[...], sin_ref[...]) * (RET_KEY_DIM ** -0.5)
    q_t = _transpose(q)
    k_t = _transpose(k)
    tb = q.shape[0]
    kd, vd = RET_KEY_DIM, RET_VAL_DIM
    rows = []
    for b in range(tb):
        ys = []
        for h in range(RET_HEADS):
            gamma = math.exp(_ret_log_gamma(h))
            s = s_ref[b, h]
            s_new = s * gamma + k_t[h * kd:(h + 1) * kd, b:b + 1] * v_ref[b:b + 1, h * vd:(h + 1) * vd]
            s_out_ref[b, h] = s_new
            ys.append(jnp.sum(q_t[h * kd:(h + 1) * kd, b:b + 1] * s_new, axis=0, keepdims=True))
        rows.append(jnp.concatenate(ys, axis=1))
    y = jnp.concatenate(rows, axis=0)
    for h in range(RET_HEADS):
        yh = (_mul8(_group_norm_rows(y[:, h * vd:(h + 1) * vd]), nw_ref[:, h * vd:(h + 1) * vd])
              * _silu(gate_ref[:, h * vd:(h + 1) * vd]))
        o_ref[:, h * vd:(h + 1) * vd] = yh.astype(o_ref.dtype)


def ret_decode(proj, state, layer, acc, cos_f, sin_s, norm_w, tb=8):
    b = proj.shape[0]
    pcol = lambda width, off: pl.BlockSpec((tb, width), lambda i: (i, off // width))
    sspec = pl.BlockSpec((None, tb, RET_HEADS, RET_KEY_DIM, RET_VAL_DIM), lambda i: (layer, i, 0, 0, 0))
    acc_specs, acc_args, aliases = _stacked_out(acc, 8, 1)
    return pl.pallas_call(
        _ret_decode_kernel,
        grid=(b // tb,),
        in_specs=[pcol(RET_QK_DIM, HYB_OFF_RQ), pcol(RET_QK_DIM, HYB_OFF_RK), pcol(RET_OUT_DIM, HYB_OFF_RV),
                  pcol(RET_OUT_DIM, HYB_OFF_RG), sspec, _resident((1, RET_QK_DIM)), _resident((1, RET_QK_DIM)),
                  _resident((SUBLANES, RET_OUT_DIM))] + acc_specs,
        out_specs=[pl.BlockSpec((tb, RET_OUT_DIM), lambda i: (i, 0)), sspec],
        out_shape=[jax.ShapeDtypeStruct((b, RET_OUT_DIM), BF16), jax.ShapeDtypeStruct(state.shape, F32)],
        input_output_aliases=aliases,
        compiler_params=_cparams(("parallel",)),
        name="ret_decode",
    )(proj, proj, proj, proj, state, cos_f, sin_s, _rep8(norm_w), *acc_args)


def _ssd_decode_kernel(z_ref, x_ref, b_ref, c_ref, dt_ref, bx_ref, bb_ref, bc_ref, s_ref,
                       cwx_ref, cwb_ref, cwc_ref, cbx_ref, cbb_ref, cbc_ref,
                       dtb_ref, alog_ref, dskip_ref, nw_ref, *rest):
    o_ref, s_out_ref = rest[-2:]
    grp = pl.program_id(1)
    xr, br, cr = x_ref[...], b_ref[...], c_ref[...]
    x = _silu(_conv_step(xr, bx_ref, cwx_ref, cbx_ref[...]))
    bm = _silu(_conv_step(br, bb_ref, cwb_ref, cbb_ref[...]))
    cm = _silu(_conv_step(cr, bc_ref, cwc_ref, cbc_ref[...]))
    tb = xr.shape[0]
    hd, gw = SSM_HEAD_DIM, SSM_GROUP_COLS
    dt = _softplus(dt_ref[...] + dtb_ref[...])
    spread = (_iota((LANES, gw), 0) == grp * SSM_HPG + _iota((LANES, gw), 1) // hd).astype(BF16)
    per_head = jnp.concatenate([dt, -jnp.exp(alog_ref[...]), dskip_ref[...],
                                jnp.zeros((2 * SUBLANES - tb - 2, LANES), F32)], axis=0)
    ph_hi, ph_lo = _split(per_head)
    per_col = (jnp.dot(ph_hi, spread, preferred_element_type=F32)
               + jnp.dot(ph_lo, spread, preferred_element_type=F32))
    dt_x, a_x, dskip_x = per_col[0:tb], per_col[tb:tb + 1], per_col[tb + 1:tb + 2]
    eg_x = jnp.exp(dt_x * a_x)
    xdt_t = _split((x * dt_x).T)
    row_id = _iota(bm.shape, 0)
    outer = [_dot3(xdt_t, _split(jnp.where(row_id == b, bm, 0.0))) for b in range(tb)]
    for b in range(tb):
        for hl in range(SSM_HPG):
            s_out_ref[b, hl] = (s_ref[b, hl] * eg_x[b:b + 1, hl * hd:hl * hd + 1]
                                + outer[b][hl * hd:(hl + 1) * hd, :])
    rows = [_dot(cm[b:b + 1, :], s_out_ref[b].reshape(gw, SSM_STATE), _NT) for b in range(tb)]
    y = (jnp.concatenate(rows, axis=0) + dskip_x * x) * _silu(z_ref[...])
    o_ref[...] = _rms_rows(y, nw_ref[...]).astype(o_ref.dtype)


def ssd_decode(proj, state_t, conv_t, layer, acc, conv_w, conv_b, dtb_row, alog_row, dskip_row, norm_w, tb=8):
    b = proj.shape[0]
    gw = SSM_GROUP_COLS
    ns = SSM_STATE
    assert tb == SUBLANES
    cw8, cb8, nw8 = _rep8_rows(conv_w), _rep8(conv_b), _rep8(norm_w)
    pcol = lambda width, off: pl.BlockSpec((tb, width), lambda i, g: (i, off // width + g))
    ccol = lambda width, off: pl.BlockSpec((None, CONV_WIDTH - 1, tb, width),
                                           lambda i, g: (layer, 0, i, off // width + g))
    wcol = lambda width, off: pl.BlockSpec((CONV_WIDTH, SUBLANES, width), lambda i, g: (0, 0, off // width + g))
    rcol = lambda width, off: pl.BlockSpec((SUBLANES, width), lambda i, g: (0, off // width + g))
    row = _resident((1, LANES))
    sspec = pl.BlockSpec((None, tb, SSM_HPG, SSM_HEAD_DIM, ns), lambda i, g: (layer, i, g, 0, 0))
    off_b, off_c = SSM_INNER, SSM_INNER + SSM_BC_DIM
    acc_specs, acc_args, aliases = _stacked_out(acc, 19, 1)
    return pl.pallas_call(
        _ssd_decode_kernel,
        grid=(b // tb, SSM_GROUPS),
        in_specs=[pcol(gw, SSM_OFF_Z), pcol(gw, SSM_OFF_X), pcol(ns, SSM_OFF_B), pcol(ns, SSM_OFF_C),
                  pl.BlockSpec((tb, LANES), lambda i, g: (i, SSM_OFF_DT // LANES)),
                  ccol(gw, 0), ccol(ns, off_b), ccol(ns, off_c), sspec,
                  wcol(gw, 0), wcol(ns, off_b), wcol(ns, off_c),
                  rcol(gw, 0), rcol(ns, off_b), rcol(ns, off_c),
                  row, row, row, rcol(gw, 0)] + acc_specs,
        out_specs=[pl.BlockSpec((tb, gw), lambda i, g: (i, g)), sspec],
        out_shape=[jax.ShapeDtypeStruct((b, SSM_INNER), BF16), jax.ShapeDtypeStruct(state_t.shape, F32)],
        input_output_aliases=aliases,
        compiler_params=_cparams(("parallel", "parallel")),
        name="ssd_decode",
    )(proj, proj, proj, proj, proj, conv_t, conv_t, conv_t, state_t,
      cw8, cw8, cw8, cb8, cb8, cb8, dtb_row, alog_row, dskip_row, nw8, *acc_args)


def _lane_row(vals, offset):
    return jnp.zeros((1, LANES), F32).at[0, offset:offset + vals.shape[0]].set(vals.astype(F32))


def _prep_hyb_w_in(w):
    sizes = (GDN_QKV_DIM, GDN_OUT_DIM, GDN_HEADS, GDN_HEADS, RET_QK_DIM, RET_QK_DIM, RET_OUT_DIM, RET_OUT_DIM)
    offs = [0]
    for s in sizes:
        offs.append(offs[-1] + s)
    part = lambda i: w[:, offs[i]:offs[i + 1]]
    cols = [part(0), part(1), part(4), part(5), part(6), part(7), part(2), part(3)]
    out = jnp.concatenate(cols, axis=1)
    return jnp.pad(out, ((0, 0), (0, HYB_N - out.shape[1]))).astype(BF16)


def _prep_ssm_w_in(w):
    return jnp.pad(w, ((0, 0), (0, SSM_N - w.shape[1]))).astype(BF16)


def _rope_tables(pos):
    half = RET_KEY_DIM // 2
    inv_freq = ROPE_BASE ** (-jnp.arange(half, dtype=F32) / half)
    ang = pos.astype(F32)[:, None] * inv_freq[None, :]
    cos, sin = jnp.cos(ang), jnp.sin(ang)
    cos_f = jnp.tile(jnp.concatenate([cos, cos], axis=1), (1, RET_HEADS))
    sin_s = jnp.tile(jnp.concatenate([-sin, sin], axis=1), (1, RET_HEADS))
    return cos_f, sin_s


def _trunk(x, pos, states, params, prompt):
    bsz, l, d = x.shape
    m = bsz * l
    tm = 512 if m % 512 == 0 else m
    x2 = x.reshape(m, d)
    cos_f, sin_s = _rope_tables(pos)
    new_gdn, new_gdn_conv, new_ret, new_ssm, new_ssm_conv = [], [], [], [], []
    if not prompt:
        s_gdn, c_gdn, s_ret, s_ssm, c_ssm = states
        c_gdn_t, c_ssm_t = jnp.swapaxes(c_gdn, 1, 2), jnp.swapaxes(c_ssm, 1, 2)
        s_ssm_t = jnp.swapaxes(s_ssm, 3, 4)
        acc_gdn, acc_gdn_conv, acc_ret, acc_ssm, acc_ssm_conv = (
            jnp.zeros_like(a) for a in (s_gdn, c_gdn_t, s_ret, s_ssm_t, c_ssm_t))
    for layer in range(DEPTH):
        i = layer // 2
        p = params
        if layer % 2 == 0:
            proj = norm_matmul(x2, p["norm_mix"][layer], p["w_in_hyb"][i], tm)
            alog_row = _lane_row(p["gdn_a_log"][i], GDN_HEADS)
            dtb_row = _lane_row(p["gdn_dt_bias"][i], GDN_HEADS)
            if prompt:
                proj3 = proj.reshape(bsz, l, HYB_N)
                o_a, sg = gdn_prompt(proj3, p["gdn_conv_w"][i], alog_row, dtb_row, p["gdn_norm_w"][i])
                o_b, sr = ret_prompt(proj3, cos_f, sin_s, p["ret_norm_w"][i])
                cg = proj3[:, l - (CONV_WIDTH - 1):, :GDN_QKV_DIM]
                o_a, o_b = o_a.reshape(m, GDN_OUT_DIM), o_b.reshape(m, RET_OUT_DIM)
                new_gdn.append(sg)
                new_gdn_conv.append(cg)
                new_ret.append(sr)
            else:
                o_a, acc_gdn = gdn_decode(proj, s_gdn, c_gdn_t, i, acc_gdn, p["gdn_conv_w"][i], alog_row,
                                          dtb_row, p["gdn_norm_w"][i])
                acc_gdn_conv = conv_state_update(c_gdn_t, proj, HYB_OFF_QKV, i, acc_gdn_conv, GDN_OUT_DIM)
                o_b, acc_ret = ret_decode(proj, s_ret, i, acc_ret, cos_f, sin_s, p["ret_norm_w"][i])
            w_out = p["w_out_hyb"][i]
            acts, ws = [o_a, o_b], [w_out[:GDN_OUT_DIM], w_out[GDN_OUT_DIM:]]
        else:
            proj = norm_matmul(x2, p["norm_mix"][layer], p["w_in_ssm"][i], tm)
            dtb_row = _lane_row(p["ssm_dt_bias"][i], 0)
            alog_row = _lane_row(p["ssm_a_log"][i], 0)
            dskip_row = _lane_row(p["ssm_d"][i], 0)
            if prompt:
                proj3 = proj.reshape(bsz, l, SSM_N)
                y, ss = ssd_prompt(proj3, p["ssm_conv_w"][i], p["ssm_conv_b"][i], dtb_row, alog_row,
                                   dskip_row, p["ssm_norm_w"][i])
                cs = proj3[:, l - (CONV_WIDTH - 1):, SSM_OFF_X:SSM_OFF_X + SSM_CONV_DIM]
                y = y.reshape(m, SSM_INNER)
                new_ssm.append(ss)
                new_ssm_conv.append(cs)
            else:
                y, acc_ssm = ssd_decode(proj, s_ssm_t, c_ssm_t, i, acc_ssm, p["ssm_conv_w"][i],
                                        p["ssm_conv_b"][i], dtb_row, alog_row, dskip_row, p["ssm_norm_w"][i])
                acc_ssm_conv = conv_state_update(c_ssm_t, proj, SSM_OFF_X, i, acc_ssm_conv, D_MODEL)
            acts, ws = [y], [p["w_out_ssm"][i]]
        x2 = post_mixer(acts, ws, x2, p["norm_mlp"][layer], p["mlp_w1"][layer], p["mlp_w2"][layer],
                        p["norm_final"], layer == DEPTH - 1, tm)
    y_out = x2.reshape(bsz, l, d)
    if prompt:
        return (y_out, jnp.stack(new_gdn), jnp.stack(new_gdn_conv), jnp.stack(new_ret),
                jnp.swapaxes(jnp.stack(new_ssm), 3, 4), jnp.stack(new_ssm_conv))
    return (y_out, acc_gdn, jnp.swapaxes(acc_gdn_conv, 1, 2), acc_ret,
            jnp.swapaxes(acc_ssm, 3, 4), jnp.swapaxes(acc_ssm_conv, 1, 2))


def kernel(x_prompt, x_sample, state_gdn, state_gdn_conv, state_ret, state_ssm, state_ssm_conv, norm_mix, norm_mlp, norm_final, w_in_hyb, gdn_conv_w, gdn_a_log, gdn_dt_bias, gdn_norm_w, ret_norm_w, w_out_hyb, w_in_ssm, ssm_conv_w, ssm_conv_b, ssm_dt_bias, ssm_a_log, ssm_d, ssm_norm_w, w_out_ssm, mlp_w1, mlp_w2):
    n_hyb, n_ssm = w_in_hyb.shape[0], w_in_ssm.shape[0]
    params = dict(
        norm_mix=norm_mix, norm_mlp=norm_mlp, norm_final=norm_final,
        w_in_hyb=[_prep_hyb_w_in(w_in_hyb[i]) for i in range(n_hyb)],
        gdn_conv_w=gdn_conv_w, gdn_a_log=gdn_a_log, gdn_dt_bias=gdn_dt_bias, gdn_norm_w=gdn_norm_w,
        ret_norm_w=ret_norm_w, w_out_hyb=w_out_hyb.astype(BF16),
        w_in_ssm=[_prep_ssm_w_in(w_in_ssm[i]) for i in range(n_ssm)],
        ssm_conv_w=ssm_conv_w, ssm_conv_b=ssm_conv_b, ssm_dt_bias=ssm_dt_bias, ssm_a_log=ssm_a_log,
        ssm_d=ssm_d, ssm_norm_w=ssm_norm_w, w_out_ssm=w_out_ssm.astype(BF16),
        mlp_w1=mlp_w1.astype(BF16), mlp_w2=mlp_w2.astype(BF16))
    lp, ls = x_prompt.shape[1], x_sample.shape[1]
    pos_prompt = jnp.arange(lp, dtype=jnp.int32)
    pos_sample = PAST_LEN + jnp.arange(ls, dtype=jnp.int32)
    y_p, p_gdn, p_gdn_conv, p_ret, p_ssm, p_ssm_conv = _trunk(
        x_prompt, pos_prompt, (None,) * 5, params, prompt=True)
    y_s, s_gdn, s_gdn_conv, s_ret, s_ssm, s_ssm_conv = _trunk(
        x_sample, pos_sample, (state_gdn, state_gdn_conv, state_ret, state_ssm, state_ssm_conv),
        params, prompt=False)
    return (y_p, y_s, p_gdn, p_gdn_conv, p_ret, p_ssm, p_ssm_conv,
            s_gdn, s_gdn_conv, s_ret, s_ssm, s_ssm_conv)
```

```python
import functools
import math

import jax
import jax.numpy as jnp
from jax import lax
from jax.experimental import pallas as pl
from jax.experimental.pallas import tpu as pltpu

F32 = jnp.float32
BF16 = jnp.bfloat16

D_MODEL = 1024
DEPTH = 4
CONV_WIDTH = 4
NORM_EPS = 1e-6
PAST_LEN = 16384

GDN_HEADS = 4
GDN_DIM = 128
GDN_QKV_DIM = 3 * GDN_HEADS * GDN_DIM
GDN_OUT_DIM = GDN_HEADS * GDN_DIM

RET_HEADS = 4
RET_KEY_DIM = 64
RET_VAL_DIM = 128
RET_QK_DIM = RET_HEADS * RET_KEY_DIM
RET_OUT_DIM = RET_HEADS * RET_VAL_DIM
ROPE_BASE = 10000.0

SSM_INNER = 2 * D_MODEL
SSM_HEAD_DIM = 64
SSM_HEADS = SSM_INNER // SSM_HEAD_DIM
SSM_GROUPS = 4
SSM_HPG = SSM_HEADS // SSM_GROUPS
SSM_STATE = 128
SSM_GROUP_COLS = SSM_HPG * SSM_HEAD_DIM
SSM_BC_DIM = SSM_GROUPS * SSM_STATE
SSM_CONV_DIM = SSM_INNER + 2 * SSM_BC_DIM
MLP_HIDDEN = 4 * D_MODEL

LANES = 128
SUBLANES = 8

HYB_OFF_QKV = 0
HYB_OFF_Z = GDN_QKV_DIM
HYB_OFF_RQ = HYB_OFF_Z + GDN_OUT_DIM
HYB_OFF_RK = HYB_OFF_RQ + RET_QK_DIM
HYB_OFF_RV = HYB_OFF_RK + RET_QK_DIM
HYB_OFF_RG = HYB_OFF_RV + RET_OUT_DIM
HYB_OFF_BA = HYB_OFF_RG + RET_OUT_DIM
HYB_N = 3840
SSM_OFF_Z = 0
SSM_OFF_X = SSM_INNER
SSM_OFF_B = SSM_OFF_X + SSM_INNER
SSM_OFF_C = SSM_OFF_B + SSM_BC_DIM
SSM_OFF_DT = SSM_OFF_C + SSM_BC_DIM
SSM_N = 5376
PROJ_TN = 768

VMEM_LIMIT = 56 * 1024 * 1024

_NN = (((1,), (0,)), ((), ()))
_NT = (((1,), (1,)), ((), ()))
_TN = (((0,), (0,)), ((), ()))


def _dot(a, b, dims=_NN):
    return lax.dot_general(a.astype(BF16), b.astype(BF16), dims, preferred_element_type=F32)


def _dot_f32(a, b, dims=_NN):
    return lax.dot_general(a, b, dims, precision=lax.Precision.HIGHEST, preferred_element_type=F32)


def _sigmoid(x):
    return 0.5 + 0.5 * jnp.tanh(0.5 * x)


def _silu(x):
    t = 0.5 * x
    return t + t * jnp.tanh(t)


def _softplus(x):
    return jnp.maximum(x, 0.0) + jnp.log(1.0 + jnp.exp(-jnp.abs(x)))


def _iota(shape, dim):
    return lax.broadcasted_iota(jnp.int32, shape, dim)


def _eye(n):
    return (_iota((n, n), 0) == _iota((n, n), 1)).astype(F32)


def _transpose(x):
    return x.T


def _cparams(sem):
    return pltpu.CompilerParams(dimension_semantics=sem, vmem_limit_bytes=VMEM_LIMIT)


def _resident(shape):
    nd = len(shape)
    return pl.BlockSpec(shape, lambda *_: (0,) * nd, pipeline_mode=pl.Buffered(1))


def _rep8(v):
    v = v.reshape(1, -1)
    return jnp.broadcast_to(v, (SUBLANES, v.shape[1]))


def _rep8_rows(w):
    return jnp.broadcast_to(w[:, None, :], (w.shape[0], SUBLANES, w.shape[1]))


def _mul8(x, r8):
    r, n = x.shape
    return (x.reshape(r // SUBLANES, SUBLANES, n) * r8).reshape(r, n)


def _add8(x, r8):
    r, n = x.shape
    return (x.reshape(r // SUBLANES, SUBLANES, n) + r8).reshape(r, n)


def _rms_rows(x, gain8):
    return _mul8(x * lax.rsqrt(jnp.mean(x * x, axis=-1, keepdims=True) + NORM_EPS), gain8)


def _norm_matmul_kernel(x_ref, g_ref, w_ref, o_ref, *, tn):
    xn = _rms_rows(x_ref[...], g_ref[...]).astype(BF16)
    for j in range(w_ref.shape[1] // tn):
        o_ref[:, j * tn:(j + 1) * tn] = jnp.dot(xn, w_ref[:, j * tn:(j + 1) * tn],
                                                preferred_element_type=F32)


def norm_matmul(x, gain, w, tm):
    m, d = x.shape
    n = w.shape[1]
    return pl.pallas_call(
        functools.partial(_norm_matmul_kernel, tn=PROJ_TN),
        grid=(m // tm,),
        in_specs=[pl.BlockSpec((tm, d), lambda i: (i, 0)), _resident((SUBLANES, d)), _resident((d, n))],
        out_specs=pl.BlockSpec((tm, n), lambda i: (i, 0)),
        out_shape=jax.ShapeDtypeStruct((m, n), F32),
        compiler_params=_cparams(("parallel",)),
        name="norm_matmul",
    )(x, _rep8(gain), w)


def _post_mixer_kernel(*refs, n_in, tf, final_norm):
    a_refs, w_refs = refs[:n_in], refs[n_in:2 * n_in]
    r_ref, g_ref, w1_ref, w2_ref, gf_ref, o_ref = refs[2 * n_in:]
    x = r_ref[...]
    for a_ref, w_ref in zip(a_refs, w_refs):
        x = x + jnp.dot(a_ref[...], w_ref[...], preferred_element_type=F32)
    o_ref[...] = x
    x = o_ref[...]
    xn = _rms_rows(x, g_ref[...]).astype(BF16)
    acc = x
    for j in range(w1_ref.shape[1] // tf):
        h = jnp.dot(xn, w1_ref[:, j * tf:(j + 1) * tf], preferred_element_type=F32)
        h = jnp.maximum(h, 0.0)
        acc = acc + jnp.dot((h * h).astype(BF16), w2_ref[j * tf:(j + 1) * tf, :],
                            preferred_element_type=F32)
    if final_norm:
        acc = _rms_rows(acc, gf_ref[...])
    o_ref[...] = acc


def post_mixer(acts, ws, res, gain, w1, w2, final_gain, final_norm, tm):
    m, d = res.shape
    f = w1.shape[1]
    n_in = len(acts)
    in_specs = ([pl.BlockSpec((tm, a.shape[1]), lambda i: (i, 0)) for a in acts]
                + [_resident(w.shape) for w in ws]
                + [pl.BlockSpec((tm, d), lambda i: (i, 0)), _resident((SUBLANES, d)), _resident((d, f)),
                   _resident((f, d)), _resident((SUBLANES, d))])
    return pl.pallas_call(
        functools.partial(_post_mixer_kernel, n_in=n_in, tf=1024, final_norm=final_norm),
        grid=(m // tm,),
        in_specs=in_specs,
        out_specs=pl.BlockSpec((tm, d), lambda i: (i, 0)),
        out_shape=jax.ShapeDtypeStruct((m, d), F32),
        compiler_params=_cparams(("parallel",)),
        name="post_mixer",
    )(*acts, *ws, res, _rep8(gain), w1, w2, _rep8(final_gain))


def _causal_conv_chunk(x, tail, w8, bias8=None):
    c, n = x.shape
    g = c // SUBLANES
    x3 = x.reshape(g, SUBLANES, n)
    acc = x3 * w8[CONV_WIDTH - 1]
    row = _iota((SUBLANES, n), 0)
    for k in range(1, CONV_WIDTH):
        r = pltpu.roll(x3, k, 1)
        prev = jnp.concatenate([pltpu.roll(tail, k, 0)[None], r[:g - 1]], axis=0)
        acc = acc + jnp.where(row < k, prev, r) * w8[CONV_WIDTH - 1 - k]
    if bias8 is not None:
        acc = acc + bias8
    return acc.reshape(c, n)


def _cumsum_rows(g):
    c = g.shape[0]
    tri = (_iota((c, c), 0) >= _iota((c, c), 1)).astype(F32)
    return _dot_f32(tri, g)


def _decay_matrix(gc_col, gc_row):
    c = gc_col.shape[0]
    causal = _iota((c, c), 0) >= _iota((c, c), 1)
    return jnp.where(causal, jnp.exp(gc_col - gc_row), 0.0)


def _split(x):
    hi = x.astype(BF16)
    return hi, (x - hi.astype(F32)).astype(BF16)


def _dot3(a, b, dims=_NN):
    (ah, al), (bh, bl) = a, b
    d = lambda p, q: lax.dot_general(p, q, dims, preferred_element_type=F32)
    return d(ah, bh) + (d(ah, bl) + d(al, bh))


def _unit_lower_inverses(n_list):
    c = n_list[0].shape[0]
    ri, ci = _iota((c, c), 0), _iota((c, c), 1)
    eye = (ri == ci).astype(F32)
    blk = 16
    same = (ri // blk) == (ci // blk)
    p = [jnp.where(same, -n, 0.0).astype(BF16) for n in n_list]
    inv = [eye + x for x in p]
    for _ in range(3):
        p = [_dot(x, x).astype(BF16) for x in p]
        inv = [i + _dot(i, x) for i, x in zip(inv, p)]
    while blk < c:
        pair = ((ri // (2 * blk)) == (ci // (2 * blk))) & ((ri // blk) != (ci // blk))
        inv_b = [i.astype(BF16) for i in inv]
        t = [_dot(jnp.where(pair, n, 0.0), i) for n, i in zip(n_list, inv_b)]
        inv = [i - _dot(i_b, x) for i, i_b, x in zip(inv, inv_b, t)]
        blk *= 2
    return inv


def _gdn_prompt_kernel(qkv_ref, z_ref, ba_ref, cw_ref, alog_ref, dtb_ref, nw_ref,
                       o_ref, s_out_ref, s_ref, tail_ref):
    c_idx = pl.program_id(1)

    @pl.when(c_idx == 0)
    def _():
        s_ref[...] = jnp.zeros_like(s_ref)
        tail_ref[...] = jnp.zeros_like(tail_ref)

    nb, c = qkv_ref.shape[0], qkv_ref.shape[1]
    hd = GDN_DIM
    ri, ci = _iota((c, c), 0), _iota((c, c), 1)
    strict = ri > ci
    chains = [(bb, h) for bb in range(nb) for h in range(GDN_HEADS)]

    conv, beta, gcum, gcum_t = [], [], [], []
    for bb in range(nb):
        x = qkv_ref[bb]
        conv.append(_silu(_causal_conv_chunk(x, tail_ref[bb], cw_ref[...])))
        tail_ref[bb] = x[c - SUBLANES:c]
        ba = ba_ref[bb]
        beta.append(_sigmoid(ba))
        gc = _cumsum_rows(-jnp.exp(alog_ref[...]) * _softplus(ba + dtb_ref[...]))
        gcum.append(gc)
        gcum_t.append(_transpose(gc))

    q, k, v, b_col, gc_col, g_end, decay = [], [], [], [], [], [], []
    for bb, h in chains:
        qh = conv[bb][:, h * hd:(h + 1) * hd]
        kh = conv[bb][:, GDN_OUT_DIM + h * hd:GDN_OUT_DIM + (h + 1) * hd]
        q.append(qh * lax.rsqrt(jnp.sum(qh * qh, axis=-1, keepdims=True) + NORM_EPS) * (hd ** -0.5))
        k.append(kh * lax.rsqrt(jnp.sum(kh * kh, axis=-1, keepdims=True) + NORM_EPS))
        v.append(conv[bb][:, 2 * GDN_OUT_DIM + h * hd:2 * GDN_OUT_DIM + (h + 1) * hd])
        b_col.append(beta[bb][:, h:h + 1])
        gc_col.append(gcum[bb][:, GDN_HEADS + h:GDN_HEADS + h + 1])
        g_end.append(gcum[bb][c - 1:c, GDN_HEADS + h:GDN_HEADS + h + 1])
        decay.append(_decay_matrix(gc_col[-1], gcum_t[bb][GDN_HEADS + h:GDN_HEADS + h + 1, :]))
    kk = [_dot(x, x, _NT) for x in k]
    qk = [_dot(x, y, _NT) * d for x, y, d in zip(q, k, decay)]
    t_inv = _unit_lower_inverses([jnp.where(strict, b * x * d, 0.0) for b, x, d in zip(b_col, kk, decay)])
    rhs = [jnp.concatenate([vv * b, kx * (b * jnp.exp(g))], axis=1)
           for vv, kx, b, g in zip(v, k, b_col, gc_col)]
    sol = [_dot(t, r) for t, r in zip(t_inv, rhs)]
    s_old = [s_ref[bb, h] for bb, h in chains]
    delta = [x[:, :hd] - _dot(x[:, hd:], s) for x, s in zip(sol, s_old)]
    y = [_dot(x * jnp.exp(g), s) + _dot(a, dl)
         for x, g, s, a, dl in zip(q, gc_col, s_old, qk, delta)]
    s_new = [s * jnp.exp(ge) + _dot(kx * jnp.exp(ge - g), dl, _TN)
             for s, ge, kx, g, dl in zip(s_old, g_end, k, gc_col, delta)]
    for (bb, h), sn, yy in zip(chains, s_new, y):
        s_ref[bb, h] = sn
        out = _rms_rows(yy, nw_ref[...]) * _silu(z_ref[bb, :, h * hd:(h + 1) * hd])
        o_ref[bb, :, h * hd:(h + 1) * hd] = out.astype(o_ref.dtype)

    @pl.when(c_idx == pl.num_programs(1) - 1)
    def _():
        s_out_ref[...] = s_ref[...]


def gdn_prompt(proj, conv_w, alog_row, dtb_row, norm_w, chunk=64, nb=4):
    b, l, _ = proj.shape
    nc = l // chunk
    col = lambda width, off: pl.BlockSpec((nb, chunk, width), lambda i, j: (i, j, off // width))
    return pl.pallas_call(
        _gdn_prompt_kernel,
        grid=(b // nb, nc),
        in_specs=[col(GDN_QKV_DIM, HYB_OFF_QKV), col(GDN_OUT_DIM, HYB_OFF_Z), col(LANES, HYB_OFF_BA),
                  _resident((CONV_WIDTH, SUBLANES, GDN_QKV_DIM)), _resident((1, LANES)), _resident((1, LANES)),
                  _resident((SUBLANES, GDN_DIM))],
        out_specs=[pl.BlockSpec((nb, chunk, GDN_OUT_DIM), lambda i, j: (i, j, 0)),
                   pl.BlockSpec((nb, GDN_HEADS, GDN_DIM, GDN_DIM), lambda i, j: (i, 0, 0, 0))],
        out_shape=[jax.ShapeDtypeStruct((b, l, GDN_OUT_DIM), BF16),
                   jax.ShapeDtypeStruct((b, GDN_HEADS, GDN_DIM, GDN_DIM), F32)],
        scratch_shapes=[pltpu.VMEM((nb, GDN_HEADS, GDN_DIM, GDN_DIM), F32),
                        pltpu.VMEM((nb, SUBLANES, GDN_QKV_DIM), F32)],
        compiler_params=_cparams(("parallel", "arbitrary")),
        name="gdn_prompt",
    )(proj, proj, proj, _rep8_rows(conv_w), alog_row, dtb_row, _rep8(norm_w))


def _ret_log_gamma(h):
    return math.log(1.0 - 2.0 ** (-5.0 - h))


def _rotary(t, cos_f, sin_s):
    n = t.shape[1]
    half = RET_KEY_DIM // 2
    first = (_iota(t.shape, 1) % RET_KEY_DIM) < half
    swapped = jnp.where(first, pltpu.roll(t, n - half, 1), pltpu.roll(t, half, 1))
    return t * cos_f + swapped * sin_s


def _group_norm_rows(x):
    xc = x - jnp.mean(x, axis=-1, keepdims=True)
    return xc * lax.rsqrt(jnp.mean(xc * xc, axis=-1, keepdims=True) + NORM_EPS)


def _ret_prompt_kernel(q_ref, k_ref, v_ref, gate_ref, cos_ref, sin_ref, nw_ref,
                       o_ref, s_out_ref, s_ref):
    c_idx = pl.program_id(1)

    @pl.when(c_idx == 0)
    def _():
        s_ref[...] = jnp.zeros_like(s_ref)

    nb, c = q_ref.shape[0], q_ref.shape[1]
    kd, vd = RET_KEY_DIM, RET_VAL_DIM
    ri, ci = _iota((c, c), 0), _iota((c, c), 1)
    causal = ri >= ci
    pos_col = _iota((c, 1), 0).astype(F32)
    lgs = [_ret_log_gamma(h) for h in range(RET_HEADS)]
    decay = [jnp.where(causal, jnp.exp((ri - ci).astype(F32) * lg), 0.0) for lg in lgs]
    grow = [jnp.exp((pos_col + 1.0) * lg) for lg in lgs]
    gtail = [jnp.exp((c - 1.0 - pos_col) * lg) for lg in lgs]
    chains = [(bb, h) for bb in range(nb) for h in range(RET_HEADS)]
    q = [_rotary(q_ref[bb], cos_ref[...], sin_ref[...]) for bb in range(nb)]
    k = [_rotary(k_ref[bb], cos_ref[...], sin_ref[...]) * (kd ** -0.5) for bb in range(nb)]
    qh = [q[bb][:, h * kd:(h + 1) * kd] for bb, h in chains]
    kh = [k[bb][:, h * kd:(h + 1) * kd] for bb, h in chains]
    vh = [v_ref[bb, :, h * vd:(h + 1) * vd] for bb, h in chains]
    s_old = [s_ref[bb, h] for bb, h in chains]
    att = [_dot(a, b, _NT) * decay[h] for a, b, (_, h) in zip(qh, kh, chains)]
    y = [_dot(a, v) + _dot(qq, s) * grow[h] for a, v, qq, s, (_, h) in zip(att, vh, qh, s_old, chains)]
    s_new = [s * math.exp(c * lgs[h]) + _dot(kk * gtail[h], v, _TN)
             for s, kk, v, (_, h) in zip(s_old, kh, vh, chains)]
    for (bb, h), yy, sn in zip(chains, y, s_new):
        s_ref[bb, h] = sn
        out = (_mul8(_group_norm_rows(yy), nw_ref[:, h * vd:(h + 1) * vd])
               * _silu(gate_ref[bb, :, h * vd:(h + 1) * vd]))
        o_ref[bb, :, h * vd:(h + 1) * vd] = out.astype(o_ref.dtype)

    @pl.when(c_idx == pl.num_programs(1) - 1)
    def _():
        s_out_ref[...] = s_ref[...]


def ret_prompt(proj, cos_f, sin_s, norm_w, chunk=128, nb=2):
    b, l, _ = proj.shape
    nc = l // chunk
    col = lambda width, off: pl.BlockSpec((nb, chunk, width), lambda i, j: (i, j, off // width))
    tab = pl.BlockSpec((chunk, RET_QK_DIM), lambda i, j: (j, 0))
    return pl.pallas_call(
        _ret_prompt_kernel,
        grid=(b // nb, nc),
        in_specs=[col(RET_QK_DIM, HYB_OFF_RQ), col(RET_QK_DIM, HYB_OFF_RK), col(RET_OUT_DIM, HYB_OFF_RV),
                  col(RET_OUT_DIM, HYB_OFF_RG), tab, tab, _resident((SUBLANES, RET_OUT_DIM))],
        out_specs=[pl.BlockSpec((nb, chunk, RET_OUT_DIM), lambda i, j: (i, j, 0)),
                   pl.BlockSpec((nb, RET_HEADS, RET_KEY_DIM, RET_VAL_DIM), lambda i, j: (i, 0, 0, 0))],
        out_shape=[jax.ShapeDtypeStruct((b, l, RET_OUT_DIM), BF16),
                   jax.ShapeDtypeStruct((b, RET_HEADS, RET_KEY_DIM, RET_VAL_DIM), F32)],
        scratch_shapes=[pltpu.VMEM((nb, RET_HEADS, RET_KEY_DIM, RET_VAL_DIM), F32)],
        compiler_params=_cparams(("parallel", "arbitrary")),
        name="ret_prompt",
    )(proj, proj, proj, proj, cos_f, sin_s, _rep8(norm_w))


def _split3(x):
    a = x.astype(BF16)
    r = x - a.astype(F32)
    b = r.astype(BF16)
    return a, b, (r - b.astype(F32)).astype(BF16)


def _spread(x, sel):
    return sum(jnp.dot(p, sel, preferred_element_type=F32) for p in _split3(x))


def _ssd_prompt_kernel(z_ref, x_ref, b_ref, c_ref, dt_ref, cw_ref, cb_ref, dtb_ref, alog_ref, dskip_ref,
                       nw_ref, sel_hd_ref, sel_c_ref, o_ref, s_out_ref, s_ref, tx_ref, tb_ref, tc_ref):
    c_idx = pl.program_id(1)

    @pl.when(c_idx == 0)
    def _():
        s_ref[...] = jnp.zeros_like(s_ref)
        tx_ref[...] = jnp.zeros_like(tx_ref)
        tb_ref[...] = jnp.zeros_like(tb_ref)
        tc_ref[...] = jnp.zeros_like(tc_ref)

    c = x_ref.shape[0]
    hd, gw, ns = SSM_HEAD_DIM, SSM_GROUP_COLS, SSM_STATE
    off_b, off_c = SSM_INNER, SSM_INNER + SSM_BC_DIM
    groups = range(SSM_GROUPS)

    br, cr = b_ref[...], c_ref[...]
    bm = _silu(_causal_conv_chunk(br, tb_ref[...], cw_ref[:, :, off_b:off_c], cb_ref[:, off_b:off_c]))
    cm = _silu(_causal_conv_chunk(cr, tc_ref[...], cw_ref[:, :, off_c:], cb_ref[:, off_c:]))
    tb_ref[...] = br[c - SUBLANES:c]
    tc_ref[...] = cr[c - SUBLANES:c]
    bg = [bm[:, g * ns:(g + 1) * ns] for g in groups]
    cg = [cm[:, g * ns:(g + 1) * ns] for g in groups]
    s_old = [s_ref[g] for g in groups]
    scores = [_dot(cc, bb, _NT) for cc, bb in zip(cg, bg)]
    y_inter = jnp.concatenate([_dot(cc, s) for cc, s in zip(cg, s_old)], axis=1)

    dt = _softplus(dt_ref[...] + dtb_ref[...])
    gcum = _cumsum_rows(dt * (-jnp.exp(alog_ref[...])))
    gcum_t = _transpose(gcum)
    per_col = _spread(jnp.concatenate([dt, gcum], axis=0), sel_hd_ref[...])
    dt_x, gc_x = per_col[:c], per_col[c:]
    gc_colb = _spread(gcum, sel_c_ref[...])
    ge8 = jnp.broadcast_to(gc_x[c - 1:c, :], (SUBLANES, SSM_INNER))

    xr = x_ref[...]
    x = _silu(_causal_conv_chunk(xr, tx_ref[...], cw_ref[:, :, :off_b], cb_ref[:, :off_b]))
    tx_ref[...] = xr[c - SUBLANES:c]

    xdt = x * dt_x
    xdec = xdt * jnp.exp(_add8(-gc_x, ge8))
    att =[scores[h // SSM_HPG] * _decay_matrix(gc_colb[:, h * c:(h + 1) * c], gcum_t[h:h + 1, :])
           for h in range(SSM_HEADS)]
    y_intra = jnp.concatenate([_dot(a, xdt[:, h * hd:(h + 1) * hd]) for h, a in enumerate(att)], axis=1)
    for g in groups:
        cols = slice(g * gw, (g + 1) * gw)
        s_ref[g] = _mul8(s_old[g], jnp.exp(ge8[:, cols])) + _dot(bg[g], xdec[:, cols], _TN)
    y = (y_intra + y_inter * jnp.exp(gc_x) + _mul8(x, dskip_ref[...])) * _silu(z_ref[...])
    for g in groups:
        cols = slice(g * gw, (g + 1) * gw)
        o_ref[:, cols] = _rms_rows(y[:, cols], nw_ref[:, cols]).astype(o_ref.dtype)

    @pl.when(c_idx == pl.num_programs(1) - 1)
    def _():
        for g in range(SSM_GROUPS):
            s_out_ref[g * SSM_HPG:(g + 1) * SSM_HPG] = _transpose(s_ref[g]).reshape(SSM_HPG, hd, ns)


def ssd_prompt(proj, conv_w, conv_b, dtb_row, alog_row, dskip_row, norm_w, chunk=128):
    b, l, _ = proj.shape
    assert chunk == LANES
    nc = l // chunk
    col = lambda width, off: pl.BlockSpec((None, chunk, width), lambda i, j: (i, j, off // width))
    row = _resident((1, LANES))
    lane = jnp.arange(LANES, dtype=jnp.int32)[:, None]
    sel_hd = (lane == jnp.arange(SSM_INNER, dtype=jnp.int32)[None, :] // SSM_HEAD_DIM).astype(BF16)
    sel_c = (lane == jnp.arange(SSM_HEADS * chunk, dtype=jnp.int32)[None, :] // chunk).astype(BF16)
    dskip8 = _rep8(jnp.repeat(dskip_row[0, :SSM_HEADS], SSM_HEAD_DIM))
    return pl.pallas_call(
        _ssd_prompt_kernel,
        grid=(b, nc),
        in_specs=[col(SSM_INNER, SSM_OFF_Z), col(SSM_INNER, SSM_OFF_X), col(SSM_BC_DIM, SSM_OFF_B),
                  col(SSM_BC_DIM, SSM_OFF_C), col(LANES, SSM_OFF_DT),
                  _resident((CONV_WIDTH, SUBLANES, SSM_CONV_DIM)), _resident((SUBLANES, SSM_CONV_DIM)),
                  row, row, _resident((SUBLANES, SSM_INNER)), _resident((SUBLANES, SSM_INNER)),
                  _resident(sel_hd.shape), _resident(sel_c.shape)],
        out_specs=[pl.BlockSpec((None, chunk, SSM_INNER), lambda i, j: (i, j, 0)),
                   pl.BlockSpec((None, SSM_HEADS, SSM_HEAD_DIM, SSM_STATE), lambda i, j: (i, 0, 0, 0))],
        out_shape=[jax.ShapeDtypeStruct((b, l, SSM_INNER), BF16),
                   jax.ShapeDtypeStruct((b, SSM_HEADS, SSM_HEAD_DIM, SSM_STATE), F32)],
        scratch_shapes=[pltpu.VMEM((SSM_GROUPS, SSM_STATE, SSM_GROUP_COLS), F32),
                        pltpu.VMEM((SUBLANES, SSM_INNER), F32),
                        pltpu.VMEM((SUBLANES, SSM_BC_DIM), F32), pltpu.VMEM((SUBLANES, SSM_BC_DIM), F32)],
        compiler_params=_cparams(("parallel", "arbitrary")),
        name="ssd_prompt",
    )(proj, proj, proj, proj, proj, _rep8_rows(conv_w), _rep8(conv_b), dtb_row, alog_row, dskip8,
      _rep8(norm_w), sel_hd, sel_c)


def _stacked_out(acc, n_inputs, out_index):
    return [pl.BlockSpec(memory_space=pl.ANY)], [acc], {n_inputs: out_index}


def _conv_step(x, buf_ref, w8_ref, bias8=None):
    acc = x * w8_ref[CONV_WIDTH - 1]
    for i in range(CONV_WIDTH - 1):
        acc = acc + buf_ref[i] * w8_ref[i]
    if bias8 is not None:
        acc = acc + bias8
    return acc


def _conv_state_kernel(buf_ref, x_ref, *rest):
    out_ref = rest[-1]
    for i in range(CONV_WIDTH - 2):
        out_ref[i] = buf_ref[i + 1]
    out_ref[CONV_WIDTH - 2] = x_ref[...]


def conv_state_update(conv_t, proj, col_off, layer, acc, wc):
    _, nbuf, b, n = conv_t.shape
    spec = pl.BlockSpec((None, nbuf, b, wc), lambda j: (layer, 0, 0, j))
    acc_specs, acc_args, aliases = _stacked_out(acc, 2, 0)
    return pl.pallas_call(
        _conv_state_kernel,
        grid=(n // wc,),
        in_specs=[spec, pl.BlockSpec((b, wc), lambda j: (0, col_off // wc + j))] + acc_specs,
        out_specs=spec,
        out_shape=jax.ShapeDtypeStruct(conv_t.shape, F32),
        input_output_aliases=aliases,
        compiler_params=_cparams(("parallel",)),
        name="conv_state",
    )(conv_t, proj, *acc_args)


def _gdn_decode_kernel(q_ref, k_ref, v_ref, z_ref, ba_ref, bq_ref, bk_ref, bv_ref, s_ref,
                       cwq_ref, cwk_ref, cwv_ref, alog_ref, dtb_ref, nw_ref, *rest):
    o_ref, s_out_ref = rest[-2:]
    hd = GDN_DIM
    tb = q_ref.shape[0]
    q = _silu(_conv_step(q_ref[...], bq_ref, cwq_ref))
    k = _silu(_conv_step(k_ref[...], bk_ref, cwk_ref))
    v = _silu(_conv_step(v_ref[...], bv_ref, cwv_ref))
    ba = ba_ref[...]
    beta = _sigmoid(ba)
    eg = jnp.exp(-jnp.exp(alog_ref[...]) * _softplus(ba + dtb_ref[...]))
    qn, kn = [], []
    for h in range(GDN_HEADS):
        qh, kh = q[:, h * hd:(h + 1) * hd], k[:, h * hd:(h + 1) * hd]
        qn.append(qh * lax.rsqrt(jnp.sum(qh * qh, axis=-1, keepdims=True) + NORM_EPS) * (hd ** -0.5))
        kn.append(kh * lax.rsqrt(jnp.sum(kh * kh, axis=-1, keepdims=True) + NORM_EPS))
    q_t = _transpose(jnp.concatenate(qn, axis=1))
    k_t = _transpose(jnp.concatenate(kn, axis=1))
    chains = [(b, h) for b in range(tb) for h in range(GDN_HEADS)]
    s_old = [s_ref[b, h] for b, h in chains]
    k_col = [k_t[h * hd:(h + 1) * hd, b:b + 1] for b, h in chains]
    q_col = [q_t[h * hd:(h + 1) * hd, b:b + 1] for b, h in chains]
    egb = [eg[b:b + 1, GDN_HEADS + h:GDN_HEADS + h + 1] for b, h in chains]
    ks = [jnp.sum(kc * s, axis=0, keepdims=True) for kc, s in zip(k_col, s_old)]
    delta = [beta[b:b + 1, h:h + 1] * (v[b:b + 1, h * hd:(h + 1) * hd] - e * x)
             for (b, h), e, x in zip(chains, egb, ks)]
    s_new = [e * s + kc * dl for e, s, kc, dl in zip(egb, s_old, k_col, delta)]
    for (b, h), sn in zip(chains, s_new):
        s_out_ref[b, h] = sn
    ys = [jnp.sum(qc * sn, axis=0, keepdims=True) for qc, sn in zip(q_col, s_new)]
    for h in range(GDN_HEADS):
        y = jnp.concatenate([ys[b * GDN_HEADS + h] for b in range(tb)], axis=0)
        y = _rms_rows(y, nw_ref[...]) * _silu(z_ref[:, h * hd:(h + 1) * hd])
        o_ref[:, h * hd:(h + 1) * hd] = y.astype(o_ref.dtype)


def gdn_decode(proj, state, conv_t, layer, acc, conv_w, alog_row, dtb_row, norm_w, tb=8):
    b = proj.shape[0]
    hd, od = GDN_DIM, GDN_OUT_DIM
    pcol = lambda off: pl.BlockSpec((tb, od), lambda i: (i, off // od))
    ccol = lambda part: pl.BlockSpec((None, CONV_WIDTH - 1, tb, od), lambda i: (layer, 0, i, part))
    wcol = lambda part: pl.BlockSpec((CONV_WIDTH, SUBLANES, od), lambda i: (0, 0, part))
    row = _resident((1, LANES))
    sspec = pl.BlockSpec((None, tb, GDN_HEADS, hd, hd), lambda i: (layer, i, 0, 0, 0))
    acc_specs, acc_args, aliases = _stacked_out(acc, 15, 1)
    assert tb == SUBLANES
    conv_w8 = _rep8_rows(conv_w)
    return pl.pallas_call(
        _gdn_decode_kernel,
        grid=(b // tb,),
        in_specs=[pcol(HYB_OFF_QKV), pcol(HYB_OFF_QKV + od), pcol(HYB_OFF_QKV + 2 * od),
                  pcol(HYB_OFF_Z), pl.BlockSpec((tb, LANES), lambda i: (i, HYB_OFF_BA // LANES)),
                  ccol(0), ccol(1), ccol(2), sspec,
                  wcol(0), wcol(1), wcol(2), row, row, _resident((SUBLANES, hd))] + acc_specs,
        out_specs=[pl.BlockSpec((tb, od), lambda i: (i, 0)), sspec],
        out_shape=[jax.ShapeDtypeStruct((b, od), BF16), jax.ShapeDtypeStruct(state.shape, F32)],
        input_output_aliases=aliases,
        compiler_params=_cparams(("parallel",)),
        name="gdn_decode",
    )(proj, proj, proj, proj, proj, conv_t, conv_t, conv_t, state,
      conv_w8, conv_w8, conv_w8, alog_row, dtb_row, _rep8(norm_w), *acc_args)


def _ret_decode_kernel(q_ref, k_ref, v_ref, gate_ref, s_ref, cos_ref, sin_ref, nw_ref, *rest):
    o_ref, s_out_ref = rest[-2:]
    q = _rotary(q_ref[...], cos_ref[...], sin_ref[...])
    k = _rotary(k_ref[...], cos_ref[...], sin_ref[...]) * (RET_KEY_DIM ** -0.5)
    q_t = _transpose(q)
    k_t = _transpose(k)
    tb = q.shape[0]
    kd, vd = RET_KEY_DIM, RET_VAL_DIM
    rows = []
    for b in range(tb):
        ys = []
        for h in range(RET_HEADS):
            gamma = math.exp(_ret_log_gamma(h))
            s = s_ref[b, h]
            s_new = s * gamma + k_t[h * kd:(h + 1) * kd, b:b + 1] * v_ref[b:b + 1, h * vd:(h + 1) * vd]
            s_out_ref[b, h] = s_new
            ys.append(jnp.sum(q_t[h * kd:(h + 1) * kd, b:b + 1] * s_new, axis=0, keepdims=True))
        rows.append(jnp.concatenate(ys, axis=1))
    y = jnp.concatenate(rows, axis=0)
    for h in range(RET_HEADS):
        yh = (_mul8(_group_norm_rows(y[:, h * vd:(h + 1) * vd]), nw_ref[:, h * vd:(h + 1) * vd])
              * _silu(gate_ref[:, h * vd:(h + 1) * vd]))
        o_ref[:, h * vd:(h + 1) * vd] = yh.astype(o_ref.dtype)


def ret_decode(proj, state, layer, acc, cos_f, sin_s, norm_w, tb=8):
    b = proj.shape[0]
    pcol = lambda width, off: pl.BlockSpec((tb, width), lambda i: (i, off // width))
    sspec = pl.BlockSpec((None, tb, RET_HEADS, RET_KEY_DIM, RET_VAL_DIM), lambda i: (layer, i, 0, 0, 0))
    acc_specs, acc_args, aliases = _stacked_out(acc, 8, 1)
    return pl.pallas_call(
        _ret_decode_kernel,
        grid=(b // tb,),
        in_specs=[pcol(RET_QK_DIM, HYB_OFF_RQ), pcol(RET_QK_DIM, HYB_OFF_RK), pcol(RET_OUT_DIM, HYB_OFF_RV),
                  pcol(RET_OUT_DIM, HYB_OFF_RG), sspec, _resident((1, RET_QK_DIM)), _resident((1, RET_QK_DIM)),
                  _resident((SUBLANES, RET_OUT_DIM))] + acc_specs,
        out_specs=[pl.BlockSpec((tb, RET_OUT_DIM), lambda i: (i, 0)), sspec],
        out_shape=[jax.ShapeDtypeStruct((b, RET_OUT_DIM), BF16), jax.ShapeDtypeStruct(state.shape, F32)],
        input_output_aliases=aliases,
        compiler_params=_cparams(("parallel",)),
        name="ret_decode",
    )(proj, proj, proj, proj, state, cos_f, sin_s, _rep8(norm_w), *acc_args)


def _ssd_decode_kernel(z_ref, x_ref, b_ref, c_ref, dt_ref, bx_ref, bb_ref, bc_ref, s_ref,
                       cwx_ref, cwb_ref, cwc_ref, cbx_ref, cbb_ref, cbc_ref,
                       dtb_ref, alog_ref, dskip_ref, nw_ref, *rest):
    o_ref, s_out_ref = rest[-2:]
    hd, gw, ns = SSM_HEAD_DIM, SSM_GROUP_COLS, SSM_STATE
    tb, width = x_ref.shape
    ng = width // gw
    first_head = pl.program_id(1) * (ng * SSM_HPG)
    x = _silu(_conv_step(x_ref[...], bx_ref, cwx_ref, cbx_ref[...]))
    bm = _silu(_conv_step(b_ref[...], bb_ref, cwb_ref, cbb_ref[...]))
    cm = _silu(_conv_step(c_ref[...], bc_ref, cwc_ref, cbc_ref[...]))
    dt = _softplus(dt_ref[...] + dtb_ref[...])
    spread = (_iota((LANES, width), 0) == first_head + _iota((LANES, width), 1) // hd).astype(BF16)
    per_head = jnp.concatenate([dt, -jnp.exp(alog_ref[...]), dskip_ref[...],
                                jnp.zeros((2 * SUBLANES - tb - 2, LANES), F32)], axis=0)
    ph_hi, ph_lo = _split(per_head)
    per_col = (jnp.dot(ph_hi, spread, preferred_element_type=F32)
               + jnp.dot(ph_lo, spread, preferred_element_type=F32))
    dt_x, a_x, dskip_x = per_col[0:tb], per_col[tb:tb + 1], per_col[tb + 1:tb + 2]
    eg_x = jnp.exp(dt_x * a_x)
    xdt_t = (x * dt_x).T
    row_id = _iota((tb, ns), 0)
    pairs = [(b, g) for b in range(tb) for g in range(ng)]
    outer = [_dot3(_split(xdt_t[g * gw:(g + 1) * gw, :]),
                   _split(jnp.where(row_id == b, bm[:, g * ns:(g + 1) * ns], 0.0))) for b, g in pairs]
    for (b, g), out in zip(pairs, outer):
        for hl in range(SSM_HPG):
            h = g * SSM_HPG + hl
            s_out_ref[b, h] = s_ref[b, h] * eg_x[b:b + 1, h * hd:h * hd + 1] + out[hl * hd:(hl + 1) * hd, :]
    ys = [_dot(cm[b:b + 1, g * ns:(g + 1) * ns],
               s_out_ref[b, g * SSM_HPG:(g + 1) * SSM_HPG].reshape(gw, ns), _NT) for b, g in pairs]
    y = jnp.concatenate([jnp.concatenate(ys[b * ng:(b + 1) * ng], axis=1) for b in range(tb)], axis=0)
    y = (y + dskip_x * x) * _silu(z_ref[...])
    for g in range(ng):
        cols = slice(g * gw, (g + 1) * gw)
        o_ref[:, cols] = _rms_rows(y[:, cols], nw_ref[:, cols]).astype(o_ref.dtype)


def ssd_decode(proj, state_t, conv_t, layer, acc, conv_w, conv_b, dtb_row, alog_row, dskip_row, norm_w, tb=8, ng=4):
    b = proj.shape[0]
    gw = ng * SSM_GROUP_COLS
    ns = ng * SSM_STATE
    assert tb == SUBLANES
    cw8, cb8, nw8 = _rep8_rows(conv_w), _rep8(conv_b), _rep8(norm_w)
    pcol = lambda width, off: pl.BlockSpec((tb, width), lambda i, g: (i, off // width + g))
    ccol = lambda width, off: pl.BlockSpec((None, CONV_WIDTH - 1, tb, width),
                                           lambda i, g: (layer, 0, i, off // width + g))
    wcol = lambda width, off: pl.BlockSpec((CONV_WIDTH, SUBLANES, width), lambda i, g: (0, 0, off // width + g))
    rcol = lambda width, off: pl.BlockSpec((SUBLANES, width), lambda i, g: (0, off // width + g))
    row = _resident((1, LANES))
    sspec = pl.BlockSpec((None, tb, ng * SSM_HPG, SSM_HEAD_DIM, SSM_STATE), lambda i, g: (layer, i, g, 0, 0))
    off_b, off_c = SSM_INNER, SSM_INNER + SSM_BC_DIM
    acc_specs, acc_args, aliases = _stacked_out(acc, 19, 1)
    return pl.pallas_call(
        _ssd_decode_kernel,
        grid=(b // tb, SSM_GROUPS // ng),
        in_specs=[pcol(gw, SSM_OFF_Z), pcol(gw, SSM_OFF_X), pcol(ns, SSM_OFF_B), pcol(ns, SSM_OFF_C),
                  pl.BlockSpec((tb, LANES), lambda i, g: (i, SSM_OFF_DT // LANES)),
                  ccol(gw, 0), ccol(ns, off_b), ccol(ns, off_c), sspec,
                  wcol(gw, 0), wcol(ns, off_b), wcol(ns, off_c),
                  rcol(gw, 0), rcol(ns, off_b), rcol(ns, off_c),
                  row, row, row, rcol(gw, 0)] + acc_specs,
        out_specs=[pl.BlockSpec((tb, gw), lambda i, g: (i, g)), sspec],
        out_shape=[jax.ShapeDtypeStruct((b, SSM_INNER), BF16), jax.ShapeDtypeStruct(state_t.shape, F32)],
        input_output_aliases=aliases,
        compiler_params=_cparams(("parallel", "parallel")),
        name="ssd_decode",
    )(proj, proj, proj, proj, proj, conv_t, conv_t, conv_t, state_t,
      cw8, cw8, cw8, cb8, cb8, cb8, dtb_row, alog_row, dskip_row, nw8, *acc_args)


def _lane_row(vals, offset):
    return jnp.zeros((1, LANES), F32).at[0, offset:offset + vals.shape[0]].set(vals.astype(F32))


def _prep_hyb_w_in(w):
    sizes = (GDN_QKV_DIM, GDN_OUT_DIM, GDN_HEADS, GDN_HEADS, RET_QK_DIM, RET_QK_DIM, RET_OUT_DIM, RET_OUT_DIM)
    offs = [0]
    for s in sizes:
        offs.append(offs[-1] + s)
    part = lambda i: w[:, offs[i]:offs[i + 1]]
    cols = [part(0), part(1), part(4), part(5), part(6), part(7), part(2), part(3)]
    out = jnp.concatenate(cols, axis=1)
    return jnp.pad(out, ((0, 0), (0, HYB_N - out.shape[1]))).astype(BF16)


def _prep_ssm_w_in(w):
    return jnp.pad(w, ((0, 0), (0, SSM_N - w.shape[1]))).astype(BF16)


def _rope_tables(pos):
    half = RET_KEY_DIM // 2
    inv_freq = ROPE_BASE ** (-jnp.arange(half, dtype=F32) / half)
    ang = pos.astype(F32)[:, None] * inv_freq[None, :]
    cos, sin = jnp.cos(ang), jnp.sin(ang)
    cos_f = jnp.tile(jnp.concatenate([cos, cos], axis=1), (1, RET_HEADS))
    sin_s = jnp.tile(jnp.concatenate([-sin, sin], axis=1), (1, RET_HEADS))
    return cos_f, sin_s


def _trunk(x, pos, states, params, prompt):
    bsz, l, d = x.shape
    m = bsz * l
    tm = 512 if m % 512 == 0 else m
    x2 = x.reshape(m, d)
    cos_f, sin_s = _rope_tables(pos)
    new_gdn, new_gdn_conv, new_ret, new_ssm, new_ssm_conv = [], [], [], [], []
    if not prompt:
        s_gdn, c_gdn, s_ret, s_ssm, c_ssm = states
        c_gdn_t, c_ssm_t = jnp.swapaxes(c_gdn, 1, 2), jnp.swapaxes(c_ssm, 1, 2)
        s_ssm_t = jnp.swapaxes(s_ssm, 3, 4)
        acc_gdn, acc_gdn_conv, acc_ret, acc_ssm, acc_ssm_conv = (
            jnp.zeros_like(a) for a in (s_gdn, c_gdn_t, s_ret, s_ssm_t, c_ssm_t))
    for layer in range(DEPTH):
        i = layer // 2
        p = params
        if layer % 2 == 0:
            proj = norm_matmul(x2, p["norm_mix"][layer], p["w_in_hyb"][i], tm)
            alog_row = _lane_row(p["gdn_a_log"][i], GDN_HEADS)
            dtb_row = _lane_row(p["gdn_dt_bias"][i], GDN_HEADS)
            if prompt:
                proj3 = proj.reshape(bsz, l, HYB_N)
                o_a, sg = gdn_prompt(proj3, p["gdn_conv_w"][i], alog_row, dtb_row, p["gdn_norm_w"][i])
                o_b, sr = ret_prompt(proj3, cos_f, sin_s, p["ret_norm_w"][i])
                cg = proj3[:, l - (CONV_WIDTH - 1):, :GDN_QKV_DIM]
                o_a, o_b = o_a.reshape(m, GDN_OUT_DIM), o_b.reshape(m, RET_OUT_DIM)
                new_gdn.append(sg)
                new_gdn_conv.append(cg)
                new_ret.append(sr)
            else:
                o_a, acc_gdn = gdn_decode(proj, s_gdn, c_gdn_t, i, acc_gdn, p["gdn_conv_w"][i], alog_row,
                                          dtb_row, p["gdn_norm_w"][i])
                acc_gdn_conv = conv_state_update(c_gdn_t, proj, HYB_OFF_QKV, i, acc_gdn_conv, GDN_OUT_DIM)
                o_b, acc_ret = ret_decode(proj, s_ret, i, acc_ret, cos_f, sin_s, p["ret_norm_w"][i])
            w_out = p["w_out_hyb"][i]
            acts, ws = [o_a, o_b], [w_out[:GDN_OUT_DIM], w_out[GDN_OUT_DIM:]]
        else:
            proj = norm_matmul(x2, p["norm_mix"][layer], p["w_in_ssm"][i], tm)
            dtb_row = _lane_row(p["ssm_dt_bias"][i], 0)
            alog_row = _lane_row(p["ssm_a_log"][i], 0)
            dskip_row = _lane_row(p["ssm_d"][i], 0)
            if prompt:
                proj3 = proj.reshape(bsz, l, SSM_N)
                y, ss = ssd_prompt(proj3, p["ssm_conv_w"][i], p["ssm_conv_b"][i], dtb_row, alog_row,
                                   dskip_row, p["ssm_norm_w"][i])
                cs = proj3[:, l - (CONV_WIDTH - 1):, SSM_OFF_X:SSM_OFF_X + SSM_CONV_DIM]
                y = y.reshape(m, SSM_INNER)
                new_ssm.append(ss)
                new_ssm_conv.append(cs)
            else:
                y, acc_ssm = ssd_decode(proj, s_ssm_t, c_ssm_t, i, acc_ssm, p["ssm_conv_w"][i],
                                        p["ssm_conv_b"][i], dtb_row, alog_row, dskip_row, p["ssm_norm_w"][i])
                acc_ssm_conv = conv_state_update(c_ssm_t, proj, SSM_OFF_X, i, acc_ssm_conv, D_MODEL)
            acts, ws = [y], [p["w_out_ssm"][i]]
        x2 = post_mixer(acts, ws, x2, p["norm_mlp"][layer], p["mlp_w1"][layer], p["mlp_w2"][layer],
                        p["norm_final"], layer == DEPTH - 1, tm)
    y_out = x2.reshape(bsz, l, d)
    if prompt:
        return (y_out, jnp.stack(new_gdn), jnp.stack(new_gdn_conv), jnp.stack(new_ret),
                jnp.swapaxes(jnp.stack(new_ssm), 3, 4), jnp.stack(new_ssm_conv))
    return (y_out, acc_gdn, jnp.swapaxes(acc_gdn_conv, 1, 2), acc_ret,
            jnp.swapaxes(acc_ssm, 3, 4), jnp.swapaxes(acc_ssm_conv, 1, 2))


def kernel(x_prompt, x_sample, state_gdn, state_gdn_conv, state_ret, state_ssm, state_ssm_conv, norm_mix, norm_mlp, norm_final, w_in_hyb, gdn_conv_w, gdn_a_log, gdn_dt_bias, gdn_norm_w, ret_norm_w, w_out_hyb, w_in_ssm, ssm_conv_w, ssm_conv_b, ssm_dt_bias, ssm_a_log, ssm_d, ssm_norm_w, w_out_ssm, mlp_w1, mlp_w2):
    n_hyb, n_ssm = w_in_hyb.shape[0], w_in_ssm.shape[0]
    params = dict(
        norm_mix=norm_mix, norm_mlp=norm_mlp, norm_final=norm_final,
        w_in_hyb=[_prep_hyb_w_in(w_in_hyb[i]) for i in range(n_hyb)],
        gdn_conv_w=gdn_conv_w, gdn_a_log=gdn_a_log, gdn_dt_bias=gdn_dt_bias, gdn_norm_w=gdn_norm_w,
        ret_norm_w=ret_norm_w, w_out_hyb=w_out_hyb.astype(BF16),
        w_in_ssm=[_prep_ssm_w_in(w_in_ssm[i]) for i in range(n_ssm)],
        ssm_conv_w=ssm_conv_w, ssm_conv_b=ssm_conv_b, ssm_dt_bias=ssm_dt_bias, ssm_a_log=ssm_a_log,
        ssm_d=ssm_d, ssm_norm_w=ssm_norm_w, w_out_ssm=w_out_ssm.astype(BF16),
        mlp_w1=mlp_w1.astype(BF16), mlp_w2=mlp_w2.astype(BF16))
    lp, ls = x_prompt.shape[1], x_sample.shape[1]
    pos_prompt = jnp.arange(lp, dtype=jnp.int32)
    pos_sample = PAST_LEN + jnp.arange(ls, dtype=jnp.int32)
    y_p, p_gdn, p_gdn_conv, p_ret, p_ssm, p_ssm_conv = _trunk(
        x_prompt, pos_prompt, (None,) * 5, params, prompt=True)
    y_s, s_gdn, s_gdn_conv, s_ret, s_ssm, s_ssm_conv = _trunk(
        x_sample, pos_sample, (state_gdn, state_gdn_conv, state_ret, state_ssm, state_ssm_conv),
        params, prompt=False)
    return (y_p, y_s, p_gdn, p_gdn_conv, p_ret, p_ssm, p_ssm_conv,
            s_gdn, s_gdn_conv, s_ret, s_ssm, s_ssm_conv)
```

```python
import functools
import math

import jax
import jax.numpy as jnp
from jax import lax
from jax.experimental import pallas as pl
from jax.experimental.pallas import tpu as pltpu

F32 = jnp.float32
BF16 = jnp.bfloat16

D_MODEL = 1024
DEPTH = 4
CONV_WIDTH = 4
NORM_EPS = 1e-6
PAST_LEN = 16384

GDN_HEADS = 4
GDN_DIM = 128
GDN_QKV_DIM = 3 * GDN_HEADS * GDN_DIM
GDN_OUT_DIM = GDN_HEADS * GDN_DIM

RET_HEADS = 4
RET_KEY_DIM = 64
RET_VAL_DIM = 128
RET_QK_DIM = RET_HEADS * RET_KEY_DIM
RET_OUT_DIM = RET_HEADS * RET_VAL_DIM
ROPE_BASE = 10000.0

SSM_INNER = 2 * D_MODEL
SSM_HEAD_DIM = 64
SSM_HEADS = SSM_INNER // SSM_HEAD_DIM
SSM_GROUPS = 4
SSM_HPG = SSM_HEADS // SSM_GROUPS
SSM_STATE = 128
SSM_GROUP_COLS = SSM_HPG * SSM_HEAD_DIM
SSM_BC_DIM = SSM_GROUPS * SSM_STATE
SSM_CONV_DIM = SSM_INNER + 2 * SSM_BC_DIM
MLP_HIDDEN = 4 * D_MODEL

LANES = 128
SUBLANES = 8

HYB_OFF_QKV = 0
HYB_OFF_Z = GDN_QKV_DIM
HYB_OFF_RQ = HYB_OFF_Z + GDN_OUT_DIM
HYB_OFF_RK = HYB_OFF_RQ + RET_QK_DIM
HYB_OFF_RV = HYB_OFF_RK + RET_QK_DIM
HYB_OFF_RG = HYB_OFF_RV + RET_OUT_DIM
HYB_OFF_BA = HYB_OFF_RG + RET_OUT_DIM
HYB_N = 3840
SSM_OFF_Z = 0
SSM_OFF_X = SSM_INNER
SSM_OFF_B = SSM_OFF_X + SSM_INNER
SSM_OFF_C = SSM_OFF_B + SSM_BC_DIM
SSM_OFF_DT = SSM_OFF_C + SSM_BC_DIM
SSM_N = 5376
PROJ_TN = 768

VMEM_LIMIT = 56 * 1024 * 1024

_NN = (((1,), (0,)), ((), ()))
_NT = (((1,), (1,)), ((), ()))
_TN = (((0,), (0,)), ((), ()))


def _dot(a, b, dims=_NN):
    return lax.dot_general(a.astype(BF16), b.astype(BF16), dims, preferred_element_type=F32)


def _dot_f32(a, b, dims=_NN):
    return lax.dot_general(a, b, dims, precision=lax.Precision.HIGHEST, preferred_element_type=F32)


def _sigmoid(x):
    return 0.5 + 0.5 * jnp.tanh(0.5 * x)


def _silu(x):
    t = 0.5 * x
    return t + t * jnp.tanh(t)


def _softplus(x):
    return jnp.maximum(x, 0.0) + jnp.log(1.0 + jnp.exp(-jnp.abs(x)))


def _iota(shape, dim):
    return lax.broadcasted_iota(jnp.int32, shape, dim)


def _eye(n):
    return (_iota((n, n), 0) == _iota((n, n), 1)).astype(F32)


def _transpose(x):
    return x.T


def _cparams(sem):
    return pltpu.CompilerParams(dimension_semantics=sem, vmem_limit_bytes=VMEM_LIMIT)


def _resident(shape):
    nd = len(shape)
    return pl.BlockSpec(shape, lambda *_: (0,) * nd, pipeline_mode=pl.Buffered(1))


def _resident_layer(shape, layer, rows=None, row_block=0):
    rows = shape[1] if rows is None else rows
    return pl.BlockSpec((None, rows, shape[2]), lambda *_: (layer, row_block, 0), pipeline_mode=pl.Buffered(1))


def _rep8(v):
    v = v.reshape(1, -1)
    return jnp.broadcast_to(v, (SUBLANES, v.shape[1]))


def _rep8_rows(w):
    return jnp.broadcast_to(w[:, None, :], (w.shape[0], SUBLANES, w.shape[1]))


def _mul8(x, r8):
    r, n = x.shape
    return (x.reshape(r // SUBLANES, SUBLANES, n) * r8).reshape(r, n)


def _add8(x, r8):
    r, n = x.shape
    return (x.reshape(r // SUBLANES, SUBLANES, n) + r8).reshape(r, n)


def _rms_rows(x, gain8):
    return _mul8(x * lax.rsqrt(jnp.mean(x * x, axis=-1, keepdims=True) + NORM_EPS), gain8)


def _norm_matmul_kernel(x_ref, g_ref, w_ref, o_ref, *, tn):
    xn = _rms_rows(x_ref[...], g_ref[...]).astype(BF16)
    for j in range(w_ref.shape[1] // tn):
        o_ref[:, j * tn:(j + 1) * tn] = jnp.dot(xn, w_ref[:, j * tn:(j + 1) * tn],
                                                preferred_element_type=F32)


def norm_matmul(x, gains8, layer, w, w_layer, tm):
    m, d = x.shape
    n = w.shape[2]
    return pl.pallas_call(
        functools.partial(_norm_matmul_kernel, tn=PROJ_TN),
        grid=(m // tm,),
        in_specs=[pl.BlockSpec((tm, d), lambda i: (i, 0)), _resident_layer(gains8.shape, layer),
                  _resident_layer(w.shape, w_layer)],
        out_specs=pl.BlockSpec((tm, n), lambda i: (i, 0)),
        out_shape=jax.ShapeDtypeStruct((m, n), F32),
        compiler_params=_cparams(("parallel",)),
        name="norm_matmul",
    )(x, gains8, w)


def _post_mixer_kernel(*refs, n_in, tf, final_norm):
    a_refs, w_refs = refs[:n_in], refs[n_in:2 * n_in]
    r_ref, g_ref, w1_ref, w2_ref, gf_ref, o_ref = refs[2 * n_in:]
    x = r_ref[...]
    for a_ref, w_ref in zip(a_refs, w_refs):
        x = x + jnp.dot(a_ref[...], w_ref[...], preferred_element_type=F32)
    o_ref[...] = x
    x = o_ref[...]
    xn = _rms_rows(x, g_ref[...]).astype(BF16)
    acc = x
    for j in range(w1_ref.shape[1] // tf):
        h = jnp.dot(xn, w1_ref[:, j * tf:(j + 1) * tf], preferred_element_type=F32)
        h = jnp.maximum(h, 0.0)
        acc = acc + jnp.dot((h * h).astype(BF16), w2_ref[j * tf:(j + 1) * tf, :],
                            preferred_element_type=F32)
    if final_norm:
        acc = _rms_rows(acc, gf_ref[...])
    o_ref[...] = acc


def post_mixer(acts, w_out, w_layer, res, gains8, layer, w1, w2, final8, final_norm, tm):
    m, d = res.shape
    n_in = len(acts)
    k_rows = acts[0].shape[1]
    assert all(a.shape[1] == k_rows for a in acts) and n_in * k_rows == w_out.shape[1]
    in_specs = ([pl.BlockSpec((tm, k_rows), lambda i: (i, 0)) for _ in acts]
                + [_resident_layer(w_out.shape, w_layer, k_rows, j) for j in range(n_in)]
                + [pl.BlockSpec((tm, d), lambda i: (i, 0)), _resident_layer(gains8.shape, layer),
                   _resident_layer(w1.shape, layer), _resident_layer(w2.shape, layer), _resident((SUBLANES, d))])
    return pl.pallas_call(
        functools.partial(_post_mixer_kernel, n_in=n_in, tf=1024, final_norm=final_norm),
        grid=(m // tm,),
        in_specs=in_specs,
        out_specs=pl.BlockSpec((tm, d), lambda i: (i, 0)),
        out_shape=jax.ShapeDtypeStruct((m, d), F32),
        compiler_params=_cparams(("parallel",)),
        name="post_mixer",
    )(*acts, *([w_out] * n_in), res, gains8, w1, w2, final8)


def _causal_conv_chunk(x, tail, w8, bias8=None):
    c, n = x.shape
    g = c // SUBLANES
    x3 = x.reshape(g, SUBLANES, n)
    acc = x3 * w8[CONV_WIDTH - 1]
    row = _iota((SUBLANES, n), 0)
    for k in range(1, CONV_WIDTH):
        r = pltpu.roll(x3, k, 1)
        prev = jnp.concatenate([pltpu.roll(tail, k, 0)[None], r[:g - 1]], axis=0)
        acc = acc + jnp.where(row < k, prev, r) * w8[CONV_WIDTH - 1 - k]
    if bias8 is not None:
        acc = acc + bias8
    return acc.reshape(c, n)


def _cumsum_rows(g):
    c = g.shape[0]
    tri = (_iota((c, c), 0) >= _iota((c, c), 1)).astype(F32)
    return _dot_f32(tri, g)


def _decay_matrix(gc_col, gc_row):
    c = gc_col.shape[0]
    causal = _iota((c, c), 0) >= _iota((c, c), 1)
    return jnp.where(causal, jnp.exp(gc_col - gc_row), 0.0)


def _split(x):
    hi = x.astype(BF16)
    return hi, (x - hi.astype(F32)).astype(BF16)


def _dot3(a, b, dims=_NN):
    (ah, al), (bh, bl) = a, b
    d = lambda p, q: lax.dot_general(p, q, dims, preferred_element_type=F32)
    return d(ah, bh) + (d(ah, bl) + d(al, bh))


def _unit_lower_inverses(n_list):
    c = n_list[0].shape[0]
    ri, ci = _iota((c, c), 0), _iota((c, c), 1)
    eye = (ri == ci).astype(F32)
    blk = 16
    same = (ri // blk) == (ci // blk)
    p = [jnp.where(same, -n, 0.0).astype(BF16) for n in n_list]
    inv = [eye + x for x in p]
    for _ in range(3):
        p = [_dot(x, x).astype(BF16) for x in p]
        inv = [i + _dot(i, x) for i, x in zip(inv, p)]
    while blk < c:
        pair = ((ri // (2 * blk)) == (ci // (2 * blk))) & ((ri // blk) != (ci // blk))
        inv_b = [i.astype(BF16) for i in inv]
        t = [_dot(jnp.where(pair, n, 0.0), i) for n, i in zip(n_list, inv_b)]
        inv = [i - _dot(i_b, x) for i, i_b, x in zip(inv, inv_b, t)]
        blk *= 2
    return inv


def _gdn_prompt_kernel(qkv_ref, z_ref, ba_ref, cw_ref, alog_ref, dtb_ref, nw_ref,
                       o_ref, s_out_ref, s_ref, tail_ref):
    c_idx = pl.program_id(1)

    @pl.when(c_idx == 0)
    def _():
        s_ref[...] = jnp.zeros_like(s_ref)
        tail_ref[...] = jnp.zeros_like(tail_ref)

    nb, c = qkv_ref.shape[0], qkv_ref.shape[1]
    hd = GDN_DIM
    ri, ci = _iota((c, c), 0), _iota((c, c), 1)
    strict = ri > ci
    chains = [(bb, h) for bb in range(nb) for h in range(GDN_HEADS)]

    conv, beta, gcum, gcum_t = [], [], [], []
    for bb in range(nb):
        x = qkv_ref[bb]
        conv.append(_silu(_causal_conv_chunk(x, tail_ref[bb], cw_ref[...])))
        tail_ref[bb] = x[c - SUBLANES:c]
        ba = ba_ref[bb]
        beta.append(_sigmoid(ba))
        gc = _cumsum_rows(-jnp.exp(alog_ref[...]) * _softplus(ba + dtb_ref[...]))
        gcum.append(gc)
        gcum_t.append(_transpose(gc))

    q, k, v, b_col, gc_col, g_end, decay = [], [], [], [], [], [], []
    for bb, h in chains:
        qh = conv[bb][:, h * hd:(h + 1) * hd]
        kh = conv[bb][:, GDN_OUT_DIM + h * hd:GDN_OUT_DIM + (h + 1) * hd]
        q.append(qh * lax.rsqrt(jnp.sum(qh * qh, axis=-1, keepdims=True) + NORM_EPS) * (hd ** -0.5))
        k.append(kh * lax.rsqrt(jnp.sum(kh * kh, axis=-1, keepdims=True) + NORM_EPS))
        v.append(conv[bb][:, 2 * GDN_OUT_DIM + h * hd:2 * GDN_OUT_DIM + (h + 1) * hd])
        b_col.append(beta[bb][:, h:h + 1])
        gc_col.append(gcum[bb][:, GDN_HEADS + h:GDN_HEADS + h + 1])
        g_end.append(gcum[bb][c - 1:c, GDN_HEADS + h:GDN_HEADS + h + 1])
        decay.append(_decay_matrix(gc_col[-1], gcum_t[bb][GDN_HEADS + h:GDN_HEADS + h + 1, :]))
    kk = [_dot(x, x, _NT) for x in k]
    qk = [_dot(x, y, _NT) * d for x, y, d in zip(q, k, decay)]
    t_inv = _unit_lower_inverses([jnp.where(strict, b * x * d, 0.0) for b, x, d in zip(b_col, kk, decay)])
    rhs = [jnp.concatenate([vv * b, kx * (b * jnp.exp(g))], axis=1)
           for vv, kx, b, g in zip(v, k, b_col, gc_col)]
    sol = [_dot(t, r) for t, r in zip(t_inv, rhs)]
    s_old = [s_ref[bb, h] for bb, h in chains]
    delta = [x[:, :hd] - _dot(x[:, hd:], s) for x, s in zip(sol, s_old)]
    y = [_dot(x * jnp.exp(g), s) + _dot(a, dl)
         for x, g, s, a, dl in zip(q, gc_col, s_old, qk, delta)]
    s_new = [s * jnp.exp(ge) + _dot(kx * jnp.exp(ge - g), dl, _TN)
             for s, ge, kx, g, dl in zip(s_old, g_end, k, gc_col, delta)]
    for (bb, h), sn, yy in zip(chains, s_new, y):
        s_ref[bb, h] = sn
        out = _rms_rows(yy, nw_ref[...]) * _silu(z_ref[bb, :, h * hd:(h + 1) * hd])
        o_ref[bb, :, h * hd:(h + 1) * hd] = out.astype(o_ref.dtype)

    @pl.when(c_idx == pl.num_programs(1) - 1)
    def _():
        s_out_ref[...] = s_ref[...]


def gdn_prompt(proj, conv_w, alog_row, dtb_row, norm_w, chunk=64, nb=4):
    b, l, _ = proj.shape
    nc = l // chunk
    col = lambda width, off: pl.BlockSpec((nb, chunk, width), lambda i, j: (i, j, off // width))
    return pl.pallas_call(
        _gdn_prompt_kernel,
        grid=(b // nb, nc),
        in_specs=[col(GDN_QKV_DIM, HYB_OFF_QKV), col(GDN_OUT_DIM, HYB_OFF_Z), col(LANES, HYB_OFF_BA),
                  _resident((CONV_WIDTH, SUBLANES, GDN_QKV_DIM)), _resident((1, LANES)), _resident((1, LANES)),
                  _resident((SUBLANES, GDN_DIM))],
        out_specs=[pl.BlockSpec((nb, chunk, GDN_OUT_DIM), lambda i, j: (i, j, 0)),
                   pl.BlockSpec((nb, GDN_HEADS, GDN_DIM, GDN_DIM), lambda i, j: (i, 0, 0, 0))],
        out_shape=[jax.ShapeDtypeStruct((b, l, GDN_OUT_DIM), BF16),
                   jax.ShapeDtypeStruct((b, GDN_HEADS, GDN_DIM, GDN_DIM), F32)],
        scratch_shapes=[pltpu.VMEM((nb, GDN_HEADS, GDN_DIM, GDN_DIM), F32),
                        pltpu.VMEM((nb, SUBLANES, GDN_QKV_DIM), F32)],
        compiler_params=_cparams(("parallel", "arbitrary")),
        name="gdn_prompt",
    )(proj, proj, proj, _rep8_rows(conv_w), alog_row, dtb_row, _rep8(norm_w))


def _ret_log_gamma(h):
    return math.log(1.0 - 2.0 ** (-5.0 - h))


def _rotary(t, cos_f, sin_s):
    n = t.shape[1]
    half = RET_KEY_DIM // 2
    first = (_iota(t.shape, 1) % RET_KEY_DIM) < half
    swapped = jnp.where(first, pltpu.roll(t, n - half, 1), pltpu.roll(t, half, 1))
    return t * cos_f + swapped * sin_s


def _group_norm_rows(x):
    xc = x - jnp.mean(x, axis=-1, keepdims=True)
    return xc * lax.rsqrt(jnp.mean(xc * xc, axis=-1, keepdims=True) + NORM_EPS)


def _ret_prompt_kernel(q_ref, k_ref, v_ref, gate_ref, cos_ref, sin_ref, nw_ref,
                       o_ref, s_out_ref, s_ref):
    c_idx = pl.program_id(1)

    @pl.when(c_idx == 0)
    def _():
        s_ref[...] = jnp.zeros_like(s_ref)

    nb, c = q_ref.shape[0], q_ref.shape[1]
    kd, vd = RET_KEY_DIM, RET_VAL_DIM
    ri, ci = _iota((c, c), 0), _iota((c, c), 1)
    causal = ri >= ci
    pos_col = _iota((c, 1), 0).astype(F32)
    lgs = [_ret_log_gamma(h) for h in range(RET_HEADS)]
    decay = [jnp.where(causal, jnp.exp((ri - ci).astype(F32) * lg), 0.0) for lg in lgs]
    grow = [jnp.exp((pos_col + 1.0) * lg) for lg in lgs]
    gtail = [jnp.exp((c - 1.0 - pos_col) * lg) for lg in lgs]
    chains = [(bb, h) for bb in range(nb) for h in range(RET_HEADS)]
    q = [_rotary(q_ref[bb], cos_ref[...], sin_ref[...]) for bb in range(nb)]
    k = [_rotary(k_ref[bb], cos_ref[...], sin_ref[...]) * (kd ** -0.5) for bb in range(nb)]
    qh = [q[bb][:, h * kd:(h + 1) * kd] for bb, h in chains]
    kh = [k[bb][:, h * kd:(h + 1) * kd] for bb, h in chains]
    vh = [v_ref[bb, :, h * vd:(h + 1) * vd] for bb, h in chains]
    s_old = [s_ref[bb, h] for bb, h in chains]
    att = [_dot(a, b, _NT) * decay[h] for a, b, (_, h) in zip(qh, kh, chains)]
    y = [_dot(a, v) + _dot(qq, s) * grow[h] for a, v, qq, s, (_, h) in zip(att, vh, qh, s_old, chains)]
    s_new = [s * math.exp(c * lgs[h]) + _dot(kk * gtail[h], v, _TN)
             for s, kk, v, (_, h) in zip(s_old, kh, vh, chains)]
    for (bb, h), yy, sn in zip(chains, y, s_new):
        s_ref[bb, h] = sn
        out = (_mul8(_group_norm_rows(yy), nw_ref[:, h * vd:(h + 1) * vd])
               * _silu(gate_ref[bb, :, h * vd:(h + 1) * vd]))
        o_ref[bb, :, h * vd:(h + 1) * vd] = out.astype(o_ref.dtype)

    @pl.when(c_idx == pl.num_programs(1) - 1)
    def _():
        s_out_ref[...] = s_ref[...]


def ret_prompt(proj, cos_f, sin_s, norm_w, chunk=128, nb=2):
    b, l, _ = proj.shape
    nc = l // chunk
    col = lambda width, off: pl.BlockSpec((nb, chunk, width), lambda i, j: (i, j, off // width))
    tab = pl.BlockSpec((chunk, RET_QK_DIM), lambda i, j: (j, 0))
    return pl.pallas_call(
        _ret_prompt_kernel,
        grid=(b // nb, nc),
        in_specs=[col(RET_QK_DIM, HYB_OFF_RQ), col(RET_QK_DIM, HYB_OFF_RK), col(RET_OUT_DIM, HYB_OFF_RV),
                  col(RET_OUT_DIM, HYB_OFF_RG), tab, tab, _resident((SUBLANES, RET_OUT_DIM))],
        out_specs=[pl.BlockSpec((nb, chunk, RET_OUT_DIM), lambda i, j: (i, j, 0)),
                   pl.BlockSpec((nb, RET_HEADS, RET_KEY_DIM, RET_VAL_DIM), lambda i, j: (i, 0, 0, 0))],
        out_shape=[jax.ShapeDtypeStruct((b, l, RET_OUT_DIM), BF16),
                   jax.ShapeDtypeStruct((b, RET_HEADS, RET_KEY_DIM, RET_VAL_DIM), F32)],
        scratch_shapes=[pltpu.VMEM((nb, RET_HEADS, RET_KEY_DIM, RET_VAL_DIM), F32)],
        compiler_params=_cparams(("parallel", "arbitrary")),
        name="ret_prompt",
    )(proj, proj, proj, proj, cos_f, sin_s, _rep8(norm_w))


def _split3(x):
    a = x.astype(BF16)
    r = x - a.astype(F32)
    b = r.astype(BF16)
    return a, b, (r - b.astype(F32)).astype(BF16)


def _spread(x, sel):
    return sum(jnp.dot(p, sel, preferred_element_type=F32) for p in _split3(x))


def _ssd_prompt_kernel(z_ref, x_ref, b_ref, c_ref, dt_ref, cw_ref, cb_ref, dtb_ref, alog_ref, dskip_ref,
                       nw_ref, sel_hd_ref, sel_c_ref, o_ref, s_out_ref, s_ref, tx_ref, tb_ref, tc_ref):
    c_idx = pl.program_id(1)

    @pl.when(c_idx == 0)
    def _():
        s_ref[...] = jnp.zeros_like(s_ref)
        tx_ref[...] = jnp.zeros_like(tx_ref)
        tb_ref[...] = jnp.zeros_like(tb_ref)
        tc_ref[...] = jnp.zeros_like(tc_ref)

    c = x_ref.shape[0]
    hd, gw, ns = SSM_HEAD_DIM, SSM_GROUP_COLS, SSM_STATE
    off_b, off_c = SSM_INNER, SSM_INNER + SSM_BC_DIM
    groups = range(SSM_GROUPS)

    br, cr = b_ref[...], c_ref[...]
    bm = _silu(_causal_conv_chunk(br, tb_ref[...], cw_ref[:, :, off_b:off_c], cb_ref[:, off_b:off_c]))
    cm = _silu(_causal_conv_chunk(cr, tc_ref[...], cw_ref[:, :, off_c:], cb_ref[:, off_c:]))
    tb_ref[...] = br[c - SUBLANES:c]
    tc_ref[...] = cr[c - SUBLANES:c]
    bg = [bm[:, g * ns:(g + 1) * ns] for g in groups]
    cg = [cm[:, g * ns:(g + 1) * ns] for g in groups]
    s_old = [s_ref[g] for g in groups]
    scores = [_dot(cc, bb, _NT) for cc, bb in zip(cg, bg)]
    y_inter = jnp.concatenate([_dot(cc, s) for cc, s in zip(cg, s_old)], axis=1)

    dt = _softplus(dt_ref[...] + dtb_ref[...])
    gcum = _cumsum_rows(dt * (-jnp.exp(alog_ref[...])))
    gcum_t = _transpose(gcum)
    per_col = _spread(jnp.concatenate([dt, gcum], axis=0), sel_hd_ref[...])
    dt_x, gc_x = per_col[:c], per_col[c:]
    gc_colb = _spread(gcum, sel_c_ref[...])
    ge8 = jnp.broadcast_to(gc_x[c - 1:c, :], (SUBLANES, SSM_INNER))

    xr = x_ref[...]
    x = _silu(_causal_conv_chunk(xr, tx_ref[...], cw_ref[:, :, :off_b], cb_ref[:, :off_b]))
    tx_ref[...] = xr[c - SUBLANES:c]

    xdt = x * dt_x
    xdec = xdt * jnp.exp(_add8(-gc_x, ge8))
    att =[scores[h // SSM_HPG] * _decay_matrix(gc_colb[:, h * c:(h + 1) * c], gcum_t[h:h + 1, :])
           for h in range(SSM_HEADS)]
    y_intra = jnp.concatenate([_dot(a, xdt[:, h * hd:(h + 1) * hd]) for h, a in enumerate(att)], axis=1)
    for g in groups:
        cols = slice(g * gw, (g + 1) * gw)
        s_ref[g] = _mul8(s_old[g], jnp.exp(ge8[:, cols])) + _dot(bg[g], xdec[:, cols], _TN)
    y = (y_intra + y_inter * jnp.exp(gc_x) + _mul8(x, dskip_ref[...])) * _silu(z_ref[...])
    for g in groups:
        cols = slice(g * gw, (g + 1) * gw)
        o_ref[:, cols] = _rms_rows(y[:, cols], nw_ref[:, cols]).astype(o_ref.dtype)

    @pl.when(c_idx == pl.num_programs(1) - 1)
    def _():
        for g in range(SSM_GROUPS):
            s_out_ref[g * SSM_HPG:(g + 1) * SSM_HPG] = _transpose(s_ref[g]).reshape(SSM_HPG, hd, ns)


def ssd_prompt(proj, conv_w, conv_b, dtb_row, alog_row, dskip_row, norm_w, chunk=128):
    b, l, _ = proj.shape
    assert chunk == LANES
    nc = l // chunk
    col = lambda width, off: pl.BlockSpec((None, chunk, width), lambda i, j: (i, j, off // width))
    row = _resident((1, LANES))
    lane = jnp.arange(LANES, dtype=jnp.int32)[:, None]
    sel_hd = (lane == jnp.arange(SSM_INNER, dtype=jnp.int32)[None, :] // SSM_HEAD_DIM).astype(BF16)
    sel_c = (lane == jnp.arange(SSM_HEADS * chunk, dtype=jnp.int32)[None, :] // chunk).astype(BF16)
    dskip8 = _rep8(jnp.repeat(dskip_row[0, :SSM_HEADS], SSM_HEAD_DIM))
    return pl.pallas_call(
        _ssd_prompt_kernel,
        grid=(b, nc),
        in_specs=[col(SSM_INNER, SSM_OFF_Z), col(SSM_INNER, SSM_OFF_X), col(SSM_BC_DIM, SSM_OFF_B),
                  col(SSM_BC_DIM, SSM_OFF_C), col(LANES, SSM_OFF_DT),
                  _resident((CONV_WIDTH, SUBLANES, SSM_CONV_DIM)), _resident((SUBLANES, SSM_CONV_DIM)),
                  row, row, _resident((SUBLANES, SSM_INNER)), _resident((SUBLANES, SSM_INNER)),
                  _resident(sel_hd.shape), _resident(sel_c.shape)],
        out_specs=[pl.BlockSpec((None, chunk, SSM_INNER), lambda i, j: (i, j, 0)),
                   pl.BlockSpec((None, SSM_HEADS, SSM_HEAD_DIM, SSM_STATE), lambda i, j: (i, 0, 0, 0))],
        out_shape=[jax.ShapeDtypeStruct((b, l, SSM_INNER), BF16),
                   jax.ShapeDtypeStruct((b, SSM_HEADS, SSM_HEAD_DIM, SSM_STATE), F32)],
        scratch_shapes=[pltpu.VMEM((SSM_GROUPS, SSM_STATE, SSM_GROUP_COLS), F32),
                        pltpu.VMEM((SUBLANES, SSM_INNER), F32),
                        pltpu.VMEM((SUBLANES, SSM_BC_DIM), F32), pltpu.VMEM((SUBLANES, SSM_BC_DIM), F32)],
        compiler_params=_cparams(("parallel", "arbitrary")),
        name="ssd_prompt",
    )(proj, proj, proj, proj, proj, _rep8_rows(conv_w), _rep8(conv_b), dtb_row, alog_row, dskip8,
      _rep8(norm_w), sel_hd, sel_c)


def _stacked_out(acc, n_inputs, out_index):
    return [pl.BlockSpec(memory_space=pl.ANY)], [acc], {n_inputs: out_index}


def _conv_step(x, buf_ref, w8_ref, bias8=None):
    acc = x * w8_ref[CONV_WIDTH - 1]
    for i in range(CONV_WIDTH - 1):
        acc = acc + buf_ref[i] * w8_ref[i]
    if bias8 is not None:
        acc = acc + bias8
    return acc


def _conv_state_kernel(buf_ref, x_ref, *rest):
    out_ref = rest[-1]
    for i in range(CONV_WIDTH - 2):
        out_ref[i] = buf_ref[i + 1]
    out_ref[CONV_WIDTH - 2] = x_ref[...]


def conv_state_update(conv_t, proj, col_off, layer, acc, wc):
    _, nbuf, b, n = conv_t.shape
    spec = pl.BlockSpec((None, nbuf, b, wc), lambda j: (layer, 0, 0, j))
    acc_specs, acc_args, aliases = _stacked_out(acc, 2, 0)
    return pl.pallas_call(
        _conv_state_kernel,
        grid=(n // wc,),
        in_specs=[spec, pl.BlockSpec((b, wc), lambda j: (0, col_off // wc + j))] + acc_specs,
        out_specs=spec,
        out_shape=jax.ShapeDtypeStruct(conv_t.shape, F32),
        input_output_aliases=aliases,
        compiler_params=_cparams(("parallel",)),
        name="conv_state",
    )(conv_t, proj, *acc_args)


def _gdn_decode_kernel(q_ref, k_ref, v_ref, z_ref, ba_ref, bq_ref, bk_ref, bv_ref, s_ref,
                       cwq_ref, cwk_ref, cwv_ref, alog_ref, dtb_ref, nw_ref, *rest):
    o_ref, s_out_ref = rest[-2:]
    hd = GDN_DIM
    tb = q_ref.shape[0]
    q = _silu(_conv_step(q_ref[...], bq_ref, cwq_ref))
    k = _silu(_conv_step(k_ref[...], bk_ref, cwk_ref))
    v = _silu(_conv_step(v_ref[...], bv_ref, cwv_ref))
    ba = ba_ref[...]
    beta = _sigmoid(ba)
    eg = jnp.exp(-jnp.exp(alog_ref[...]) * _softplus(ba + dtb_ref[...]))
    qn, kn = [], []
    for h in range(GDN_HEADS):
        qh, kh = q[:, h * hd:(h + 1) * hd], k[:, h * hd:(h + 1) * hd]
        qn.append(qh * lax.rsqrt(jnp.sum(qh * qh, axis=-1, keepdims=True) + NORM_EPS) * (hd ** -0.5))
        kn.append(kh * lax.rsqrt(jnp.sum(kh * kh, axis=-1, keepdims=True) + NORM_EPS))
    q_t = _transpose(jnp.concatenate(qn, axis=1))
    k_t = _transpose(jnp.concatenate(kn, axis=1))
    chains = [(b, h) for b in range(tb) for h in range(GDN_HEADS)]
    s_old = [s_ref[b, h] for b, h in chains]
    k_col = [k_t[h * hd:(h + 1) * hd, b:b + 1] for b, h in chains]
    q_col = [q_t[h * hd:(h + 1) * hd, b:b + 1] for b, h in chains]
    egb = [eg[b:b + 1, GDN_HEADS + h:GDN_HEADS + h + 1] for b, h in chains]
    ks = [jnp.sum(kc * s, axis=0, keepdims=True) for kc, s in zip(k_col, s_old)]
    delta = [beta[b:b + 1, h:h + 1] * (v[b:b + 1, h * hd:(h + 1) * hd] - e * x)
             for (b, h), e, x in zip(chains, egb, ks)]
    s_new = [e * s + kc * dl for e, s, kc, dl in zip(egb, s_old, k_col, delta)]
    for (b, h), sn in zip(chains, s_new):
        s_out_ref[b, h] = sn
    ys = [jnp.sum(qc * sn, axis=0, keepdims=True) for qc, sn in zip(q_col, s_new)]
    for h in range(GDN_HEADS):
        y = jnp.concatenate([ys[b * GDN_HEADS + h] for b in range(tb)], axis=0)
        y = _rms_rows(y, nw_ref[...]) * _silu(z_ref[:, h * hd:(h + 1) * hd])
        o_ref[:, h * hd:(h + 1) * hd] = y.astype(o_ref.dtype)


def gdn_decode(proj, state, conv_t, layer, acc, conv_w, alog_row, dtb_row, norm_w, tb=8):
    b = proj.shape[0]
    hd, od = GDN_DIM, GDN_OUT_DIM
    pcol = lambda off: pl.BlockSpec((tb, od), lambda i: (i, off // od))
    ccol = lambda part: pl.BlockSpec((None, CONV_WIDTH - 1, tb, od), lambda i: (layer, 0, i, part))
    wcol = lambda part: pl.BlockSpec((CONV_WIDTH, SUBLANES, od), lambda i: (0, 0, part))
    row = _resident((1, LANES))
    sspec = pl.BlockSpec((None, tb, GDN_HEADS, hd, hd), lambda i: (layer, i, 0, 0, 0))
    acc_specs, acc_args, aliases = _stacked_out(acc, 15, 1)
    assert tb == SUBLANES
    conv_w8 = _rep8_rows(conv_w)
    return pl.pallas_call(
        _gdn_decode_kernel,
        grid=(b // tb,),
        in_specs=[pcol(HYB_OFF_QKV), pcol(HYB_OFF_QKV + od), pcol(HYB_OFF_QKV + 2 * od),
                  pcol(HYB_OFF_Z), pl.BlockSpec((tb, LANES), lambda i: (i, HYB_OFF_BA // LANES)),
                  ccol(0), ccol(1), ccol(2), sspec,
                  wcol(0), wcol(1), wcol(2), row, row, _resident((SUBLANES, hd))] + acc_specs,
        out_specs=[pl.BlockSpec((tb, od), lambda i: (i, 0)), sspec],
        out_shape=[jax.ShapeDtypeStruct((b, od), BF16), jax.ShapeDtypeStruct(state.shape, F32)],
        input_output_aliases=aliases,
        compiler_params=_cparams(("parallel",)),
        name="gdn_decode",
    )(proj, proj, proj, proj, proj, conv_t, conv_t, conv_t, state,
      conv_w8, conv_w8, conv_w8, alog_row, dtb_row, _rep8(norm_w), *acc_args)


def _ret_decode_kernel(q_ref, k_ref, v_ref, gate_ref, s_ref, cos_ref, sin_ref, nw_ref, *rest):
    o_ref, s_out_ref = rest[-2:]
    q = _rotary(q_ref[...], cos_ref[...], sin_ref[...])
    k = _rotary(k_ref[...], cos_ref[...], sin_ref[...]) * (RET_KEY_DIM ** -0.5)
    q_t = _transpose(q)
    k_t = _transpose(k)
    tb = q.shape[0]
    kd, vd = RET_KEY_DIM, RET_VAL_DIM
    rows = []
    for b in range(tb):
        ys = []
        for h in range(RET_HEADS):
            gamma = math.exp(_ret_log_gamma(h))
            s = s_ref[b, h]
            s_new = s * gamma + k_t[h * kd:(h + 1) * kd, b:b + 1] * v_ref[b:b + 1, h * vd:(h + 1) * vd]
            s_out_ref[b, h] = s_new
            ys.append(jnp.sum(q_t[h * kd:(h + 1) * kd, b:b + 1] * s_new, axis=0, keepdims=True))
        rows.append(jnp.concatenate(ys, axis=1))
    y = jnp.concatenate(rows, axis=0)
    for h in range(RET_HEADS):
        yh = (_mul8(_group_norm_rows(y[:, h * vd:(h + 1) * vd]), nw_ref[:, h * vd:(h + 1) * vd])
              * _silu(gate_ref[:, h * vd:(h + 1) * vd]))
        o_ref[:, h * vd:(h + 1) * vd] = yh.astype(o_ref.dtype)


def ret_decode(proj, state, layer, acc, cos_f, sin_s, norm_w, tb=8):
    b = proj.shape[0]
    pcol = lambda width, off: pl.BlockSpec((tb, width), lambda i: (i, off // width))
    sspec = pl.BlockSpec((None, tb, RET_HEADS, RET_KEY_DIM, RET_VAL_DIM), lambda i: (layer, i, 0, 0, 0))
    acc_specs, acc_args, aliases = _stacked_out(acc, 8, 1)
    return pl.pallas_call(
        _ret_decode_kernel,
        grid=(b // tb,),
        in_specs=[pcol(RET_QK_DIM, HYB_OFF_RQ), pcol(RET_QK_DIM, HYB_OFF_RK), pcol(RET_OUT_DIM, HYB_OFF_RV),
                  pcol(RET_OUT_DIM, HYB_OFF_RG), sspec, _resident((1, RET_QK_DIM)), _resident((1, RET_QK_DIM)),
                  _resident((SUBLANES, RET_OUT_DIM))] + acc_specs,
        out_specs=[pl.BlockSpec((tb, RET_OUT_DIM), lambda i: (i, 0)), sspec],
        out_shape=[jax.ShapeDtypeStruct((b, RET_OUT_DIM), BF16), jax.ShapeDtypeStruct(state.shape, F32)],
        input_output_aliases=aliases,
        compiler_params=_cparams(("parallel",)),
        name="ret_decode",
    )(proj, proj, proj, proj, state, cos_f, sin_s, _rep8(norm_w), *acc_args)


def _ssd_decode_kernel(z_ref, x_ref, b_ref, c_ref, dt_ref, bx_ref, bb_ref, bc_ref, s_ref,
                       cwx_ref, cwb_ref, cwc_ref, cbx_ref, cbb_ref, cbc_ref,
                       dtb_ref, alog_ref, dskip_ref, nw_ref, *rest):
    o_ref, s_out_ref = rest[-2:]
    hd, gw, ns = SSM_HEAD_DIM, SSM_GROUP_COLS, SSM_STATE
    tb, width = x_ref.shape
    ng = width // gw
    first_head = pl.program_id(1) * (ng * SSM_HPG)
    x = _silu(_conv_step(x_ref[...], bx_ref, cwx_ref, cbx_ref[...]))
    bm = _silu(_conv_step(b_ref[...], bb_ref, cwb_ref, cbb_ref[...]))
    cm = _silu(_conv_step(c_ref[...], bc_ref, cwc_ref, cbc_ref[...]))
    dt = _softplus(dt_ref[...] + dtb_ref[...])
    spread = (_iota((LANES, width), 0) == first_head + _iota((LANES, width), 1) // hd).astype(BF16)
    per_head = jnp.concatenate([dt, -jnp.exp(alog_ref[...]), dskip_ref[...],
                                jnp.zeros((2 * SUBLANES - tb - 2, LANES), F32)], axis=0)
    ph_hi, ph_lo = _split(per_head)
    per_col = (jnp.dot(ph_hi, spread, preferred_element_type=F32)
               + jnp.dot(ph_lo, spread, preferred_element_type=F32))
    dt_x, a_x, dskip_x = per_col[0:tb], per_col[tb:tb + 1], per_col[tb + 1:tb + 2]
    eg_x = jnp.exp(dt_x * a_x)
    xdt_t = (x * dt_x).T
    row_id = _iota((tb, ns), 0)
    pairs = [(b, g) for b in range(tb) for g in range(ng)]
    outer = [_dot3(_split(xdt_t[g * gw:(g + 1) * gw, :]),
                   _split(jnp.where(row_id == b, bm[:, g * ns:(g + 1) * ns], 0.0))) for b, g in pairs]
    for (b, g), out in zip(pairs, outer):
        for hl in range(SSM_HPG):
            h = g * SSM_HPG + hl
            s_out_ref[b, h] = s_ref[b, h] * eg_x[b:b + 1, h * hd:h * hd + 1] + out[hl * hd:(hl + 1) * hd, :]
    ys = [_dot(cm[b:b + 1, g * ns:(g + 1) * ns],
               s_out_ref[b, g * SSM_HPG:(g + 1) * SSM_HPG].reshape(gw, ns), _NT) for b, g in pairs]
    y = jnp.concatenate([jnp.concatenate(ys[b * ng:(b + 1) * ng], axis=1) for b in range(tb)], axis=0)
    y = (y + dskip_x * x) * _silu(z_ref[...])
    for g in range(ng):
        cols = slice(g * gw, (g + 1) * gw)
        o_ref[:, cols] = _rms_rows(y[:, cols], nw_ref[:, cols]).astype(o_ref.dtype)


def ssd_decode(proj, state_t, conv_t, layer, acc, conv_w, conv_b, dtb_row, alog_row, dskip_row, norm_w, tb=8, ng=4):
    b = proj.shape[0]
    gw = ng * SSM_GROUP_COLS
    ns = ng * SSM_STATE
    assert tb == SUBLANES
    cw8, cb8, nw8 = _rep8_rows(conv_w), _rep8(conv_b), _rep8(norm_w)
    pcol = lambda width, off: pl.BlockSpec((tb, width), lambda i, g: (i, off // width + g))
    ccol = lambda width, off: pl.BlockSpec((None, CONV_WIDTH - 1, tb, width),
                                           lambda i, g: (layer, 0, i, off // width + g))
    wcol = lambda width, off: pl.BlockSpec((CONV_WIDTH, SUBLANES, width), lambda i, g: (0, 0, off // width + g))
    rcol = lambda width, off: pl.BlockSpec((SUBLANES, width), lambda i, g: (0, off // width + g))
    row = _resident((1, LANES))
    sspec = pl.BlockSpec((None, tb, ng * SSM_HPG, SSM_HEAD_DIM, SSM_STATE), lambda i, g: (layer, i, g, 0, 0))
    off_b, off_c = SSM_INNER, SSM_INNER + SSM_BC_DIM
    acc_specs, acc_args, aliases = _stacked_out(acc, 19, 1)
    return pl.pallas_call(
        _ssd_decode_kernel,
        grid=(b // tb, SSM_GROUPS // ng),
        in_specs=[pcol(gw, SSM_OFF_Z), pcol(gw, SSM_OFF_X), pcol(ns, SSM_OFF_B), pcol(ns, SSM_OFF_C),
                  pl.BlockSpec((tb, LANES), lambda i, g: (i, SSM_OFF_DT // LANES)),
                  ccol(gw, 0), ccol(ns, off_b), ccol(ns, off_c), sspec,
                  wcol(gw, 0), wcol(ns, off_b), wcol(ns, off_c),
                  rcol(gw, 0), rcol(ns, off_b), rcol(ns, off_c),
                  row, row, row, rcol(gw, 0)] + acc_specs,
        out_specs=[pl.BlockSpec((tb, gw), lambda i, g: (i, g)), sspec],
        out_shape=[jax.ShapeDtypeStruct((b, SSM_INNER), BF16), jax.ShapeDtypeStruct(state_t.shape, F32)],
        input_output_aliases=aliases,
        compiler_params=_cparams(("parallel", "parallel")),
        name="ssd_decode",
    )(proj, proj, proj, proj, proj, conv_t, conv_t, conv_t, state_t,
      cw8, cw8, cw8, cb8, cb8, cb8, dtb_row, alog_row, dskip_row, nw8, *acc_args)


def _lane_row(vals, offset):
    return jnp.zeros((1, LANES), F32).at[0, offset:offset + vals.shape[0]].set(vals.astype(F32))


def _prep_hyb_w_in(w):
    sizes = (GDN_QKV_DIM, GDN_OUT_DIM, GDN_HEADS, GDN_HEADS, RET_QK_DIM, RET_QK_DIM, RET_OUT_DIM, RET_OUT_DIM)
    offs = [0]
    for s in sizes:
        offs.append(offs[-1] + s)
    part = lambda i: w[..., offs[i]:offs[i + 1]]
    cols = [part(0), part(1), part(4), part(5), part(6), part(7), part(2), part(3)]
    pad = jnp.zeros(w.shape[:-1] + (HYB_N - offs[-1],), w.dtype)
    return jnp.concatenate(cols + [pad], axis=-1).astype(BF16)


def _prep_ssm_w_in(w):
    pad = jnp.zeros(w.shape[:-1] + (SSM_N - w.shape[-1],), w.dtype)
    return jnp.concatenate([w, pad], axis=-1).astype(BF16)


def _rope_tables(pos):
    half = RET_KEY_DIM // 2
    inv_freq = ROPE_BASE ** (-jnp.arange(half, dtype=F32) / half)
    ang = pos.astype(F32)[:, None] * inv_freq[None, :]
    cos, sin = jnp.cos(ang), jnp.sin(ang)
    cos_f = jnp.tile(jnp.concatenate([cos, cos], axis=1), (1, RET_HEADS))
    sin_s = jnp.tile(jnp.concatenate([-sin, sin], axis=1), (1, RET_HEADS))
    return cos_f, sin_s


def _trunk(x, pos, states, params, prompt):
    bsz, l, d = x.shape
    m = bsz * l
    tm = 512 if m % 512 == 0 else m
    x2 = x.reshape(m, d)
    cos_f, sin_s = _rope_tables(pos)
    new_gdn, new_gdn_conv, new_ret, new_ssm, new_ssm_conv = [], [], [], [], []
    if not prompt:
        s_gdn, c_gdn, s_ret, s_ssm, c_ssm = states
        c_gdn_t, c_ssm_t = jnp.swapaxes(c_gdn, 1, 2), jnp.swapaxes(c_ssm, 1, 2)
        s_ssm_t = jnp.swapaxes(s_ssm, 3, 4)
        acc_gdn, acc_gdn_conv, acc_ret, acc_ssm, acc_ssm_conv = (
            jnp.zeros_like(a) for a in (s_gdn, c_gdn_t, s_ret, s_ssm_t, c_ssm_t))
    for layer in range(DEPTH):
        i = layer // 2
        p = params
        if layer % 2 == 0:
            proj = norm_matmul(x2, p["norm_mix8"], layer, p["w_in_hyb"], i, tm)
            alog_row = _lane_row(p["gdn_a_log"][i], GDN_HEADS)
            dtb_row = _lane_row(p["gdn_dt_bias"][i], GDN_HEADS)
            if prompt:
                proj3 = proj.reshape(bsz, l, HYB_N)
                o_a, sg = gdn_prompt(proj3, p["gdn_conv_w"][i], alog_row, dtb_row, p["gdn_norm_w"][i])
                o_b, sr = ret_prompt(proj3, cos_f, sin_s, p["ret_norm_w"][i])
                cg = proj3[:, l - (CONV_WIDTH - 1):, :GDN_QKV_DIM]
                o_a, o_b = o_a.reshape(m, GDN_OUT_DIM), o_b.reshape(m, RET_OUT_DIM)
                new_gdn.append(sg)
                new_gdn_conv.append(cg)
                new_ret.append(sr)
            else:
                o_a, acc_gdn = gdn_decode(proj, s_gdn, c_gdn_t, i, acc_gdn, p["gdn_conv_w"][i], alog_row,
                                          dtb_row, p["gdn_norm_w"][i])
                acc_gdn_conv = conv_state_update(c_gdn_t, proj, HYB_OFF_QKV, i, acc_gdn_conv, GDN_OUT_DIM)
                o_b, acc_ret = ret_decode(proj, s_ret, i, acc_ret, cos_f, sin_s, p["ret_norm_w"][i])
            acts, w_out = [o_a, o_b], p["w_out_hyb"]
        else:
            proj = norm_matmul(x2, p["norm_mix8"], layer, p["w_in_ssm"], i, tm)
            dtb_row = _lane_row(p["ssm_dt_bias"][i], 0)
            alog_row = _lane_row(p["ssm_a_log"][i], 0)
            dskip_row = _lane_row(p["ssm_d"][i], 0)
            if prompt:
                proj3 = proj.reshape(bsz, l, SSM_N)
                y, ss = ssd_prompt(proj3, p["ssm_conv_w"][i], p["ssm_conv_b"][i], dtb_row, alog_row,
                                   dskip_row, p["ssm_norm_w"][i])
                cs = proj3[:, l - (CONV_WIDTH - 1):, SSM_OFF_X:SSM_OFF_X + SSM_CONV_DIM]
                y = y.reshape(m, SSM_INNER)
                new_ssm.append(ss)
                new_ssm_conv.append(cs)
            else:
                y, acc_ssm = ssd_decode(proj, s_ssm_t, c_ssm_t, i, acc_ssm, p["ssm_conv_w"][i],
                                        p["ssm_conv_b"][i], dtb_row, alog_row, dskip_row, p["ssm_norm_w"][i])
                acc_ssm_conv = conv_state_update(c_ssm_t, proj, SSM_OFF_X, i, acc_ssm_conv, D_MODEL)
            acts, w_out = [y], p["w_out_ssm"]
        x2 = post_mixer(acts, w_out, i, x2, p["norm_mlp8"], layer, p["mlp_w1"], p["mlp_w2"],
                        p["norm_final8"], layer == DEPTH - 1, tm)
    y_out = x2.reshape(bsz, l, d)
    if prompt:
        return (y_out, jnp.stack(new_gdn), jnp.stack(new_gdn_conv), jnp.stack(new_ret),
                jnp.swapaxes(jnp.stack(new_ssm), 3, 4), jnp.stack(new_ssm_conv))
    return (y_out, acc_gdn, jnp.swapaxes(acc_gdn_conv, 1, 2), acc_ret,
            jnp.swapaxes(acc_ssm, 3, 4), jnp.swapaxes(acc_ssm_conv, 1, 2))


def kernel(x_prompt, x_sample, state_gdn, state_gdn_conv, state_ret, state_ssm, state_ssm_conv, norm_mix, norm_mlp, norm_final, w_in_hyb, gdn_conv_w, gdn_a_log, gdn_dt_bias, gdn_norm_w, ret_norm_w, w_out_hyb, w_in_ssm, ssm_conv_w, ssm_conv_b, ssm_dt_bias, ssm_a_log, ssm_d, ssm_norm_w, w_out_ssm, mlp_w1, mlp_w2):
    params = dict(
        norm_mix8=_rep8_rows(norm_mix), norm_mlp8=_rep8_rows(norm_mlp), norm_final8=_rep8(norm_final),
        w_in_hyb=_prep_hyb_w_in(w_in_hyb),
        gdn_conv_w=gdn_conv_w, gdn_a_log=gdn_a_log, gdn_dt_bias=gdn_dt_bias, gdn_norm_w=gdn_norm_w,
        ret_norm_w=ret_norm_w, w_out_hyb=w_out_hyb.astype(BF16),
        w_in_ssm=_prep_ssm_w_in(w_in_ssm),
        ssm_conv_w=ssm_conv_w, ssm_conv_b=ssm_conv_b, ssm_dt_bias=ssm_dt_bias, ssm_a_log=ssm_a_log,
        ssm_d=ssm_d, ssm_norm_w=ssm_norm_w, w_out_ssm=w_out_ssm.astype(BF16),
        mlp_w1=mlp_w1.astype(BF16), mlp_w2=mlp_w2.astype(BF16))
    lp, ls = x_prompt.shape[1], x_sample.shape[1]
    pos_prompt = jnp.arange(lp, dtype=jnp.int32)
    pos_sample = PAST_LEN + jnp.arange(ls, dtype=jnp.int32)
    y_p, p_gdn, p_gdn_conv, p_ret, p_ssm, p_ssm_conv = _trunk(
        x_prompt, pos_prompt, (None,) * 5, params, prompt=True)
    y_s, s_gdn, s_gdn_conv, s_ret, s_ssm, s_ssm_conv = _trunk(
        x_sample, pos_sample, (state_gdn, state_gdn_conv, state_ret, state_ssm, state_ssm_conv),
        params, prompt=False)
    return (y_p, y_s, p_gdn, p_gdn_conv, p_ret, p_ssm, p_ssm_conv,
            s_gdn, s_gdn_conv, s_ret, s_ssm, s_ssm_conv)
```

```python
import functools
import math

import jax
import jax.numpy as jnp
from jax import lax
from jax.experimental import pallas as pl
from jax.experimental.pallas import tpu as pltpu

F32 = jnp.float32
BF16 = jnp.bfloat16

D_MODEL = 1024
DEPTH = 4
CONV_WIDTH = 4
NORM_EPS = 1e-6
PAST_LEN = 16384

GDN_HEADS = 4
GDN_DIM = 128
GDN_QKV_DIM = 3 * GDN_HEADS * GDN_DIM
GDN_OUT_DIM = GDN_HEADS * GDN_DIM

RET_HEADS = 4
RET_KEY_DIM = 64
RET_VAL_DIM = 128
RET_QK_DIM = RET_HEADS * RET_KEY_DIM
RET_OUT_DIM = RET_HEADS * RET_VAL_DIM
ROPE_BASE = 10000.0

SSM_INNER = 2 * D_MODEL
SSM_HEAD_DIM = 64
SSM_HEADS = SSM_INNER // SSM_HEAD_DIM
SSM_GROUPS = 4
SSM_HPG = SSM_HEADS // SSM_GROUPS
SSM_STATE = 128
SSM_GROUP_COLS = SSM_HPG * SSM_HEAD_DIM
SSM_BC_DIM = SSM_GROUPS * SSM_STATE
SSM_CONV_DIM = SSM_INNER + 2 * SSM_BC_DIM
MLP_HIDDEN = 4 * D_MODEL

LANES = 128
SUBLANES = 8

HYB_OFF_QKV = 0
HYB_OFF_Z = GDN_QKV_DIM
HYB_OFF_RQ = HYB_OFF_Z + GDN_OUT_DIM
HYB_OFF_RK = HYB_OFF_RQ + RET_QK_DIM
HYB_OFF_RV = HYB_OFF_RK + RET_QK_DIM
HYB_OFF_RG = HYB_OFF_RV + RET_OUT_DIM
HYB_OFF_BA = HYB_OFF_RG + RET_OUT_DIM
HYB_N = 3840
SSM_OFF_Z = 0
SSM_OFF_X = SSM_INNER
SSM_OFF_B = SSM_OFF_X + SSM_INNER
SSM_OFF_C = SSM_OFF_B + SSM_BC_DIM
SSM_OFF_DT = SSM_OFF_C + SSM_BC_DIM
SSM_N = 5376
PROJ_TN = 768

VMEM_LIMIT = 56 * 1024 * 1024

_NN = (((1,), (0,)), ((), ()))
_NT = (((1,), (1,)), ((), ()))
_TN = (((0,), (0,)), ((), ()))


def _dot(a, b, dims=_NN):
    return lax.dot_general(a.astype(BF16), b.astype(BF16), dims, preferred_element_type=F32)


def _dot_f32(a, b, dims=_NN):
    return lax.dot_general(a, b, dims, precision=lax.Precision.HIGHEST, preferred_element_type=F32)


def _sigmoid(x):
    return 0.5 + 0.5 * jnp.tanh(0.5 * x)


def _silu(x):
    t = 0.5 * x
    return t + t * jnp.tanh(t)


def _softplus(x):
    return jnp.maximum(x, 0.0) + jnp.log(1.0 + jnp.exp(-jnp.abs(x)))


def _iota(shape, dim):
    return lax.broadcasted_iota(jnp.int32, shape, dim)


def _eye(n):
    return (_iota((n, n), 0) == _iota((n, n), 1)).astype(F32)


def _transpose(x):
    return x.T


def _cparams(sem):
    return pltpu.CompilerParams(dimension_semantics=sem, vmem_limit_bytes=VMEM_LIMIT)


def _resident(shape):
    nd = len(shape)
    return pl.BlockSpec(shape, lambda *_: (0,) * nd, pipeline_mode=pl.Buffered(1))


def _resident_layer(shape, layer, rows=None, row_block=0):
    rows = shape[1] if rows is None else rows
    return pl.BlockSpec((None, rows, shape[2]), lambda *_: (layer, row_block, 0), pipeline_mode=pl.Buffered(1))


def _rep8(v):
    v = v.reshape(1, -1)
    return jnp.broadcast_to(v, (SUBLANES, v.shape[1]))


def _rep8_rows(w):
    return jnp.broadcast_to(w[:, None, :], (w.shape[0], SUBLANES, w.shape[1]))


def _mul8(x, r8):
    r, n = x.shape
    return (x.reshape(r // SUBLANES, SUBLANES, n) * r8).reshape(r, n)


def _add8(x, r8):
    r, n = x.shape
    return (x.reshape(r // SUBLANES, SUBLANES, n) + r8).reshape(r, n)


def _rms_rows(x, gain8):
    return _mul8(x * lax.rsqrt(jnp.mean(x * x, axis=-1, keepdims=True) + NORM_EPS), gain8)


def _norm_matmul_kernel(x_ref, g_ref, w_ref, o_ref, *, tn):
    xn = _rms_rows(x_ref[...], g_ref[...]).astype(BF16)
    for j in range(w_ref.shape[1] // tn):
        o_ref[:, j * tn:(j + 1) * tn] = jnp.dot(xn, w_ref[:, j * tn:(j + 1) * tn],
                                                preferred_element_type=F32)


def norm_matmul(x, gains8, layer, w, w_layer, tm):
    m, d = x.shape
    n = w.shape[2]
    return pl.pallas_call(
        functools.partial(_norm_matmul_kernel, tn=PROJ_TN),
        grid=(m // tm,),
        in_specs=[pl.BlockSpec((tm, d), lambda i: (i, 0)), _resident_layer(gains8.shape, layer),
                  _resident_layer(w.shape, w_layer)],
        out_specs=pl.BlockSpec((tm, n), lambda i: (i, 0)),
        out_shape=jax.ShapeDtypeStruct((m, n), F32),
        compiler_params=_cparams(("parallel",)),
        name="norm_matmul",
    )(x, gains8, w)


def _post_mixer_kernel(*refs, n_in, tf, final_norm):
    a_refs, w_refs = refs[:n_in], refs[n_in:2 * n_in]
    r_ref, g_ref, w1_ref, w2_ref, gf_ref, o_ref = refs[2 * n_in:]
    x = r_ref[...]
    for a_ref, w_ref in zip(a_refs, w_refs):
        x = x + jnp.dot(a_ref[...], w_ref[...], preferred_element_type=F32)
    o_ref[...] = x
    x = o_ref[...]
    xn = _rms_rows(x, g_ref[...]).astype(BF16)
    acc = x
    for j in range(w1_ref.shape[1] // tf):
        h = jnp.dot(xn, w1_ref[:, j * tf:(j + 1) * tf], preferred_element_type=F32)
        h = jnp.maximum(h, 0.0)
        acc = acc + jnp.dot((h * h).astype(BF16), w2_ref[j * tf:(j + 1) * tf, :],
                            preferred_element_type=F32)
    if final_norm:
        acc = _rms_rows(acc, gf_ref[...])
    o_ref[...] = acc


def post_mixer(acts, w_out, w_layer, res, gains8, layer, w1, w2, final8, final_norm, tm):
    m, d = res.shape
    n_in = len(acts)
    k_rows = acts[0].shape[1]
    assert all(a.shape[1] == k_rows for a in acts) and n_in * k_rows == w_out.shape[1]
    in_specs = ([pl.BlockSpec((tm, k_rows), lambda i: (i, 0)) for _ in acts]
                + [_resident_layer(w_out.shape, w_layer, k_rows, j) for j in range(n_in)]
                + [pl.BlockSpec((tm, d), lambda i: (i, 0)), _resident_layer(gains8.shape, layer),
                   _resident_layer(w1.shape, layer), _resident_layer(w2.shape, layer), _resident((SUBLANES, d))])
    return pl.pallas_call(
        functools.partial(_post_mixer_kernel, n_in=n_in, tf=1024, final_norm=final_norm),
        grid=(m // tm,),
        in_specs=in_specs,
        out_specs=pl.BlockSpec((tm, d), lambda i: (i, 0)),
        out_shape=jax.ShapeDtypeStruct((m, d), F32),
        compiler_params=_cparams(("parallel",)),
        name="post_mixer",
    )(*acts, *([w_out] * n_in), res, gains8, w1, w2, final8)


def _causal_conv_chunk(x, tail, w8, bias8=None):
    c, n = x.shape
    g = c // SUBLANES
    x3 = x.reshape(g, SUBLANES, n)
    acc = x3 * w8[CONV_WIDTH - 1]
    row = _iota((SUBLANES, n), 0)
    for k in range(1, CONV_WIDTH):
        r = pltpu.roll(x3, k, 1)
        prev = jnp.concatenate([pltpu.roll(tail, k, 0)[None], r[:g - 1]], axis=0)
        acc = acc + jnp.where(row < k, prev, r) * w8[CONV_WIDTH - 1 - k]
    if bias8 is not None:
        acc = acc + bias8
    return acc.reshape(c, n)


def _cumsum_rows(g):
    c = g.shape[0]
    tri = (_iota((c, c), 0) >= _iota((c, c), 1)).astype(F32)
    return _dot_f32(tri, g)


def _decay_matrix(gc_col, gc_row):
    c = gc_col.shape[0]
    causal = _iota((c, c), 0) >= _iota((c, c), 1)
    return jnp.where(causal, jnp.exp(gc_col - gc_row), 0.0)


def _split(x):
    hi = x.astype(BF16)
    return hi, (x - hi.astype(F32)).astype(BF16)


def _dot3(a, b, dims=_NN):
    (ah, al), (bh, bl) = a, b
    d = lambda p, q: lax.dot_general(p, q, dims, preferred_element_type=F32)
    return d(ah, bh) + (d(ah, bl) + d(al, bh))


def _unit_lower_inverses(n_list):
    c = n_list[0].shape[0]
    ri, ci = _iota((c, c), 0), _iota((c, c), 1)
    eye = (ri == ci).astype(F32)
    blk = 16
    same = (ri // blk) == (ci // blk)
    p = [jnp.where(same, -n, 0.0).astype(BF16) for n in n_list]
    inv = [eye + x for x in p]
    for _ in range(3):
        p = [_dot(x, x).astype(BF16) for x in p]
        inv = [i + _dot(i, x) for i, x in zip(inv, p)]
    while blk < c:
        pair = ((ri // (2 * blk)) == (ci // (2 * blk))) & ((ri // blk) != (ci // blk))
        inv_b = [i.astype(BF16) for i in inv]
        t = [_dot(jnp.where(pair, n, 0.0), i) for n, i in zip(n_list, inv_b)]
        inv = [i - _dot(i_b, x) for i, i_b, x in zip(inv, inv_b, t)]
        blk *= 2
    return inv


def _gdn_prompt_kernel(qkv_ref, z_ref, ba_ref, cw_ref, alog_ref, dtb_ref, nw_ref,
                       o_ref, s_out_ref, s_ref, tail_ref):
    c_idx = pl.program_id(1)

    @pl.when(c_idx == 0)
    def _():
        s_ref[...] = jnp.zeros_like(s_ref)
        tail_ref[...] = jnp.zeros_like(tail_ref)

    nb, c = qkv_ref.shape[0], qkv_ref.shape[1]
    hd = GDN_DIM
    ri, ci = _iota((c, c), 0), _iota((c, c), 1)
    strict = ri > ci
    chains = [(bb, h) for bb in range(nb) for h in range(GDN_HEADS)]

    conv, beta, gcum, gcum_t = [], [], [], []
    for bb in range(nb):
        x = qkv_ref[bb]
        conv.append(_silu(_causal_conv_chunk(x, tail_ref[bb], cw_ref[...])))
        tail_ref[bb] = x[c - SUBLANES:c]
        ba = ba_ref[bb]
        beta.append(_sigmoid(ba))
        gc = _cumsum_rows(-jnp.exp(alog_ref[...]) * _softplus(ba + dtb_ref[...]))
        gcum.append(gc)
        gcum_t.append(_transpose(gc))

    q, k, v, b_col, gc_col, g_end, decay = [], [], [], [], [], [], []
    for bb, h in chains:
        qh = conv[bb][:, h * hd:(h + 1) * hd]
        kh = conv[bb][:, GDN_OUT_DIM + h * hd:GDN_OUT_DIM + (h + 1) * hd]
        q.append(qh * lax.rsqrt(jnp.sum(qh * qh, axis=-1, keepdims=True) + NORM_EPS) * (hd ** -0.5))
        k.append(kh * lax.rsqrt(jnp.sum(kh * kh, axis=-1, keepdims=True) + NORM_EPS))
        v.append(conv[bb][:, 2 * GDN_OUT_DIM + h * hd:2 * GDN_OUT_DIM + (h + 1) * hd])
        b_col.append(beta[bb][:, h:h + 1])
        gc_col.append(gcum[bb][:, GDN_HEADS + h:GDN_HEADS + h + 1])
        g_end.append(gcum[bb][c - 1:c, GDN_HEADS + h:GDN_HEADS + h + 1])
        decay.append(_decay_matrix(gc_col[-1], gcum_t[bb][GDN_HEADS + h:GDN_HEADS + h + 1, :]))
    kk = [_dot(x, x, _NT) for x in k]
    qk = [_dot(x, y, _NT) * d for x, y, d in zip(q, k, decay)]
    t_inv = _unit_lower_inverses([jnp.where(strict, b * x * d, 0.0) for b, x, d in zip(b_col, kk, decay)])
    rhs = [jnp.concatenate([vv * b, kx * (b * jnp.exp(g))], axis=1)
           for vv, kx, b, g in zip(v, k, b_col, gc_col)]
    sol = [_dot(t, r) for t, r in zip(t_inv, rhs)]
    s_old = [s_ref[bb, h] for bb, h in chains]
    delta = [x[:, :hd] - _dot(x[:, hd:], s) for x, s in zip(sol, s_old)]
    y = [_dot(x * jnp.exp(g), s) + _dot(a, dl)
         for x, g, s, a, dl in zip(q, gc_col, s_old, qk, delta)]
    s_new = [s * jnp.exp(ge) + _dot(kx * jnp.exp(ge - g), dl, _TN)
             for s, ge, kx, g, dl in zip(s_old, g_end, k, gc_col, delta)]
    for (bb, h), sn, yy in zip(chains, s_new, y):
        s_ref[bb, h] = sn
        out = _rms_rows(yy, nw_ref[...]) * _silu(z_ref[bb, :, h * hd:(h + 1) * hd])
        o_ref[bb, :, h * hd:(h + 1) * hd] = out.astype(o_ref.dtype)

    @pl.when(c_idx == pl.num_programs(1) - 1)
    def _():
        s_out_ref[...] = s_ref[...]


def gdn_prompt(proj, conv_w, alog_row, dtb_row, norm_w, chunk=64, nb=4):
    b, l, _ = proj.shape
    nc = l // chunk
    col = lambda width, off: pl.BlockSpec((nb, chunk, width), lambda i, j: (i, j, off // width))
    return pl.pallas_call(
        _gdn_prompt_kernel,
        grid=(b // nb, nc),
        in_specs=[col(GDN_QKV_DIM, HYB_OFF_QKV), col(GDN_OUT_DIM, HYB_OFF_Z), col(LANES, HYB_OFF_BA),
                  _resident((CONV_WIDTH, SUBLANES, GDN_QKV_DIM)), _resident((1, LANES)), _resident((1, LANES)),
                  _resident((SUBLANES, GDN_DIM))],
        out_specs=[pl.BlockSpec((nb, chunk, GDN_OUT_DIM), lambda i, j: (i, j, 0)),
                   pl.BlockSpec((nb, GDN_HEADS, GDN_DIM, GDN_DIM), lambda i, j: (i, 0, 0, 0))],
        out_shape=[jax.ShapeDtypeStruct((b, l, GDN_OUT_DIM), BF16),
                   jax.ShapeDtypeStruct((b, GDN_HEADS, GDN_DIM, GDN_DIM), F32)],
        scratch_shapes=[pltpu.VMEM((nb, GDN_HEADS, GDN_DIM, GDN_DIM), F32),
                        pltpu.VMEM((nb, SUBLANES, GDN_QKV_DIM), F32)],
        compiler_params=_cparams(("parallel", "arbitrary")),
        name="gdn_prompt",
    )(proj, proj, proj, _rep8_rows(conv_w), alog_row, dtb_row, _rep8(norm_w))


def _ret_log_gamma(h):
    return math.log(1.0 - 2.0 ** (-5.0 - h))


def _rotary(t, cos_f, sin_s):
    n = t.shape[1]
    half = RET_KEY_DIM // 2
    first = (_iota(t.shape, 1) % RET_KEY_DIM) < half
    swapped = jnp.where(first, pltpu.roll(t, n - half, 1), pltpu.roll(t, half, 1))
    return t * cos_f + swapped * sin_s


def _group_norm_rows(x):
    xc = x - jnp.mean(x, axis=-1, keepdims=True)
    return xc * lax.rsqrt(jnp.mean(xc * xc, axis=-1, keepdims=True) + NORM_EPS)


def _ret_prompt_kernel(q_ref, k_ref, v_ref, gate_ref, cos_ref, sin_ref, nw_ref,
                       o_ref, s_out_ref, s_ref):
    c_idx = pl.program_id(1)

    @pl.when(c_idx == 0)
    def _():
        s_ref[...] = jnp.zeros_like(s_ref)

    nb, c = q_ref.shape[0], q_ref.shape[1]
    kd, vd = RET_KEY_DIM, RET_VAL_DIM
    ri, ci = _iota((c, c), 0), _iota((c, c), 1)
    causal = ri >= ci
    pos_col = _iota((c, 1), 0).astype(F32)
    lgs = [_ret_log_gamma(h) for h in range(RET_HEADS)]
    decay = [jnp.where(causal, jnp.exp((ri - ci).astype(F32) * lg), 0.0) for lg in lgs]
    grow = [jnp.exp((pos_col + 1.0) * lg) for lg in lgs]
    gtail = [jnp.exp((c - 1.0 - pos_col) * lg) for lg in lgs]
    chains = [(bb, h) for bb in range(nb) for h in range(RET_HEADS)]
    q = [_rotary(q_ref[bb], cos_ref[...], sin_ref[...]) for bb in range(nb)]
    k = [_rotary(k_ref[bb], cos_ref[...], sin_ref[...]) * (kd ** -0.5) for bb in range(nb)]
    qh = [q[bb][:, h * kd:(h + 1) * kd] for bb, h in chains]
    kh = [k[bb][:, h * kd:(h + 1) * kd] for bb, h in chains]
    vh = [v_ref[bb, :, h * vd:(h + 1) * vd] for bb, h in chains]
    s_old = [s_ref[bb, h] for bb, h in chains]
    att = [_dot(a, b, _NT) * decay[h] for a, b, (_, h) in zip(qh, kh, chains)]
    y = [_dot(a, v) + _dot(qq, s) * grow[h] for a, v, qq, s, (_, h) in zip(att, vh, qh, s_old, chains)]
    s_new = [s * math.exp(c * lgs[h]) + _dot(kk * gtail[h], v, _TN)
             for s, kk, v, (_, h) in zip(s_old, kh, vh, chains)]
    for (bb, h), yy, sn in zip(chains, y, s_new):
        s_ref[bb, h] = sn
        out = (_mul8(_group_norm_rows(yy), nw_ref[:, h * vd:(h + 1) * vd])
               * _silu(gate_ref[bb, :, h * vd:(h + 1) * vd]))
        o_ref[bb, :, h * vd:(h + 1) * vd] = out.astype(o_ref.dtype)

    @pl.when(c_idx == pl.num_programs(1) - 1)
    def _():
        s_out_ref[...] = s_ref[...]


def ret_prompt(proj, cos_f, sin_s, norm_w, chunk=128, nb=2):
    b, l, _ = proj.shape
    nc = l // chunk
    col = lambda width, off: pl.BlockSpec((nb, chunk, width), lambda i, j: (i, j, off // width))
    tab = pl.BlockSpec((chunk, RET_QK_DIM), lambda i, j: (j, 0))
    return pl.pallas_call(
        _ret_prompt_kernel,
        grid=(b // nb, nc),
        in_specs=[col(RET_QK_DIM, HYB_OFF_RQ), col(RET_QK_DIM, HYB_OFF_RK), col(RET_OUT_DIM, HYB_OFF_RV),
                  col(RET_OUT_DIM, HYB_OFF_RG), tab, tab, _resident((SUBLANES, RET_OUT_DIM))],
        out_specs=[pl.BlockSpec((nb, chunk, RET_OUT_DIM), lambda i, j: (i, j, 0)),
                   pl.BlockSpec((nb, RET_HEADS, RET_KEY_DIM, RET_VAL_DIM), lambda i, j: (i, 0, 0, 0))],
        out_shape=[jax.ShapeDtypeStruct((b, l, RET_OUT_DIM), BF16),
                   jax.ShapeDtypeStruct((b, RET_HEADS, RET_KEY_DIM, RET_VAL_DIM), F32)],
        scratch_shapes=[pltpu.VMEM((nb, RET_HEADS, RET_KEY_DIM, RET_VAL_DIM), F32)],
        compiler_params=_cparams(("parallel", "arbitrary")),
        name="ret_prompt",
    )(proj, proj, proj, proj, cos_f, sin_s, _rep8(norm_w))


def _split3(x):
    a = x.astype(BF16)
    r = x - a.astype(F32)
    b = r.astype(BF16)
    return a, b, (r - b.astype(F32)).astype(BF16)


def _spread(x, sel):
    return sum(jnp.dot(p, sel, preferred_element_type=F32) for p in _split3(x))


def _ssd_prompt_kernel(z_ref, x_ref, b_ref, c_ref, dt_ref, cw_ref, cb_ref, dtb_ref, alog_ref, dskip_ref,
                       nw_ref, sel_hd_ref, sel_c_ref, o_ref, s_out_ref, s_ref, tx_ref, tb_ref, tc_ref):
    c_idx = pl.program_id(1)

    @pl.when(c_idx == 0)
    def _():
        s_ref[...] = jnp.zeros_like(s_ref)
        tx_ref[...] = jnp.zeros_like(tx_ref)
        tb_ref[...] = jnp.zeros_like(tb_ref)
        tc_ref[...] = jnp.zeros_like(tc_ref)

    c = x_ref.shape[0]
    hd, gw, ns = SSM_HEAD_DIM, SSM_GROUP_COLS, SSM_STATE
    off_b, off_c = SSM_INNER, SSM_INNER + SSM_BC_DIM
    groups = range(SSM_GROUPS)

    br, cr = b_ref[...], c_ref[...]
    bm = _silu(_causal_conv_chunk(br, tb_ref[...], cw_ref[:, :, off_b:off_c], cb_ref[:, off_b:off_c]))
    cm = _silu(_causal_conv_chunk(cr, tc_ref[...], cw_ref[:, :, off_c:], cb_ref[:, off_c:]))
    tb_ref[...] = br[c - SUBLANES:c]
    tc_ref[...] = cr[c - SUBLANES:c]
    bg = [bm[:, g * ns:(g + 1) * ns] for g in groups]
    cg = [cm[:, g * ns:(g + 1) * ns] for g in groups]
    s_old = [s_ref[g] for g in groups]
    scores = [_dot(cc, bb, _NT) for cc, bb in zip(cg, bg)]
    y_inter = jnp.concatenate([_dot(cc, s) for cc, s in zip(cg, s_old)], axis=1)

    dt = _softplus(dt_ref[...] + dtb_ref[...])
    gcum = _cumsum_rows(dt * (-jnp.exp(alog_ref[...])))
    gcum_t = _transpose(gcum)
    per_col = _spread(jnp.concatenate([dt, gcum], axis=0), sel_hd_ref[...])
    dt_x, gc_x = per_col[:c], per_col[c:]
    gc_colb = _spread(gcum, sel_c_ref[...])
    ge8 = jnp.broadcast_to(gc_x[c - 1:c, :], (SUBLANES, SSM_INNER))

    xr = x_ref[...]
    x = _silu(_causal_conv_chunk(xr, tx_ref[...], cw_ref[:, :, :off_b], cb_ref[:, :off_b]))
    tx_ref[...] = xr[c - SUBLANES:c]

    xdt = x * dt_x
    xdec = xdt * jnp.exp(_add8(-gc_x, ge8))
    att =[scores[h // SSM_HPG] * _decay_matrix(gc_colb[:, h * c:(h + 1) * c], gcum_t[h:h + 1, :])
           for h in range(SSM_HEADS)]
    y_intra = jnp.concatenate([_dot(a, xdt[:, h * hd:(h + 1) * hd]) for h, a in enumerate(att)], axis=1)
    for g in groups:
        cols = slice(g * gw, (g + 1) * gw)
        s_ref[g] = _mul8(s_old[g], jnp.exp(ge8[:, cols])) + _dot(bg[g], xdec[:, cols], _TN)
    y = (y_intra + y_inter * jnp.exp(gc_x) + _mul8(x, dskip_ref[...])) * _silu(z_ref[...])
    for g in groups:
        cols = slice(g * gw, (g + 1) * gw)
        o_ref[:, cols] = _rms_rows(y[:, cols], nw_ref[:, cols]).astype(o_ref.dtype)

    @pl.when(c_idx == pl.num_programs(1) - 1)
    def _():
        for g in range(SSM_GROUPS):
            s_out_ref[g * SSM_HPG:(g + 1) * SSM_HPG] = _transpose(s_ref[g]).reshape(SSM_HPG, hd, ns)


def ssd_prompt(proj, conv_w, conv_b, dtb_row, alog_row, dskip_row, norm_w, chunk=128):
    b, l, _ = proj.shape
    assert chunk == LANES
    nc = l // chunk
    col = lambda width, off: pl.BlockSpec((None, chunk, width), lambda i, j: (i, j, off // width))
    row = _resident((1, LANES))
    lane = jnp.arange(LANES, dtype=jnp.int32)[:, None]
    sel_hd = (lane == jnp.arange(SSM_INNER, dtype=jnp.int32)[None, :] // SSM_HEAD_DIM).astype(BF16)
    sel_c = (lane == jnp.arange(SSM_HEADS * chunk, dtype=jnp.int32)[None, :] // chunk).astype(BF16)
    dskip8 = _rep8(jnp.repeat(dskip_row[0, :SSM_HEADS], SSM_HEAD_DIM))
    return pl.pallas_call(
        _ssd_prompt_kernel,
        grid=(b, nc),
        in_specs=[col(SSM_INNER, SSM_OFF_Z), col(SSM_INNER, SSM_OFF_X), col(SSM_BC_DIM, SSM_OFF_B),
                  col(SSM_BC_DIM, SSM_OFF_C), col(LANES, SSM_OFF_DT),
                  _resident((CONV_WIDTH, SUBLANES, SSM_CONV_DIM)), _resident((SUBLANES, SSM_CONV_DIM)),
                  row, row, _resident((SUBLANES, SSM_INNER)), _resident((SUBLANES, SSM_INNER)),
                  _resident(sel_hd.shape), _resident(sel_c.shape)],
        out_specs=[pl.BlockSpec((None, chunk, SSM_INNER), lambda i, j: (i, j, 0)),
                   pl.BlockSpec((None, SSM_HEADS, SSM_HEAD_DIM, SSM_STATE), lambda i, j: (i, 0, 0, 0))],
        out_shape=[jax.ShapeDtypeStruct((b, l, SSM_INNER), BF16),
                   jax.ShapeDtypeStruct((b, SSM_HEADS, SSM_HEAD_DIM, SSM_STATE), F32)],
        scratch_shapes=[pltpu.VMEM((SSM_GROUPS, SSM_STATE, SSM_GROUP_COLS), F32),
                        pltpu.VMEM((SUBLANES, SSM_INNER), F32),
                        pltpu.VMEM((SUBLANES, SSM_BC_DIM), F32), pltpu.VMEM((SUBLANES, SSM_BC_DIM), F32)],
        compiler_params=_cparams(("parallel", "arbitrary")),
        name="ssd_prompt",
    )(proj, proj, proj, proj, proj, _rep8_rows(conv_w), _rep8(conv_b), dtb_row, alog_row, dskip8,
      _rep8(norm_w), sel_hd, sel_c)


def _stacked_out(acc, n_inputs, out_index, n_layers, layer):
    if acc is None:
        return [], [], {}, (n_layers,), ("arbitrary",), lambda k: (layer + k[-1]) % n_layers
    return [pl.BlockSpec(memory_space=pl.ANY)], [acc], {n_inputs: out_index}, (), (), lambda k: layer


def _compute_or_fill(body, fill_axis, *refs):
    if fill_axis is None:
        body(*refs)
        return
    first = pl.program_id(fill_axis) == 0

    @pl.when(first)
    def _():
        body(*refs)

    @pl.when(jnp.logical_not(first))
    def _():
        refs[-1][...] = jnp.zeros_like(refs[-1])


def _conv_step(x, buf_ref, w8_ref, bias8=None):
    acc = x * w8_ref[CONV_WIDTH - 1]
    for i in range(CONV_WIDTH - 1):
        acc = acc + buf_ref[i] * w8_ref[i]
    if bias8 is not None:
        acc = acc + bias8
    return acc


def _conv_state_kernel(buf_ref, x_ref, *rest):
    out_ref = rest[-1]
    for i in range(CONV_WIDTH - 2):
        out_ref[i] = buf_ref[i + 1]
    out_ref[CONV_WIDTH - 2] = x_ref[...]


def conv_state_update(conv_t, proj, col_off, layer, acc, wc):
    nl, nbuf, b, n = conv_t.shape
    acc_specs, acc_args, aliases, fill_grid, fill_sem, lyr = _stacked_out(acc, 2, 0, nl, layer)
    return pl.pallas_call(
        functools.partial(_compute_or_fill, _conv_state_kernel, 1 if fill_grid else None),
        grid=(n // wc,) + fill_grid,
        in_specs=[pl.BlockSpec((None, nbuf, b, wc), lambda j, *k: (layer, 0, 0, j)),
                  pl.BlockSpec((b, wc), lambda j, *k: (0, col_off // wc + j))] + acc_specs,
        out_specs=pl.BlockSpec((None, nbuf, b, wc), lambda j, *k: (lyr(k), 0, 0, j)),
        out_shape=jax.ShapeDtypeStruct(conv_t.shape, F32),
        input_output_aliases=aliases,
        compiler_params=_cparams(("parallel",) + fill_sem),
        name="conv_state",
    )(conv_t, proj, *acc_args)


def _gdn_decode_kernel(q_ref, k_ref, v_ref, z_ref, ba_ref, bq_ref, bk_ref, bv_ref, s_ref,
                       cwq_ref, cwk_ref, cwv_ref, alog_ref, dtb_ref, nw_ref, *rest):
    o_ref, s_out_ref = rest[-2:]
    hd = GDN_DIM
    tb = q_ref.shape[0]
    q = _silu(_conv_step(q_ref[...], bq_ref, cwq_ref))
    k = _silu(_conv_step(k_ref[...], bk_ref, cwk_ref))
    v = _silu(_conv_step(v_ref[...], bv_ref, cwv_ref))
    ba = ba_ref[...]
    beta = _sigmoid(ba)
    eg = jnp.exp(-jnp.exp(alog_ref[...]) * _softplus(ba + dtb_ref[...]))
    qn, kn = [], []
    for h in range(GDN_HEADS):
        qh, kh = q[:, h * hd:(h + 1) * hd], k[:, h * hd:(h + 1) * hd]
        qn.append(qh * lax.rsqrt(jnp.sum(qh * qh, axis=-1, keepdims=True) + NORM_EPS) * (hd ** -0.5))
        kn.append(kh * lax.rsqrt(jnp.sum(kh * kh, axis=-1, keepdims=True) + NORM_EPS))
    q_t = _transpose(jnp.concatenate(qn, axis=1))
    k_t = _transpose(jnp.concatenate(kn, axis=1))
    chains = [(b, h) for b in range(tb) for h in range(GDN_HEADS)]
    s_old = [s_ref[b, h] for b, h in chains]
    k_col = [k_t[h * hd:(h + 1) * hd, b:b + 1] for b, h in chains]
    q_col = [q_t[h * hd:(h + 1) * hd, b:b + 1] for b, h in chains]
    egb = [eg[b:b + 1, GDN_HEADS + h:GDN_HEADS + h + 1] for b, h in chains]
    ks = [jnp.sum(kc * s, axis=0, keepdims=True) for kc, s in zip(k_col, s_old)]
    delta = [beta[b:b + 1, h:h + 1] * (v[b:b + 1, h * hd:(h + 1) * hd] - e * x)
             for (b, h), e, x in zip(chains, egb, ks)]
    s_new = [e * s + kc * dl for e, s, kc, dl in zip(egb, s_old, k_col, delta)]
    for (b, h), sn in zip(chains, s_new):
        s_out_ref[b, h] = sn
    ys = [jnp.sum(qc * sn, axis=0, keepdims=True) for qc, sn in zip(q_col, s_new)]
    for h in range(GDN_HEADS):
        y = jnp.concatenate([ys[b * GDN_HEADS + h] for b in range(tb)], axis=0)
        y = _rms_rows(y, nw_ref[...]) * _silu(z_ref[:, h * hd:(h + 1) * hd])
        o_ref[:, h * hd:(h + 1) * hd] = y.astype(o_ref.dtype)


def gdn_decode(proj, state, conv_t, layer, acc, conv_w, alog_row, dtb_row, norm_w, tb=8):
    b = proj.shape[0]
    hd, od = GDN_DIM, GDN_OUT_DIM
    acc_specs, acc_args, aliases, fill_grid, fill_sem, lyr = _stacked_out(acc, 15, 1, state.shape[0], layer)
    pcol = lambda off: pl.BlockSpec((tb, od), lambda i, *k: (i, off // od))
    ccol = lambda part: pl.BlockSpec((None, CONV_WIDTH - 1, tb, od), lambda i, *k: (layer, 0, i, part))
    wcol = lambda part: pl.BlockSpec((CONV_WIDTH, SUBLANES, od), lambda i, *k: (0, 0, part))
    row = _resident((1, LANES))
    sblock = (None, tb, GDN_HEADS, hd, hd)
    assert tb == SUBLANES
    conv_w8 = _rep8_rows(conv_w)
    return pl.pallas_call(
        functools.partial(_compute_or_fill, _gdn_decode_kernel, 1 if fill_grid else None),
        grid=(b // tb,) + fill_grid,
        in_specs=[pcol(HYB_OFF_QKV), pcol(HYB_OFF_QKV + od), pcol(HYB_OFF_QKV + 2 * od),
                  pcol(HYB_OFF_Z), pl.BlockSpec((tb, LANES), lambda i, *k: (i, HYB_OFF_BA // LANES)),
                  ccol(0), ccol(1), ccol(2), pl.BlockSpec(sblock, lambda i, *k: (layer, i, 0, 0, 0)),
                  wcol(0), wcol(1), wcol(2), row, row, _resident((SUBLANES, hd))] + acc_specs,
        out_specs=[pl.BlockSpec((tb, od), lambda i, *k: (i, 0)),
                   pl.BlockSpec(sblock, lambda i, *k: (lyr(k), i, 0, 0, 0))],
        out_shape=[jax.ShapeDtypeStruct((b, od), BF16), jax.ShapeDtypeStruct(state.shape, F32)],
        input_output_aliases=aliases,
        compiler_params=_cparams(("parallel",) + fill_sem),
        name="gdn_decode",
    )(proj, proj, proj, proj, proj, conv_t, conv_t, conv_t, state,
      conv_w8, conv_w8, conv_w8, alog_row, dtb_row, _rep8(norm_w), *acc_args)


def _ret_decode_kernel(q_ref, k_ref, v_ref, gate_ref, s_ref, cos_ref, sin_ref, nw_ref, *rest):
    o_ref, s_out_ref = rest[-2:]
    q = _rotary(q_ref[...], cos_ref[...], sin_ref[...])
    k = _rotary(k_ref[...], cos_ref[...], sin_ref[...]) * (RET_KEY_DIM ** -0.5)
    q_t = _transpose(q)
    k_t = _transpose(k)
    tb = q.shape[0]
    kd, vd = RET_KEY_DIM, RET_VAL_DIM
    rows = []
    for b in range(tb):
        ys = []
        for h in range(RET_HEADS):
            gamma = math.exp(_ret_log_gamma(h))
            s = s_ref[b, h]
            s_new = s * gamma + k_t[h * kd:(h + 1) * kd, b:b + 1] * v_ref[b:b + 1, h * vd:(h + 1) * vd]
            s_out_ref[b, h] = s_new
            ys.append(jnp.sum(q_t[h * kd:(h + 1) * kd, b:b + 1] * s_new, axis=0, keepdims=True))
        rows.append(jnp.concatenate(ys, axis=1))
    y = jnp.concatenate(rows, axis=0)
    for h in range(RET_HEADS):
        yh = (_mul8(_group_norm_rows(y[:, h * vd:(h + 1) * vd]), nw_ref[:, h * vd:(h + 1) * vd])
              * _silu(gate_ref[:, h * vd:(h + 1) * vd]))
        o_ref[:, h * vd:(h + 1) * vd] = yh.astype(o_ref.dtype)


def ret_decode(proj, state, layer, acc, cos_f, sin_s, norm_w, tb=8):
    b = proj.shape[0]
    acc_specs, acc_args, aliases, fill_grid, fill_sem, lyr = _stacked_out(acc, 8, 1, state.shape[0], layer)
    pcol = lambda width, off: pl.BlockSpec((tb, width), lambda i, *k: (i, off // width))
    sblock = (None, tb, RET_HEADS, RET_KEY_DIM, RET_VAL_DIM)
    return pl.pallas_call(
        functools.partial(_compute_or_fill, _ret_decode_kernel, 1 if fill_grid else None),
        grid=(b // tb,) + fill_grid,
        in_specs=[pcol(RET_QK_DIM, HYB_OFF_RQ), pcol(RET_QK_DIM, HYB_OFF_RK), pcol(RET_OUT_DIM, HYB_OFF_RV),
                  pcol(RET_OUT_DIM, HYB_OFF_RG), pl.BlockSpec(sblock, lambda i, *k: (layer, i, 0, 0, 0)),
                  _resident((1, RET_QK_DIM)), _resident((1, RET_QK_DIM)),
                  _resident((SUBLANES, RET_OUT_DIM))] + acc_specs,
        out_specs=[pl.BlockSpec((tb, RET_OUT_DIM), lambda i, *k: (i, 0)),
                   pl.BlockSpec(sblock, lambda i, *k: (lyr(k), i, 0, 0, 0))],
        out_shape=[jax.ShapeDtypeStruct((b, RET_OUT_DIM), BF16), jax.ShapeDtypeStruct(state.shape, F32)],
        input_output_aliases=aliases,
        compiler_params=_cparams(("parallel",) + fill_sem),
        name="ret_decode",
    )(proj, proj, proj, proj, state, cos_f, sin_s, _rep8(norm_w), *acc_args)


def _ssd_decode_kernel(z_ref, x_ref, b_ref, c_ref, dt_ref, bx_ref, bb_ref, bc_ref, s_ref,
                       cwx_ref, cwb_ref, cwc_ref, cbx_ref, cbb_ref, cbc_ref,
                       dtb_ref, alog_ref, dskip_ref, nw_ref, *rest):
    o_ref, s_out_ref = rest[-2:]
    hd, gw, ns = SSM_HEAD_DIM, SSM_GROUP_COLS, SSM_STATE
    tb, width = x_ref.shape
    ng = width // gw
    first_head = pl.program_id(1) * (ng * SSM_HPG)
    x = _silu(_conv_step(x_ref[...], bx_ref, cwx_ref, cbx_ref[...]))
    bm = _silu(_conv_step(b_ref[...], bb_ref, cwb_ref, cbb_ref[...]))
    cm = _silu(_conv_step(c_ref[...], bc_ref, cwc_ref, cbc_ref[...]))
    dt = _softplus(dt_ref[...] + dtb_ref[...])
    spread = (_iota((LANES, width), 0) == first_head + _iota((LANES, width), 1) // hd).astype(BF16)
    per_head = jnp.concatenate([dt, -jnp.exp(alog_ref[...]), dskip_ref[...],
                                jnp.zeros((2 * SUBLANES - tb - 2, LANES), F32)], axis=0)
    ph_hi, ph_lo = _split(per_head)
    per_col = (jnp.dot(ph_hi, spread, preferred_element_type=F32)
               + jnp.dot(ph_lo, spread, preferred_element_type=F32))
    dt_x, a_x, dskip_x = per_col[0:tb], per_col[tb:tb + 1], per_col[tb + 1:tb + 2]
    eg_x = jnp.exp(dt_x * a_x)
    xdt_t = (x * dt_x).T
    row_id = _iota((tb, ns), 0)
    pairs = [(b, g) for b in range(tb) for g in range(ng)]
    outer = [_dot3(_split(xdt_t[g * gw:(g + 1) * gw, :]),
                   _split(jnp.where(row_id == b, bm[:, g * ns:(g + 1) * ns], 0.0))) for b, g in pairs]
    for (b, g), out in zip(pairs, outer):
        for hl in range(SSM_HPG):
            h = g * SSM_HPG + hl
            s_out_ref[b, h] = s_ref[b, h] * eg_x[b:b + 1, h * hd:h * hd + 1] + out[hl * hd:(hl + 1) * hd, :]
    ys = [_dot(cm[b:b + 1, g * ns:(g + 1) * ns],
               s_out_ref[b, g * SSM_HPG:(g + 1) * SSM_HPG].reshape(gw, ns), _NT) for b, g in pairs]
    y = jnp.concatenate([jnp.concatenate(ys[b * ng:(b + 1) * ng], axis=1) for b in range(tb)], axis=0)
    y = (y + dskip_x * x) * _silu(z_ref[...])
    for g in range(ng):
        cols = slice(g * gw, (g + 1) * gw)
        o_ref[:, cols] = _rms_rows(y[:, cols], nw_ref[:, cols]).astype(o_ref.dtype)


def ssd_decode(proj, state_t, conv_t, layer, acc, conv_w, conv_b, dtb_row, alog_row, dskip_row, norm_w, tb=8, ng=4):
    b = proj.shape[0]
    gw = ng * SSM_GROUP_COLS
    ns = ng * SSM_STATE
    assert tb == SUBLANES
    cw8, cb8, nw8 = _rep8_rows(conv_w), _rep8(conv_b), _rep8(norm_w)
    acc_specs, acc_args, aliases, fill_grid, fill_sem, lyr = _stacked_out(acc, 19, 1, state_t.shape[0], layer)
    pcol = lambda width, off: pl.BlockSpec((tb, width), lambda i, g, *k: (i, off // width + g))
    ccol = lambda width, off: pl.BlockSpec((None, CONV_WIDTH - 1, tb, width),
                                           lambda i, g, *k: (layer, 0, i, off // width + g))
    wcol = lambda width, off: pl.BlockSpec((CONV_WIDTH, SUBLANES, width), lambda i, g, *k: (0, 0, off // width + g))
    rcol = lambda width, off: pl.BlockSpec((SUBLANES, width), lambda i, g, *k: (0, off // width + g))
    row = _resident((1, LANES))
    sblock = (None, tb, ng * SSM_HPG, SSM_HEAD_DIM, SSM_STATE)
    off_b, off_c = SSM_INNER, SSM_INNER + SSM_BC_DIM
    return pl.pallas_call(
        functools.partial(_compute_or_fill, _ssd_decode_kernel, 2 if fill_grid else None),
        grid=(b // tb, SSM_GROUPS // ng) + fill_grid,
        in_specs=[pcol(gw, SSM_OFF_Z), pcol(gw, SSM_OFF_X), pcol(ns, SSM_OFF_B), pcol(ns, SSM_OFF_C),
                  pl.BlockSpec((tb, LANES), lambda i, g, *k: (i, SSM_OFF_DT // LANES)),
                  ccol(gw, 0), ccol(ns, off_b), ccol(ns, off_c),
                  pl.BlockSpec(sblock, lambda i, g, *k: (layer, i, g, 0, 0)),
                  wcol(gw, 0), wcol(ns, off_b), wcol(ns, off_c),
                  rcol(gw, 0), rcol(ns, off_b), rcol(ns, off_c),
                  row, row, row, rcol(gw, 0)] + acc_specs,
        out_specs=[pl.BlockSpec((tb, gw), lambda i, g, *k: (i, g)),
                   pl.BlockSpec(sblock, lambda i, g, *k: (lyr(k), i, g, 0, 0))],
        out_shape=[jax.ShapeDtypeStruct((b, SSM_INNER), BF16), jax.ShapeDtypeStruct(state_t.shape, F32)],
        input_output_aliases=aliases,
        compiler_params=_cparams(("parallel", "parallel") + fill_sem),
        name="ssd_decode",
    )(proj, proj, proj, proj, proj, conv_t, conv_t, conv_t, state_t,
      cw8, cw8, cw8, cb8, cb8, cb8, dtb_row, alog_row, dskip_row, nw8, *acc_args)


def _lane_row(vals, offset):
    return jnp.zeros((1, LANES), F32).at[0, offset:offset + vals.shape[0]].set(vals.astype(F32))


def _prep_hyb_w_in(w):
    sizes = (GDN_QKV_DIM, GDN_OUT_DIM, GDN_HEADS, GDN_HEADS, RET_QK_DIM, RET_QK_DIM, RET_OUT_DIM, RET_OUT_DIM)
    offs = [0]
    for s in sizes:
        offs.append(offs[-1] + s)
    wb = w.astype(BF16)
    part = lambda i: wb[..., offs[i]:offs[i + 1]]
    cols = [part(0), part(1), part(4), part(5), part(6), part(7), part(2), part(3)]
    pad = jnp.zeros(w.shape[:-1] + (HYB_N - offs[-1],), BF16)
    return jnp.concatenate(cols + [pad], axis=-1)


def _prep_ssm_w_in(w):
    pad = jnp.zeros(w.shape[:-1] + (SSM_N - w.shape[-1],), BF16)
    return jnp.concatenate([w.astype(BF16), pad], axis=-1)


def _rope_tables(pos):
    half = RET_KEY_DIM // 2
    inv_freq = ROPE_BASE ** (-jnp.arange(half, dtype=F32) / half)
    ang = pos.astype(F32)[:, None] * inv_freq[None, :]
    cos, sin = jnp.cos(ang), jnp.sin(ang)
    cos_f = jnp.tile(jnp.concatenate([cos, cos], axis=1), (1, RET_HEADS))
    sin_s = jnp.tile(jnp.concatenate([-sin, sin], axis=1), (1, RET_HEADS))
    return cos_f, sin_s


def _trunk(x, pos, states, params, prompt):
    bsz, l, d = x.shape
    m = bsz * l
    tm = 512 if m % 512 == 0 else m
    x2 = x.reshape(m, d)
    cos_f, sin_s = _rope_tables(pos)
    new_gdn, new_gdn_conv, new_ret, new_ssm, new_ssm_conv = [], [], [], [], []
    if not prompt:
        s_gdn, c_gdn, s_ret, s_ssm, c_ssm = states
        c_gdn_t, c_ssm_t = jnp.swapaxes(c_gdn, 1, 2), jnp.swapaxes(c_ssm, 1, 2)
        s_ssm_t = jnp.swapaxes(s_ssm, 3, 4)
        acc_gdn = acc_gdn_conv = acc_ret = acc_ssm = acc_ssm_conv = None
    for layer in range(DEPTH):
        i = layer // 2
        p = params
        if layer % 2 == 0:
            proj = norm_matmul(x2, p["norm_mix8"], layer, p["w_in_hyb"], i, tm)
            alog_row = _lane_row(p["gdn_a_log"][i], GDN_HEADS)
            dtb_row = _lane_row(p["gdn_dt_bias"][i], GDN_HEADS)
            if prompt:
                proj3 = proj.reshape(bsz, l, HYB_N)
                o_a, sg = gdn_prompt(proj3, p["gdn_conv_w"][i], alog_row, dtb_row, p["gdn_norm_w"][i])
                o_b, sr = ret_prompt(proj3, cos_f, sin_s, p["ret_norm_w"][i])
                cg = proj3[:, l - (CONV_WIDTH - 1):, :GDN_QKV_DIM]
                o_a, o_b = o_a.reshape(m, GDN_OUT_DIM), o_b.reshape(m, RET_OUT_DIM)
                new_gdn.append(sg)
                new_gdn_conv.append(cg)
                new_ret.append(sr)
            else:
                o_a, acc_gdn = gdn_decode(proj, s_gdn, c_gdn_t, i, acc_gdn, p["gdn_conv_w"][i], alog_row,
                                          dtb_row, p["gdn_norm_w"][i])
                acc_gdn_conv = conv_state_update(c_gdn_t, proj, HYB_OFF_QKV, i, acc_gdn_conv, GDN_OUT_DIM)
                o_b, acc_ret = ret_decode(proj, s_ret, i, acc_ret, cos_f, sin_s, p["ret_norm_w"][i])
            acts, w_out = [o_a, o_b], p["w_out_hyb"]
        else:
            proj = norm_matmul(x2, p["norm_mix8"], layer, p["w_in_ssm"], i, tm)
            dtb_row = _lane_row(p["ssm_dt_bias"][i], 0)
            alog_row = _lane_row(p["ssm_a_log"][i], 0)
            dskip_row = _lane_row(p["ssm_d"][i], 0)
            if prompt:
                proj3 = proj.reshape(bsz, l, SSM_N)
                y, ss = ssd_prompt(proj3, p["ssm_conv_w"][i], p["ssm_conv_b"][i], dtb_row, alog_row,
                                   dskip_row, p["ssm_norm_w"][i])
                cs = proj3[:, l - (CONV_WIDTH - 1):, SSM_OFF_X:SSM_OFF_X + SSM_CONV_DIM]
                y = y.reshape(m, SSM_INNER)
                new_ssm.append(ss)
                new_ssm_conv.append(cs)
            else:
                y, acc_ssm = ssd_decode(proj, s_ssm_t, c_ssm_t, i, acc_ssm, p["ssm_conv_w"][i],
                                        p["ssm_conv_b"][i], dtb_row, alog_row, dskip_row, p["ssm_norm_w"][i])
                acc_ssm_conv = conv_state_update(c_ssm_t, proj, SSM_OFF_X, i, acc_ssm_conv, D_MODEL)
            acts, w_out = [y], p["w_out_ssm"]
        x2 = post_mixer(acts, w_out, i, x2, p["norm_mlp8"], layer, p["mlp_w1"], p["mlp_w2"],
                        p["norm_final8"], layer == DEPTH - 1, tm)
    y_out = x2.reshape(bsz, l, d)
    if prompt:
        return (y_out, jnp.stack(new_gdn), jnp.stack(new_gdn_conv), jnp.stack(new_ret),
                jnp.swapaxes(jnp.stack(new_ssm), 3, 4), jnp.stack(new_ssm_conv))
    return (y_out, acc_gdn, jnp.swapaxes(acc_gdn_conv, 1, 2), acc_ret,
            jnp.swapaxes(acc_ssm, 3, 4), jnp.swapaxes(acc_ssm_conv, 1, 2))


def kernel(x_prompt, x_sample, state_gdn, state_gdn_conv, state_ret, state_ssm, state_ssm_conv, norm_mix, norm_mlp, norm_final, w_in_hyb, gdn_conv_w, gdn_a_log, gdn_dt_bias, gdn_norm_w, ret_norm_w, w_out_hyb, w_in_ssm, ssm_conv_w, ssm_conv_b, ssm_dt_bias, ssm_a_log, ssm_d, ssm_norm_w, w_out_ssm, mlp_w1, mlp_w2):
    params = dict(
        norm_mix8=_rep8_rows(norm_mix), norm_mlp8=_rep8_rows(norm_mlp), norm_final8=_rep8(norm_final),
        w_in_hyb=_prep_hyb_w_in(w_in_hyb),
        gdn_conv_w=gdn_conv_w, gdn_a_log=gdn_a_log, gdn_dt_bias=gdn_dt_bias, gdn_norm_w=gdn_norm_w,
        ret_norm_w=ret_norm_w, w_out_hyb=w_out_hyb.astype(BF16),
        w_in_ssm=_prep_ssm_w_in(w_in_ssm),
        ssm_conv_w=ssm_conv_w, ssm_conv_b=ssm_conv_b, ssm_dt_bias=ssm_dt_bias, ssm_a_log=ssm_a_log,
        ssm_d=ssm_d, ssm_norm_w=ssm_norm_w, w_out_ssm=w_out_ssm.astype(BF16),
        mlp_w1=mlp_w1.astype(BF16), mlp_w2=mlp_w2.astype(BF16))
    lp, ls = x_prompt.shape[1], x_sample.shape[1]
    pos_prompt = jnp.arange(lp, dtype=jnp.int32)
    pos_sample = PAST_LEN + jnp.arange(ls, dtype=jnp.int32)
    y_p, p_gdn, p_gdn_conv, p_ret, p_ssm, p_ssm_conv = _trunk(
        x_prompt, pos_prompt, (None,) * 5, params, prompt=True)
    y_s, s_gdn, s_gdn_conv, s_ret, s_ssm, s_ssm_conv = _trunk(
        x_sample, pos_sample, (state_gdn, state_gdn_conv, state_ret, state_ssm, state_ssm_conv),
        params, prompt=False)
    return (y_p, y_s, p_gdn, p_gdn_conv, p_ret, p_ssm, p_ssm_conv,
            s_gdn, s_gdn_conv, s_ret, s_ssm, s_ssm_conv)
```

```python
import functools
import math

import jax
import jax.numpy as jnp
from jax import lax
from jax.experimental import pallas as pl
from jax.experimental.pallas import tpu as pltpu

F32 = jnp.float32
BF16 = jnp.bfloat16

D_MODEL = 1024
DEPTH = 4
CONV_WIDTH = 4
NORM_EPS = 1e-6
PAST_LEN = 16384

GDN_HEADS = 4
GDN_DIM = 128
GDN_QKV_DIM = 3 * GDN_HEADS * GDN_DIM
GDN_OUT_DIM = GDN_HEADS * GDN_DIM

RET_HEADS = 4
RET_KEY_DIM = 64
RET_VAL_DIM = 128
RET_QK_DIM = RET_HEADS * RET_KEY_DIM
RET_OUT_DIM = RET_HEADS * RET_VAL_DIM
ROPE_BASE = 10000.0

SSM_INNER = 2 * D_MODEL
SSM_HEAD_DIM = 64
SSM_HEADS = SSM_INNER // SSM_HEAD_DIM
SSM_GROUPS = 4
SSM_HPG = SSM_HEADS // SSM_GROUPS
SSM_STATE = 128
SSM_GROUP_COLS = SSM_HPG * SSM_HEAD_DIM
SSM_BC_DIM = SSM_GROUPS * SSM_STATE
SSM_CONV_DIM = SSM_INNER + 2 * SSM_BC_DIM
MLP_HIDDEN = 4 * D_MODEL

LANES = 128
SUBLANES = 8

HYB_OFF_QKV = 0
HYB_OFF_Z = GDN_QKV_DIM
HYB_OFF_RQ = HYB_OFF_Z + GDN_OUT_DIM
HYB_OFF_RK = HYB_OFF_RQ + RET_QK_DIM
HYB_OFF_RV = HYB_OFF_RK + RET_QK_DIM
HYB_OFF_RG = HYB_OFF_RV + RET_OUT_DIM
HYB_OFF_BA = HYB_OFF_RG + RET_OUT_DIM
HYB_N = 3840
SSM_OFF_Z = 0
SSM_OFF_X = SSM_INNER
SSM_OFF_B = SSM_OFF_X + SSM_INNER
SSM_OFF_C = SSM_OFF_B + SSM_BC_DIM
SSM_OFF_DT = SSM_OFF_C + SSM_BC_DIM
SSM_N = 5376
PROJ_TN = 768

VMEM_LIMIT = 56 * 1024 * 1024

_NN = (((1,), (0,)), ((), ()))
_NT = (((1,), (1,)), ((), ()))
_TN = (((0,), (0,)), ((), ()))


def _dot(a, b, dims=_NN):
    return lax.dot_general(a.astype(BF16), b.astype(BF16), dims, preferred_element_type=F32)


def _dot_f32(a, b, dims=_NN):
    return lax.dot_general(a, b, dims, precision=lax.Precision.HIGHEST, preferred_element_type=F32)


def _sigmoid(x):
    return 0.5 + 0.5 * jnp.tanh(0.5 * x)


def _silu(x):
    t = 0.5 * x
    return t + t * jnp.tanh(t)


def _softplus(x):
    return jnp.maximum(x, 0.0) + jnp.log(1.0 + jnp.exp(-jnp.abs(x)))


def _iota(shape, dim):
    return lax.broadcasted_iota(jnp.int32, shape, dim)


def _transpose(x):
    return x.T


def _cparams(sem):
    return pltpu.CompilerParams(dimension_semantics=sem, vmem_limit_bytes=VMEM_LIMIT)


def _resident(shape):
    nd = len(shape)
    return pl.BlockSpec(shape, lambda *_: (0,) * nd, pipeline_mode=pl.Buffered(1))


def _resident_layer(shape, layer, rows=None, row_block=0):
    rows = shape[1] if rows is None else rows
    return pl.BlockSpec((None, rows, shape[2]), lambda *_: (layer, row_block, 0), pipeline_mode=pl.Buffered(1))


def _rep8(v):
    v = v.reshape(1, -1)
    return jnp.broadcast_to(v, (SUBLANES, v.shape[1]))


def _rep8_rows(w):
    return jnp.broadcast_to(w[:, None, :], (w.shape[0], SUBLANES, w.shape[1]))


def _mul8(x, r8):
    r, n = x.shape
    return (x.reshape(r // SUBLANES, SUBLANES, n) * r8).reshape(r, n)


def _add8(x, r8):
    r, n = x.shape
    return (x.reshape(r // SUBLANES, SUBLANES, n) + r8).reshape(r, n)


def _rms_rows(x, gain8):
    return _mul8(x * lax.rsqrt(jnp.mean(x * x, axis=-1, keepdims=True) + NORM_EPS), gain8)


def _norm_matmul_kernel(x_ref, g_ref, w_ref, o_ref, *, tn):
    xn = _rms_rows(x_ref[...], g_ref[...]).astype(BF16)
    for j in range(w_ref.shape[1] // tn):
        o_ref[:, j * tn:(j + 1) * tn] = jnp.dot(xn, w_ref[:, j * tn:(j + 1) * tn],
                                                preferred_element_type=F32)


def norm_matmul(x, gains8, layer, w, w_layer, tm):
    m, d = x.shape
    n = w.shape[2]
    return pl.pallas_call(
        functools.partial(_norm_matmul_kernel, tn=PROJ_TN),
        grid=(m // tm,),
        in_specs=[pl.BlockSpec((tm, d), lambda i: (i, 0)), _resident_layer(gains8.shape, layer),
                  _resident_layer(w.shape, w_layer)],
        out_specs=pl.BlockSpec((tm, n), lambda i: (i, 0)),
        out_shape=jax.ShapeDtypeStruct((m, n), F32),
        compiler_params=_cparams(("parallel",)),
        name="norm_matmul",
    )(x, gains8, w)


def _post_mixer_kernel(*refs, n_in, tf, final_norm):
    a_refs, w_refs = refs[:n_in], refs[n_in:2 * n_in]
    r_ref, g_ref, w1_ref, w2_ref, gf_ref, o_ref = refs[2 * n_in:]
    x = r_ref[...]
    for a_ref, w_ref in zip(a_refs, w_refs):
        x = x + jnp.dot(a_ref[...], w_ref[...], preferred_element_type=F32)
    o_ref[...] = x
    x = o_ref[...]
    xn = _rms_rows(x, g_ref[...]).astype(BF16)
    acc = x
    for j in range(w1_ref.shape[1] // tf):
        h = jnp.dot(xn, w1_ref[:, j * tf:(j + 1) * tf], preferred_element_type=F32)
        h = jnp.maximum(h, 0.0)
        acc = acc + jnp.dot((h * h).astype(BF16), w2_ref[j * tf:(j + 1) * tf, :],
                            preferred_element_type=F32)
    if final_norm:
        acc = _rms_rows(acc, gf_ref[...])
    o_ref[...] = acc


def post_mixer(acts, w_out, w_layer, res, gains8, layer, w1, w2, final8, final_norm, tm):
    m, d = res.shape
    n_in = len(acts)
    k_rows = acts[0].shape[1]
    assert all(a.shape[1] == k_rows for a in acts) and n_in * k_rows == w_out.shape[1]
    in_specs = ([pl.BlockSpec((tm, k_rows), lambda i: (i, 0)) for _ in acts]
                + [_resident_layer(w_out.shape, w_layer, k_rows, j) for j in range(n_in)]
                + [pl.BlockSpec((tm, d), lambda i: (i, 0)), _resident_layer(gains8.shape, layer),
                   _resident_layer(w1.shape, layer), _resident_layer(w2.shape, layer), _resident((SUBLANES, d))])
    return pl.pallas_call(
        functools.partial(_post_mixer_kernel, n_in=n_in, tf=1024, final_norm=final_norm),
        grid=(m // tm,),
        in_specs=in_specs,
        out_specs=pl.BlockSpec((tm, d), lambda i: (i, 0)),
        out_shape=jax.ShapeDtypeStruct((m, d), F32),
        compiler_params=_cparams(("parallel",)),
        name="post_mixer",
    )(*acts, *([w_out] * n_in), res, gains8, w1, w2, final8)


def _causal_conv_chunk(x, tail, w8, bias8=None):
    c, n = x.shape
    g = c // SUBLANES
    x3 = x.reshape(g, SUBLANES, n)
    acc = x3 * w8[CONV_WIDTH - 1]
    row = _iota((SUBLANES, n), 0)
    for k in range(1, CONV_WIDTH):
        r = pltpu.roll(x3, k, 1)
        prev = jnp.concatenate([pltpu.roll(tail, k, 0)[None], r[:g - 1]], axis=0)
        acc = acc + jnp.where(row < k, prev, r) * w8[CONV_WIDTH - 1 - k]
    if bias8 is not None:
        acc = acc + bias8
    return acc.reshape(c, n)


def _cumsum_rows(g):
    c = g.shape[0]
    tri = (_iota((c, c), 0) >= _iota((c, c), 1)).astype(F32)
    return _dot_f32(tri, g)


def _decay_matrix(gc_col, gc_row):
    c = gc_col.shape[0]
    causal = _iota((c, c), 0) >= _iota((c, c), 1)
    return jnp.where(causal, jnp.exp(gc_col - gc_row), 0.0)


def _split(x):
    hi = x.astype(BF16)
    return hi, (x - hi.astype(F32)).astype(BF16)


def _dot3(a, b, dims=_NN):
    (ah, al), (bh, bl) = a, b
    d = lambda p, q: lax.dot_general(p, q, dims, preferred_element_type=F32)
    return d(ah, bh) + (d(ah, bl) + d(al, bh))


def _unit_lower_inverses(n_list):
    c = n_list[0].shape[0]
    ri, ci = _iota((c, c), 0), _iota((c, c), 1)
    eye = (ri == ci).astype(F32)
    blk = 16
    same = (ri // blk) == (ci // blk)
    p = [jnp.where(same, -n, 0.0).astype(BF16) for n in n_list]
    inv = [eye + x for x in p]
    for _ in range(3):
        p = [_dot(x, x).astype(BF16) for x in p]
        inv = [i + _dot(i, x) for i, x in zip(inv, p)]
    while blk < c:
        pair = ((ri // (2 * blk)) == (ci // (2 * blk))) & ((ri // blk) != (ci // blk))
        inv_b = [i.astype(BF16) for i in inv]
        t = [_dot(jnp.where(pair, n, 0.0), i) for n, i in zip(n_list, inv_b)]
        inv = [i - _dot(i_b, x) for i, i_b, x in zip(inv, inv_b, t)]
        blk *= 2
    return inv


def _gdn_prompt_kernel(qkv_ref, z_ref, ba_ref, cw_ref, alog_ref, dtb_ref, nw_ref,
                       o_ref, s_out_ref, s_ref, tail_ref):
    c_idx = pl.program_id(1)

    @pl.when(c_idx == 0)
    def _():
        s_ref[...] = jnp.zeros_like(s_ref)
        tail_ref[...] = jnp.zeros_like(tail_ref)

    nb, c = qkv_ref.shape[0], qkv_ref.shape[1]
    hd = GDN_DIM
    ri, ci = _iota((c, c), 0), _iota((c, c), 1)
    strict = ri > ci
    chains = [(bb, h) for bb in range(nb) for h in range(GDN_HEADS)]

    conv, beta, gcum, gcum_t = [], [], [], []
    for bb in range(nb):
        x = qkv_ref[bb]
        conv.append(_silu(_causal_conv_chunk(x, tail_ref[bb], cw_ref[...])))
        tail_ref[bb] = x[c - SUBLANES:c]
        ba = ba_ref[bb]
        beta.append(_sigmoid(ba))
        gc = _cumsum_rows(-jnp.exp(alog_ref[...]) * _softplus(ba + dtb_ref[...]))
        gcum.append(gc)
        gcum_t.append(_transpose(gc))

    q, k, v, b_col, gc_col, g_end, decay = [], [], [], [], [], [], []
    for bb, h in chains:
        qh = conv[bb][:, h * hd:(h + 1) * hd]
        kh = conv[bb][:, GDN_OUT_DIM + h * hd:GDN_OUT_DIM + (h + 1) * hd]
        q.append(qh * lax.rsqrt(jnp.sum(qh * qh, axis=-1, keepdims=True) + NORM_EPS) * (hd ** -0.5))
        k.append(kh * lax.rsqrt(jnp.sum(kh * kh, axis=-1, keepdims=True) + NORM_EPS))
        v.append(conv[bb][:, 2 * GDN_OUT_DIM + h * hd:2 * GDN_OUT_DIM + (h + 1) * hd])
        b_col.append(beta[bb][:, h:h + 1])
        gc_col.append(gcum[bb][:, GDN_HEADS + h:GDN_HEADS + h + 1])
        g_end.append(gcum[bb][c - 1:c, GDN_HEADS + h:GDN_HEADS + h + 1])
        decay.append(_decay_matrix(gc_col[-1], gcum_t[bb][GDN_HEADS + h:GDN_HEADS + h + 1, :]))
    kk = [_dot(x, x, _NT) for x in k]
    qk = [_dot(x, y, _NT) * d for x, y, d in zip(q, k, decay)]
    t_inv = _unit_lower_inverses([jnp.where(strict, b * x * d, 0.0) for b, x, d in zip(b_col, kk, decay)])
    rhs = [jnp.concatenate([vv * b, kx * (b * jnp.exp(g))], axis=1)
           for vv, kx, b, g in zip(v, k, b_col, gc_col)]
    sol = [_dot(t, r) for t, r in zip(t_inv, rhs)]
    s_old = [s_ref[bb, h] for bb, h in chains]
    delta = [x[:, :hd] - _dot(x[:, hd:], s) for x, s in zip(sol, s_old)]
    y = [_dot(x * jnp.exp(g), s) + _dot(a, dl)
         for x, g, s, a, dl in zip(q, gc_col, s_old, qk, delta)]
    s_new = [s * jnp.exp(ge) + _dot(kx * jnp.exp(ge - g), dl, _TN)
             for s, ge, kx, g, dl in zip(s_old, g_end, k, gc_col, delta)]
    for (bb, h), sn, yy in zip(chains, s_new, y):
        s_ref[bb, h] = sn
        out = _rms_rows(yy, nw_ref[...]) * _silu(z_ref[bb, :, h * hd:(h + 1) * hd])
        o_ref[bb, :, h * hd:(h + 1) * hd] = out.astype(o_ref.dtype)

    @pl.when(c_idx == pl.num_programs(1) - 1)
    def _():
        s_out_ref[...] = s_ref[...]


def gdn_prompt(proj, conv_w, alog_row, dtb_row, norm_w, chunk=64, nb=8):
    b, l, _ = proj.shape
    nb = min(nb, b)
    nc = l // chunk
    col = lambda width, off: pl.BlockSpec((nb, chunk, width), lambda i, j: (i, j, off // width))
    return pl.pallas_call(
        _gdn_prompt_kernel,
        grid=(b // nb, nc),
        in_specs=[col(GDN_QKV_DIM, HYB_OFF_QKV), col(GDN_OUT_DIM, HYB_OFF_Z), col(LANES, HYB_OFF_BA),
                  _resident((CONV_WIDTH, SUBLANES, GDN_QKV_DIM)), _resident((1, LANES)), _resident((1, LANES)),
                  _resident((SUBLANES, GDN_DIM))],
        out_specs=[pl.BlockSpec((nb, chunk, GDN_OUT_DIM), lambda i, j: (i, j, 0)),
                   pl.BlockSpec((nb, GDN_HEADS, GDN_DIM, GDN_DIM), lambda i, j: (i, 0, 0, 0))],
        out_shape=[jax.ShapeDtypeStruct((b, l, GDN_OUT_DIM), BF16),
                   jax.ShapeDtypeStruct((b, GDN_HEADS, GDN_DIM, GDN_DIM), F32)],
        scratch_shapes=[pltpu.VMEM((nb, GDN_HEADS, GDN_DIM, GDN_DIM), F32),
                        pltpu.VMEM((nb, SUBLANES, GDN_QKV_DIM), F32)],
        compiler_params=_cparams(("parallel", "arbitrary")),
        name="gdn_prompt",
    )(proj, proj, proj, _rep8_rows(conv_w), alog_row, dtb_row, _rep8(norm_w))


def _ret_log_gamma(h):
    return math.log(1.0 - 2.0 ** (-5.0 - h))


def _rotary(t, cos_f, sin_s):
    n = t.shape[1]
    half = RET_KEY_DIM // 2
    first = (_iota(t.shape, 1) % RET_KEY_DIM) < half
    swapped = jnp.where(first, pltpu.roll(t, n - half, 1), pltpu.roll(t, half, 1))
    return t * cos_f + swapped * sin_s


def _group_norm_rows(x):
    xc = x - jnp.mean(x, axis=-1, keepdims=True)
    return xc * lax.rsqrt(jnp.mean(xc * xc, axis=-1, keepdims=True) + NORM_EPS)


def _ret_prompt_kernel(q_ref, k_ref, v_ref, gate_ref, cos_ref, sin_ref, nw_ref,
                       o_ref, s_out_ref, s_ref):
    c_idx = pl.program_id(1)

    @pl.when(c_idx == 0)
    def _():
        s_ref[...] = jnp.zeros_like(s_ref)

    nb, c = q_ref.shape[0], q_ref.shape[1]
    kd, vd = RET_KEY_DIM, RET_VAL_DIM
    ri, ci = _iota((c, c), 0), _iota((c, c), 1)
    causal = ri >= ci
    pos_col = _iota((c, 1), 0).astype(F32)
    lgs = [_ret_log_gamma(h) for h in range(RET_HEADS)]
    decay = [jnp.where(causal, jnp.exp((ri - ci).astype(F32) * lg), 0.0) for lg in lgs]
    grow = [jnp.exp((pos_col + 1.0) * lg) for lg in lgs]
    gtail = [jnp.exp((c - 1.0 - pos_col) * lg) for lg in lgs]
    chains = [(bb, h) for bb in range(nb) for h in range(RET_HEADS)]
    q = [_rotary(q_ref[bb], cos_ref[...], sin_ref[...]) for bb in range(nb)]
    k = [_rotary(k_ref[bb], cos_ref[...], sin_ref[...]) * (kd ** -0.5) for bb in range(nb)]
    qh = [q[bb][:, h * kd:(h + 1) * kd] for bb, h in chains]
    kh = [k[bb][:, h * kd:(h + 1) * kd] for bb, h in chains]
    vh = [v_ref[bb, :, h * vd:(h + 1) * vd] for bb, h in chains]
    s_old = [s_ref[bb, h] for bb, h in chains]
    att = [_dot(a, b, _NT) * decay[h] for a, b, (_, h) in zip(qh, kh, chains)]
    y = [_dot(a, v) + _dot(qq, s) * grow[h] for a, v, qq, s, (_, h) in zip(att, vh, qh, s_old, chains)]
    s_new = [s * math.exp(c * lgs[h]) + _dot(kk * gtail[h], v, _TN)
             for s, kk, v, (_, h) in zip(s_old, kh, vh, chains)]
    for (bb, h), yy, sn in zip(chains, y, s_new):
        s_ref[bb, h] = sn
        out = (_mul8(_group_norm_rows(yy), nw_ref[:, h * vd:(h + 1) * vd])
               * _silu(gate_ref[bb, :, h * vd:(h + 1) * vd]))
        o_ref[bb, :, h * vd:(h + 1) * vd] = out.astype(o_ref.dtype)

    @pl.when(c_idx == pl.num_programs(1) - 1)
    def _():
        s_out_ref[...] = s_ref[...]


def ret_prompt(proj, cos_f, sin_s, norm_w, chunk=128, nb=8):
    b, l, _ = proj.shape
    nb = min(nb, b)
    nc = l // chunk
    col = lambda width, off: pl.BlockSpec((nb, chunk, width), lambda i, j: (i, j, off // width))
    tab = pl.BlockSpec((chunk, RET_QK_DIM), lambda i, j: (j, 0))
    return pl.pallas_call(
        _ret_prompt_kernel,
        grid=(b // nb, nc),
        in_specs=[col(RET_QK_DIM, HYB_OFF_RQ), col(RET_QK_DIM, HYB_OFF_RK), col(RET_OUT_DIM, HYB_OFF_RV),
                  col(RET_OUT_DIM, HYB_OFF_RG), tab, tab, _resident((SUBLANES, RET_OUT_DIM))],
        out_specs=[pl.BlockSpec((nb, chunk, RET_OUT_DIM), lambda i, j: (i, j, 0)),
                   pl.BlockSpec((nb, RET_HEADS, RET_KEY_DIM, RET_VAL_DIM), lambda i, j: (i, 0, 0, 0))],
        out_shape=[jax.ShapeDtypeStruct((b, l, RET_OUT_DIM), BF16),
                   jax.ShapeDtypeStruct((b, RET_HEADS, RET_KEY_DIM, RET_VAL_DIM), F32)],
        scratch_shapes=[pltpu.VMEM((nb, RET_HEADS, RET_KEY_DIM, RET_VAL_DIM), F32)],
        compiler_params=_cparams(("parallel", "arbitrary")),
        name="ret_prompt",
    )(proj, proj, proj, proj, cos_f, sin_s, _rep8(norm_w))


def _split3(x):
    a = x.astype(BF16)
    r = x - a.astype(F32)
    b = r.astype(BF16)
    return a, b, (r - b.astype(F32)).astype(BF16)


def _spread(x, sel):
    return sum(jnp.dot(p, sel, preferred_element_type=F32) for p in _split3(x))


def _ssd_prompt_kernel(z_ref, x_ref, b_ref, c_ref, dt_ref, cw_ref, cb_ref, dtb_ref, alog_ref, dskip_ref,
                       nw_ref, sel_hd_ref, sel_c_ref, o_ref, s_out_ref, s_ref, tx_ref, tb_ref, tc_ref):
    c_idx = pl.program_id(1)

    @pl.when(c_idx == 0)
    def _():
        s_ref[...] = jnp.zeros_like(s_ref)
        tx_ref[...] = jnp.zeros_like(tx_ref)
        tb_ref[...] = jnp.zeros_like(tb_ref)
        tc_ref[...] = jnp.zeros_like(tc_ref)

    c = x_ref.shape[0]
    hd, gw, ns = SSM_HEAD_DIM, SSM_GROUP_COLS, SSM_STATE
    off_b, off_c = SSM_INNER, SSM_INNER + SSM_BC_DIM
    groups = range(SSM_GROUPS)

    br, cr = b_ref[...], c_ref[...]
    bm = _silu(_causal_conv_chunk(br, tb_ref[...], cw_ref[:, :, off_b:off_c], cb_ref[:, off_b:off_c]))
    cm = _silu(_causal_conv_chunk(cr, tc_ref[...], cw_ref[:, :, off_c:], cb_ref[:, off_c:]))
    tb_ref[...] = br[c - SUBLANES:c]
    tc_ref[...] = cr[c - SUBLANES:c]
    bg = [bm[:, g * ns:(g + 1) * ns] for g in groups]
    cg = [cm[:, g * ns:(g + 1) * ns] for g in groups]
    s_old = [s_ref[g] for g in groups]
    scores = [_dot(cc, bb, _NT) for cc, bb in zip(cg, bg)]
    y_inter = jnp.concatenate([_dot(cc, s) for cc, s in zip(cg, s_old)], axis=1)

    dt = _softplus(dt_ref[...] + dtb_ref[...])
    gcum = _cumsum_rows(dt * (-jnp.exp(alog_ref[...])))
    gcum_t = _transpose(gcum)
    per_col = _spread(jnp.concatenate([dt, gcum], axis=0), sel_hd_ref[...])
    dt_x, gc_x = per_col[:c], per_col[c:]
    gc_colb = _spread(gcum, sel_c_ref[...])
    ge8 = jnp.broadcast_to(gc_x[c - 1:c, :], (SUBLANES, SSM_INNER))

    xr = x_ref[...]
    x = _silu(_causal_conv_chunk(xr, tx_ref[...], cw_ref[:, :, :off_b], cb_ref[:, :off_b]))
    tx_ref[...] = xr[c - SUBLANES:c]

    xdt = x * dt_x
    xdec = xdt * jnp.exp(_add8(-gc_x, ge8))
    att = [scores[h // SSM_HPG] * _decay_matrix(gc_colb[:, h * c:(h + 1) * c], gcum_t[h:h + 1, :])
           for h in range(SSM_HEADS)]
    y_intra = jnp.concatenate([_dot(a, xdt[:, h * hd:(h + 1) * hd]) for h, a in enumerate(att)], axis=1)
    for g in groups:
        cols = slice(g * gw, (g + 1) * gw)
        s_ref[g] = _mul8(s_old[g], jnp.exp(ge8[:, cols])) + _dot(bg[g], xdec[:, cols], _TN)
    y = (y_intra + y_inter * jnp.exp(gc_x) + _mul8(x, dskip_ref[...])) * _silu(z_ref[...])
    for g in groups:
        cols = slice(g * gw, (g + 1) * gw)
        o_ref[:, cols] = _rms_rows(y[:, cols], nw_ref[:, cols]).astype(o_ref.dtype)

    @pl.when(c_idx == pl.num_programs(1) - 1)
    def _():
        for g in range(SSM_GROUPS):
            s_out_ref[g * SSM_HPG:(g + 1) * SSM_HPG] = _transpose(s_ref[g]).reshape(SSM_HPG, hd, ns)


def ssd_prompt(proj, conv_w, conv_b, dtb_row, alog_row, dskip_row, norm_w, chunk=128):
    b, l, _ = proj.shape
    assert chunk == LANES
    nc = l // chunk
    col = lambda width, off: pl.BlockSpec((None, chunk, width), lambda i, j: (i, j, off // width))
    row = _resident((1, LANES))
    lane = jnp.arange(LANES, dtype=jnp.int32)[:, None]
    sel_hd = (lane == jnp.arange(SSM_INNER, dtype=jnp.int32)[None, :] // SSM_HEAD_DIM).astype(BF16)
    sel_c = (lane == jnp.arange(SSM_HEADS * chunk, dtype=jnp.int32)[None, :] // chunk).astype(BF16)
    dskip8 = _rep8(jnp.repeat(dskip_row[0, :SSM_HEADS], SSM_HEAD_DIM))
    return pl.pallas_call(
        _ssd_prompt_kernel,
        grid=(b, nc),
        in_specs=[col(SSM_INNER, SSM_OFF_Z), col(SSM_INNER, SSM_OFF_X), col(SSM_BC_DIM, SSM_OFF_B),
                  col(SSM_BC_DIM, SSM_OFF_C), col(LANES, SSM_OFF_DT),
                  _resident((CONV_WIDTH, SUBLANES, SSM_CONV_DIM)), _resident((SUBLANES, SSM_CONV_DIM)),
                  row, row, _resident((SUBLANES, SSM_INNER)), _resident((SUBLANES, SSM_INNER)),
                  _resident(sel_hd.shape), _resident(sel_c.shape)],
        out_specs=[pl.BlockSpec((None, chunk, SSM_INNER), lambda i, j: (i, j, 0)),
                   pl.BlockSpec((None, SSM_HEADS, SSM_HEAD_DIM, SSM_STATE), lambda i, j: (i, 0, 0, 0))],
        out_shape=[jax.ShapeDtypeStruct((b, l, SSM_INNER), BF16),
                   jax.ShapeDtypeStruct((b, SSM_HEADS, SSM_HEAD_DIM, SSM_STATE), F32)],
        scratch_shapes=[pltpu.VMEM((SSM_GROUPS, SSM_STATE, SSM_GROUP_COLS), F32),
                        pltpu.VMEM((SUBLANES, SSM_INNER), F32),
                        pltpu.VMEM((SUBLANES, SSM_BC_DIM), F32), pltpu.VMEM((SUBLANES, SSM_BC_DIM), F32)],
        compiler_params=_cparams(("parallel", "arbitrary")),
        name="ssd_prompt",
    )(proj, proj, proj, proj, proj, _rep8_rows(conv_w), _rep8(conv_b), dtb_row, alog_row, dskip8,
      _rep8(norm_w), sel_hd, sel_c)


def _stacked_out(acc, n_inputs, out_index):
    return [pl.BlockSpec(memory_space=pl.ANY)], [acc], {n_inputs: out_index}


def _conv_step(x, buf_ref, w8_ref, bias8=None):
    acc = x * w8_ref[CONV_WIDTH - 1]
    for i in range(CONV_WIDTH - 1):
        acc = acc + buf_ref[i] * w8_ref[i]
    if bias8 is not None:
        acc = acc + bias8
    return acc


def _conv_state_kernel(buf_ref, x_ref, *rest):
    out_ref = rest[-1]
    for i in range(CONV_WIDTH - 2):
        out_ref[i] = buf_ref[i + 1]
    out_ref[CONV_WIDTH - 2] = x_ref[...]


def conv_state_update(conv_t, proj, col_off, layer, acc, wc):
    _, nbuf, b, n = conv_t.shape
    spec = pl.BlockSpec((None, nbuf, b, wc), lambda j: (layer, 0, 0, j))
    acc_specs, acc_args, aliases = _stacked_out(acc, 2, 0)
    return pl.pallas_call(
        _conv_state_kernel,
        grid=(n // wc,),
        in_specs=[spec, pl.BlockSpec((b, wc), lambda j: (0, col_off // wc + j))] + acc_specs,
        out_specs=spec,
        out_shape=jax.ShapeDtypeStruct(conv_t.shape, F32),
        input_output_aliases=aliases,
        compiler_params=_cparams(("parallel",)),
        name="conv_state",
    )(conv_t, proj, *acc_args)


def _gdn_decode_kernel(q_ref, k_ref, v_ref, z_ref, ba_ref, bq_ref, bk_ref, bv_ref, s_ref,
                       cwq_ref, cwk_ref, cwv_ref, alog_ref, dtb_ref, nw_ref, *rest):
    o_ref, s_out_ref = rest[-2:]
    hd = GDN_DIM
    tb = q_ref.shape[0]
    q = _silu(_conv_step(q_ref[...], bq_ref, cwq_ref))
    k = _silu(_conv_step(k_ref[...], bk_ref, cwk_ref))
    v = _silu(_conv_step(v_ref[...], bv_ref, cwv_ref))
    ba = ba_ref[...]
    beta = _sigmoid(ba)
    eg = jnp.exp(-jnp.exp(alog_ref[...]) * _softplus(ba + dtb_ref[...]))
    qn, kn = [], []
    for h in range(GDN_HEADS):
        qh, kh = q[:, h * hd:(h + 1) * hd], k[:, h * hd:(h + 1) * hd]
        qn.append(qh * lax.rsqrt(jnp.sum(qh * qh, axis=-1, keepdims=True) + NORM_EPS) * (hd ** -0.5))
        kn.append(kh * lax.rsqrt(jnp.sum(kh * kh, axis=-1, keepdims=True) + NORM_EPS))
    q_t = _transpose(jnp.concatenate(qn, axis=1))
    k_t = _transpose(jnp.concatenate(kn, axis=1))
    chains = [(b, h) for b in range(tb) for h in range(GDN_HEADS)]
    s_old = [s_ref[b, h] for b, h in chains]
    k_col = [k_t[h * hd:(h + 1) * hd, b:b + 1] for b, h in chains]
    q_col = [q_t[h * hd:(h + 1) * hd, b:b + 1] for b, h in chains]
    egb = [eg[b:b + 1, GDN_HEADS + h:GDN_HEADS + h + 1] for b, h in chains]
    ks = [jnp.sum(kc * s, axis=0, keepdims=True) for kc, s in zip(k_col, s_old)]
    delta = [beta[b:b + 1, h:h + 1] * (v[b:b + 1, h * hd:(h + 1) * hd] - e * x)
             for (b, h), e, x in zip(chains, egb, ks)]
    s_new = [e * s + kc * dl for e, s, kc, dl in zip(egb, s_old, k_col, delta)]
    for (b, h), sn in zip(chains, s_new):
        s_out_ref[b, h] = sn
    ys = [jnp.sum(qc * sn, axis=0, keepdims=True) for qc, sn in zip(q_col, s_new)]
    for h in range(GDN_HEADS):
        y = jnp.concatenate([ys[b * GDN_HEADS + h] for b in range(tb)], axis=0)
        y = _rms_rows(y, nw_ref[...]) * _silu(z_ref[:, h * hd:(h + 1) * hd])
        o_ref[:, h * hd:(h + 1) * hd] = y.astype(o_ref.dtype)


def gdn_decode(proj, state, conv_t, layer, acc, conv_w, alog_row, dtb_row, norm_w, tb=8):
    b = proj.shape[0]
    hd, od = GDN_DIM, GDN_OUT_DIM
    pcol = lambda off: pl.BlockSpec((tb, od), lambda i: (i, off // od))
    ccol = lambda part: pl.BlockSpec((None, CONV_WIDTH - 1, tb, od), lambda i: (layer, 0, i, part))
    wcol = lambda part: pl.BlockSpec((CONV_WIDTH, SUBLANES, od), lambda i: (0, 0, part))
    row = _resident((1, LANES))
    sspec = pl.BlockSpec((None, tb, GDN_HEADS, hd, hd), lambda i: (layer, i, 0, 0, 0))
    acc_specs, acc_args, aliases = _stacked_out(acc, 15, 1)
    assert tb == SUBLANES
    conv_w8 = _rep8_rows(conv_w)
    return pl.pallas_call(
        _gdn_decode_kernel,
        grid=(b // tb,),
        in_specs=[pcol(HYB_OFF_QKV), pcol(HYB_OFF_QKV + od), pcol(HYB_OFF_QKV + 2 * od),
                  pcol(HYB_OFF_Z), pl.BlockSpec((tb, LANES), lambda i: (i, HYB_OFF_BA // LANES)),
                  ccol(0), ccol(1), ccol(2), sspec,
                  wcol(0), wcol(1), wcol(2), row, row, _resident((SUBLANES, hd))] + acc_specs,
        out_specs=[pl.BlockSpec((tb, od), lambda i: (i, 0)), sspec],
        out_shape=[jax.ShapeDtypeStruct((b, od), BF16), jax.ShapeDtypeStruct(state.shape, F32)],
        input_output_aliases=aliases,
        compiler_params=_cparams(("parallel",)),
        name="gdn_decode",
    )(proj, proj, proj, proj, proj, conv_t, conv_t, conv_t, state,
      conv_w8, conv_w8, conv_w8, alog_row, dtb_row, _rep8(norm_w), *acc_args)


def _ret_decode_kernel(q_ref, k_ref, v_ref, gate_ref, s_ref, cos_ref, sin_ref, nw_ref, *rest):
    o_ref, s_out_ref = rest[-2:]
    q = _rotary(q_ref[...], cos_ref[...], sin_ref[...])
    k = _rotary(k_ref[...], cos_ref[...], sin_ref[...]) * (RET_KEY_DIM ** -0.5)
    q_t = _transpose(q)
    k_t = _transpose(k)
    tb = q.shape[0]
    kd, vd = RET_KEY_DIM, RET_VAL_DIM
    rows = []
    for b in range(tb):
        ys = []
        for h in range(RET_HEADS):
            gamma = math.exp(_ret_log_gamma(h))
            s = s_ref[b, h]
            s_new = s * gamma + k_t[h * kd:(h + 1) * kd, b:b + 1] * v_ref[b:b + 1, h * vd:(h + 1) * vd]
            s_out_ref[b, h] = s_new
            ys.append(jnp.sum(q_t[h * kd:(h + 1) * kd, b:b + 1] * s_new, axis=0, keepdims=True))
        rows.append(jnp.concatenate(ys, axis=1))
    y = jnp.concatenate(rows, axis=0)
    for h in range(RET_HEADS):
        yh = (_mul8(_group_norm_rows(y[:, h * vd:(h + 1) * vd]), nw_ref[:, h * vd:(h + 1) * vd])
              * _silu(gate_ref[:, h * vd:(h + 1) * vd]))
        o_ref[:, h * vd:(h + 1) * vd] = yh.astype(o_ref.dtype)


def ret_decode(proj, state, layer, acc, cos_f, sin_s, norm_w, tb=8):
    b = proj.shape[0]
    pcol = lambda width, off: pl.BlockSpec((tb, width), lambda i: (i, off // width))
    sspec = pl.BlockSpec((None, tb, RET_HEADS, RET_KEY_DIM, RET_VAL_DIM), lambda i: (layer, i, 0, 0, 0))
    acc_specs, acc_args, aliases = _stacked_out(acc, 8, 1)
    return pl.pallas_call(
        _ret_decode_kernel,
        grid=(b // tb,),
        in_specs=[pcol(RET_QK_DIM, HYB_OFF_RQ), pcol(RET_QK_DIM, HYB_OFF_RK), pcol(RET_OUT_DIM, HYB_OFF_RV),
                  pcol(RET_OUT_DIM, HYB_OFF_RG), sspec, _resident((1, RET_QK_DIM)), _resident((1, RET_QK_DIM)),
                  _resident((SUBLANES, RET_OUT_DIM))] + acc_specs,
        out_specs=[pl.BlockSpec((tb, RET_OUT_DIM), lambda i: (i, 0)), sspec],
        out_shape=[jax.ShapeDtypeStruct((b, RET_OUT_DIM), BF16), jax.ShapeDtypeStruct(state.shape, F32)],
        input_output_aliases=aliases,
        compiler_params=_cparams(("parallel",)),
        name="ret_decode",
    )(proj, proj, proj, proj, state, cos_f, sin_s, _rep8(norm_w), *acc_args)


def _ssd_decode_kernel(z_ref, x_ref, b_ref, c_ref, dt_ref, bx_ref, bb_ref, bc_ref, s_ref,
                       cwx_ref, cwb_ref, cwc_ref, cbx_ref, cbb_ref, cbc_ref,
                       dtb_ref, alog_ref, dskip_ref, nw_ref, *rest):
    o_ref, s_out_ref = rest[-2:]
    hd, gw, ns = SSM_HEAD_DIM, SSM_GROUP_COLS, SSM_STATE
    tb, width = x_ref.shape
    ng = width // gw
    first_head = pl.program_id(1) * (ng * SSM_HPG)
    x = _silu(_conv_step(x_ref[...], bx_ref, cwx_ref, cbx_ref[...]))
    bm = _silu(_conv_step(b_ref[...], bb_ref, cwb_ref, cbb_ref[...]))
    cm = _silu(_conv_step(c_ref[...], bc_ref, cwc_ref, cbc_ref[...]))
    dt = _softplus(dt_ref[...] + dtb_ref[...])
    spread = (_iota((LANES, width), 0) == first_head + _iota((LANES, width), 1) // hd).astype(BF16)
    per_head = jnp.concatenate([dt, -jnp.exp(alog_ref[...]), dskip_ref[...],
                                jnp.zeros((2 * SUBLANES - tb - 2, LANES), F32)], axis=0)
    ph_hi, ph_lo = _split(per_head)
    per_col = (jnp.dot(ph_hi, spread, preferred_element_type=F32)
               + jnp.dot(ph_lo, spread, preferred_element_type=F32))
    dt_x, a_x, dskip_x = per_col[0:tb], per_col[tb:tb + 1], per_col[tb + 1:tb + 2]
    eg_x = jnp.exp(dt_x * a_x)
    xdt_t = (x * dt_x).T
    row_id = _iota((tb, ns), 0)
    pairs = [(b, g) for b in range(tb) for g in range(ng)]
    outer = [_dot3(_split(xdt_t[g * gw:(g + 1) * gw, :]),
                   _split(jnp.where(row_id == b, bm[:, g * ns:(g + 1) * ns], 0.0))) for b, g in pairs]
    for (b, g), out in zip(pairs, outer):
        for hl in range(SSM_HPG):
            h = g * SSM_HPG + hl
            s_out_ref[b, h] = s_ref[b, h] * eg_x[b:b + 1, h * hd:h * hd + 1] + out[hl * hd:(hl + 1) * hd, :]
    ys = [_dot(cm[b:b + 1, g * ns:(g + 1) * ns],
               s_out_ref[b, g * SSM_HPG:(g + 1) * SSM_HPG].reshape(gw, ns), _NT) for b, g in pairs]
    y = jnp.concatenate([jnp.concatenate(ys[b * ng:(b + 1) * ng], axis=1) for b in range(tb)], axis=0)
    y = (y + dskip_x * x) * _silu(z_ref[...])
    for g in range(ng):
        cols = slice(g * gw, (g + 1) * gw)
        o_ref[:, cols] = _rms_rows(y[:, cols], nw_ref[:, cols]).astype(o_ref.dtype)


def ssd_decode(proj, state_t, conv_t, layer, acc, conv_w, conv_b, dtb_row, alog_row, dskip_row, norm_w, tb=8, ng=4):
    b = proj.shape[0]
    gw = ng * SSM_GROUP_COLS
    ns = ng * SSM_STATE
    assert tb == SUBLANES
    cw8, cb8, nw8 = _rep8_rows(conv_w), _rep8(conv_b), _rep8(norm_w)
    pcol = lambda width, off: pl.BlockSpec((tb, width), lambda i, g: (i, off // width + g))
    ccol = lambda width, off: pl.BlockSpec((None, CONV_WIDTH - 1, tb, width),
                                           lambda i, g: (layer, 0, i, off // width + g))
    wcol = lambda width, off: pl.BlockSpec((CONV_WIDTH, SUBLANES, width), lambda i, g: (0, 0, off // width + g))
    rcol = lambda width, off: pl.BlockSpec((SUBLANES, width), lambda i, g: (0, off // width + g))
    row = _resident((1, LANES))
    sspec = pl.BlockSpec((None, tb, ng * SSM_HPG, SSM_HEAD_DIM, SSM_STATE), lambda i, g: (layer, i, g, 0, 0))
    off_b, off_c = SSM_INNER, SSM_INNER + SSM_BC_DIM
    acc_specs, acc_args, aliases = _stacked_out(acc, 19, 1)
    return pl.pallas_call(
        _ssd_decode_kernel,
        grid=(b // tb, SSM_GROUPS // ng),
        in_specs=[pcol(gw, SSM_OFF_Z), pcol(gw, SSM_OFF_X), pcol(ns, SSM_OFF_B), pcol(ns, SSM_OFF_C),
                  pl.BlockSpec((tb, LANES), lambda i, g: (i, SSM_OFF_DT // LANES)),
                  ccol(gw, 0), ccol(ns, off_b), ccol(ns, off_c), sspec,
                  wcol(gw, 0), wcol(ns, off_b), wcol(ns, off_c),
                  rcol(gw, 0), rcol(ns, off_b), rcol(ns, off_c),
                  row, row, row, rcol(gw, 0)] + acc_specs,
        out_specs=[pl.BlockSpec((tb, gw), lambda i, g: (i, g)), sspec],
        out_shape=[jax.ShapeDtypeStruct((b, SSM_INNER), BF16), jax.ShapeDtypeStruct(state_t.shape, F32)],
        input_output_aliases=aliases,
        compiler_params=_cparams(("parallel", "parallel")),
        name="ssd_decode",
    )(proj, proj, proj, proj, proj, conv_t, conv_t, conv_t, state_t,
      cw8, cw8, cw8, cb8, cb8, cb8, dtb_row, alog_row, dskip_row, nw8, *acc_args)


def _lane_row(vals, offset):
    return jnp.zeros((1, LANES), F32).at[0, offset:offset + vals.shape[0]].set(vals.astype(F32))


def _prep_hyb_w_in(w):
    sizes = (GDN_QKV_DIM, GDN_OUT_DIM, GDN_HEADS, GDN_HEADS, RET_QK_DIM, RET_QK_DIM, RET_OUT_DIM, RET_OUT_DIM)
    offs = [0]
    for s in sizes:
        offs.append(offs[-1] + s)
    part = lambda i: w[..., offs[i]:offs[i + 1]]
    cols = [part(0), part(1), part(4), part(5), part(6), part(7), part(2), part(3)]
    pad = jnp.zeros(w.shape[:-1] + (HYB_N - offs[-1],), w.dtype)
    return jnp.concatenate(cols + [pad], axis=-1).astype(BF16)


def _prep_ssm_w_in(w):
    pad = jnp.zeros(w.shape[:-1] + (SSM_N - w.shape[-1],), w.dtype)
    return jnp.concatenate([w, pad], axis=-1).astype(BF16)


def _rope_tables(pos):
    half = RET_KEY_DIM // 2
    inv_freq = ROPE_BASE ** (-jnp.arange(half, dtype=F32) / half)
    ang = pos.astype(F32)[:, None] * inv_freq[None, :]
    cos, sin = jnp.cos(ang), jnp.sin(ang)
    cos_f = jnp.tile(jnp.concatenate([cos, cos], axis=1), (1, RET_HEADS))
    sin_s = jnp.tile(jnp.concatenate([-sin, sin], axis=1), (1, RET_HEADS))
    return cos_f, sin_s


def _trunk(x, pos, states, params, prompt):
    bsz, l, d = x.shape
    m = bsz * l
    tm = 512 if m % 512 == 0 else m
    x2 = x.reshape(m, d)
    cos_f, sin_s = _rope_tables(pos)
    new_gdn, new_gdn_conv, new_ret, new_ssm, new_ssm_conv = [], [], [], [], []
    if not prompt:
        s_gdn, c_gdn, s_ret, s_ssm, c_ssm = states
        c_gdn_t, c_ssm_t = jnp.swapaxes(c_gdn, 1, 2), jnp.swapaxes(c_ssm, 1, 2)
        s_ssm_t = jnp.swapaxes(s_ssm, 3, 4)
        acc_gdn, acc_gdn_conv, acc_ret, acc_ssm, acc_ssm_conv = (
            jnp.zeros_like(a) for a in (s_gdn, c_gdn_t, s_ret, s_ssm_t, c_ssm_t))
    for layer in range(DEPTH):
        i = layer // 2
        p = params
        if layer % 2 == 0:
            proj = norm_matmul(x2, p["norm_mix8"], layer, p["w_in_hyb"], i, tm)
            alog_row = _lane_row(p["gdn_a_log"][i], GDN_HEADS)
            dtb_row = _lane_row(p["gdn_dt_bias"][i], GDN_HEADS)
            if prompt:
                proj3 = proj.reshape(bsz, l, HYB_N)
                o_a, sg = gdn_prompt(proj3, p["gdn_conv_w"][i], alog_row, dtb_row, p["gdn_norm_w"][i])
                o_b, sr = ret_prompt(proj3, cos_f, sin_s, p["ret_norm_w"][i])
                cg = proj3[:, l - (CONV_WIDTH - 1):, :GDN_QKV_DIM]
                o_a, o_b = o_a.reshape(m, GDN_OUT_DIM), o_b.reshape(m, RET_OUT_DIM)
                new_gdn.append(sg)
                new_gdn_conv.append(cg)
                new_ret.append(sr)
            else:
                o_a, acc_gdn = gdn_decode(proj, s_gdn, c_gdn_t, i, acc_gdn, p["gdn_conv_w"][i], alog_row,
                                          dtb_row, p["gdn_norm_w"][i])
                acc_gdn_conv = conv_state_update(c_gdn_t, proj, HYB_OFF_QKV, i, acc_gdn_conv, GDN_OUT_DIM)
                o_b, acc_ret = ret_decode(proj, s_ret, i, acc_ret, cos_f, sin_s, p["ret_norm_w"][i])
            acts, w_out = [o_a, o_b], p["w_out_hyb"]
        else:
            proj = norm_matmul(x2, p["norm_mix8"], layer, p["w_in_ssm"], i, tm)
            dtb_row = _lane_row(p["ssm_dt_bias"][i], 0)
            alog_row = _lane_row(p["ssm_a_log"][i], 0)
            dskip_row = _lane_row(p["ssm_d"][i], 0)
            if prompt:
                proj3 = proj.reshape(bsz, l, SSM_N)
                y, ss = ssd_prompt(proj3, p["ssm_conv_w"][i], p["ssm_conv_b"][i], dtb_row, alog_row,
                                   dskip_row, p["ssm_norm_w"][i])
                cs = proj3[:, l - (CONV_WIDTH - 1):, SSM_OFF_X:SSM_OFF_X + SSM_CONV_DIM]
                y = y.reshape(m, SSM_INNER)
                new_ssm.append(ss)
                new_ssm_conv.append(cs)
            else:
                y, acc_ssm = ssd_decode(proj, s_ssm_t, c_ssm_t, i, acc_ssm, p["ssm_conv_w"][i],
                                        p["ssm_conv_b"][i], dtb_row, alog_row, dskip_row, p["ssm_norm_w"][i])
                acc_ssm_conv = conv_state_update(c_ssm_t, proj, SSM_OFF_X, i, acc_ssm_conv, D_MODEL)
            acts, w_out = [y], p["w_out_ssm"]
        x2 = post_mixer(acts, w_out, i, x2, p["norm_mlp8"], layer, p["mlp_w1"], p["mlp_w2"],
                        p["norm_final8"], layer == DEPTH - 1, tm)
    y_out = x2.reshape(bsz, l, d)
    if prompt:
        return (y_out, jnp.stack(new_gdn), jnp.stack(new_gdn_conv), jnp.stack(new_ret),
                jnp.swapaxes(jnp.stack(new_ssm), 3, 4), jnp.stack(new_ssm_conv))
    return (y_out, acc_gdn, jnp.swapaxes(acc_gdn_conv, 1, 2), acc_ret,
            jnp.swapaxes(acc_ssm, 3, 4), jnp.swapaxes(acc_ssm_conv, 1, 2))


def kernel(x_prompt, x_sample, state_gdn, state_gdn_conv, state_ret, state_ssm, state_ssm_conv, norm_mix, norm_mlp, norm_final, w_in_hyb, gdn_conv_w, gdn_a_log, gdn_dt_bias, gdn_norm_w, ret_norm_w, w_out_hyb, w_in_ssm, ssm_conv_w, ssm_conv_b, ssm_dt_bias, ssm_a_log, ssm_d, ssm_norm_w, w_out_ssm, mlp_w1, mlp_w2):
    params = dict(
        norm_mix8=_rep8_rows(norm_mix), norm_mlp8=_rep8_rows(norm_mlp), norm_final8=_rep8(norm_final),
        w_in_hyb=_prep_hyb_w_in(w_in_hyb),
        gdn_conv_w=gdn_conv_w, gdn_a_log=gdn_a_log, gdn_dt_bias=gdn_dt_bias, gdn_norm_w=gdn_norm_w,
        ret_norm_w=ret_norm_w, w_out_hyb=w_out_hyb.astype(BF16),
        w_in_ssm=_prep_ssm_w_in(w_in_ssm),
        ssm_conv_w=ssm_conv_w, ssm_conv_b=ssm_conv_b, ssm_dt_bias=ssm_dt_bias, ssm_a_log=ssm_a_log,
        ssm_d=ssm_d, ssm_norm_w=ssm_norm_w, w_out_ssm=w_out_ssm.astype(BF16),
        mlp_w1=mlp_w1.astype(BF16), mlp_w2=mlp_w2.astype(BF16))
    lp, ls = x_prompt.shape[1], x_sample.shape[1]
    pos_prompt = jnp.arange(lp, dtype=jnp.int32)
    pos_sample = PAST_LEN + jnp.arange(ls, dtype=jnp.int32)
    y_p, p_gdn, p_gdn_conv, p_ret, p_ssm, p_ssm_conv = _trunk(
        x_prompt, pos_prompt, (None,) * 5, params, prompt=True)
    y_s, s_gdn, s_gdn_conv, s_ret, s_ssm, s_ssm_conv = _trunk(
        x_sample, pos_sample, (state_gdn, state_gdn_conv, state_ret, state_ssm, state_ssm_conv),
        params, prompt=False)
    return (y_p, y_s, p_gdn, p_gdn_conv, p_ret, p_ssm, p_ssm_conv,
            s_gdn, s_gdn_conv, s_ret, s_ssm, s_ssm_conv)
```

```python
import functools
import math

import jax
import jax.numpy as jnp
from jax import lax
from jax.experimental import pallas as pl
from jax.experimental.pallas import tpu as pltpu

F32 = jnp.float32
BF16 = jnp.bfloat16

D_MODEL = 1024
DEPTH = 4
CONV_WIDTH = 4
NORM_EPS = 1e-6
PAST_LEN = 16384

GDN_HEADS = 4
GDN_DIM = 128
GDN_QKV_DIM = 3 * GDN_HEADS * GDN_DIM
GDN_OUT_DIM = GDN_HEADS * GDN_DIM

RET_HEADS = 4
RET_KEY_DIM = 64
RET_VAL_DIM = 128
RET_QK_DIM = RET_HEADS * RET_KEY_DIM
RET_OUT_DIM = RET_HEADS * RET_VAL_DIM
ROPE_BASE = 10000.0

SSM_INNER = 2 * D_MODEL
SSM_HEAD_DIM = 64
SSM_HEADS = SSM_INNER // SSM_HEAD_DIM
SSM_GROUPS = 4
SSM_HPG = SSM_HEADS // SSM_GROUPS
SSM_STATE = 128
SSM_GROUP_COLS = SSM_HPG * SSM_HEAD_DIM
SSM_BC_DIM = SSM_GROUPS * SSM_STATE
SSM_CONV_DIM = SSM_INNER + 2 * SSM_BC_DIM
MLP_HIDDEN = 4 * D_MODEL

LANES = 128
SUBLANES = 8

HYB_OFF_QKV = 0
HYB_OFF_Z = GDN_QKV_DIM
HYB_OFF_RQ = HYB_OFF_Z + GDN_OUT_DIM
HYB_OFF_RK = HYB_OFF_RQ + RET_QK_DIM
HYB_OFF_RV = HYB_OFF_RK + RET_QK_DIM
HYB_OFF_RG = HYB_OFF_RV + RET_OUT_DIM
HYB_OFF_BA = HYB_OFF_RG + RET_OUT_DIM
HYB_N = 3840
SSM_OFF_Z = 0
SSM_OFF_X = SSM_INNER
SSM_OFF_B = SSM_OFF_X + SSM_INNER
SSM_OFF_C = SSM_OFF_B + SSM_BC_DIM
SSM_OFF_DT = SSM_OFF_C + SSM_BC_DIM
SSM_N = 5376
PROJ_TN = 768

VMEM_LIMIT = 56 * 1024 * 1024

_NN = (((1,), (0,)), ((), ()))
_NT = (((1,), (1,)), ((), ()))
_TN = (((0,), (0,)), ((), ()))


def _dot(a, b, dims=_NN):
    return lax.dot_general(a.astype(BF16), b.astype(BF16), dims, preferred_element_type=F32)


def _dot_f32(a, b, dims=_NN):
    return lax.dot_general(a, b, dims, precision=lax.Precision.HIGHEST, preferred_element_type=F32)


def _sigmoid(x):
    return 0.5 + 0.5 * jnp.tanh(0.5 * x)


def _silu(x):
    t = 0.5 * x
    return t + t * jnp.tanh(t)


def _softplus(x):
    return jnp.maximum(x, 0.0) + jnp.log(1.0 + jnp.exp(-jnp.abs(x)))


def _iota(shape, dim):
    return lax.broadcasted_iota(jnp.int32, shape, dim)


def _transpose(x):
    return x.T


def _cparams(sem):
    return pltpu.CompilerParams(dimension_semantics=sem, vmem_limit_bytes=VMEM_LIMIT)


def _resident(shape):
    nd = len(shape)
    return pl.BlockSpec(shape, lambda *_: (0,) * nd, pipeline_mode=pl.Buffered(1))


def _resident_layer(shape, layer, rows=None, row_block=0):
    rows = shape[1] if rows is None else rows
    return pl.BlockSpec((None, rows, shape[2]), lambda *_: (layer, row_block, 0), pipeline_mode=pl.Buffered(1))


def _rep8(v):
    v = v.reshape(1, -1)
    return jnp.broadcast_to(v, (SUBLANES, v.shape[1]))


def _rep8_rows(w):
    return jnp.broadcast_to(w[:, None, :], (w.shape[0], SUBLANES, w.shape[1]))


def _mul8(x, r8):
    r, n = x.shape
    return (x.reshape(r // SUBLANES, SUBLANES, n) * r8).reshape(r, n)


def _add8(x, r8):
    r, n = x.shape
    return (x.reshape(r // SUBLANES, SUBLANES, n) + r8).reshape(r, n)


def _rms_rows(x, gain8):
    return _mul8(x * lax.rsqrt(jnp.mean(x * x, axis=-1, keepdims=True) + NORM_EPS), gain8)


def _norm_matmul_kernel(x_ref, g_ref, w_ref, o_ref, *, tn):
    xn = _rms_rows(x_ref[...], g_ref[...]).astype(BF16)
    for j in range(w_ref.shape[1] // tn):
        o_ref[:, j * tn:(j + 1) * tn] = jnp.dot(xn, w_ref[:, j * tn:(j + 1) * tn],
                                                preferred_element_type=F32)


def norm_matmul(x, gains8, layer, w, w_layer, tm):
    m, d = x.shape
    n = w.shape[2]
    return pl.pallas_call(
        functools.partial(_norm_matmul_kernel, tn=PROJ_TN),
        grid=(m // tm,),
        in_specs=[pl.BlockSpec((tm, d), lambda i: (i, 0)), _resident_layer(gains8.shape, layer),
                  _resident_layer(w.shape, w_layer)],
        out_specs=pl.BlockSpec((tm, n), lambda i: (i, 0)),
        out_shape=jax.ShapeDtypeStruct((m, n), F32),
        compiler_params=_cparams(("parallel",)),
        name="norm_matmul",
    )(x, gains8, w)


def _post_mixer_kernel(*refs, n_in, tf, final_norm):
    a_refs, w_refs = refs[:n_in], refs[n_in:2 * n_in]
    r_ref, g_ref, w1_ref, w2_ref, gf_ref, o_ref = refs[2 * n_in:]
    x = r_ref[...]
    for a_ref, w_ref in zip(a_refs, w_refs):
        x = x + jnp.dot(a_ref[...], w_ref[...], preferred_element_type=F32)
    o_ref[...] = x
    x = o_ref[...]
    xn = _rms_rows(x, g_ref[...]).astype(BF16)
    acc = x
    for j in range(w1_ref.shape[1] // tf):
        h = jnp.dot(xn, w1_ref[:, j * tf:(j + 1) * tf], preferred_element_type=F32)
        h = jnp.maximum(h, 0.0)
        acc = acc + jnp.dot((h * h).astype(BF16), w2_ref[j * tf:(j + 1) * tf, :],
                            preferred_element_type=F32)
    if final_norm:
        acc = _rms_rows(acc, gf_ref[...])
    o_ref[...] = acc


def post_mixer(acts, w_out, w_layer, res, gains8, layer, w1, w2, final8, final_norm, tm):
    m, d = res.shape
    n_in = len(acts)
    k_rows = acts[0].shape[1]
    assert all(a.shape[1] == k_rows for a in acts) and n_in * k_rows == w_out.shape[1]
    in_specs = ([pl.BlockSpec((tm, k_rows), lambda i: (i, 0)) for _ in acts]
                + [_resident_layer(w_out.shape, w_layer, k_rows, j) for j in range(n_in)]
                + [pl.BlockSpec((tm, d), lambda i: (i, 0)), _resident_layer(gains8.shape, layer),
                   _resident_layer(w1.shape, layer), _resident_layer(w2.shape, layer), _resident((SUBLANES, d))])
    return pl.pallas_call(
        functools.partial(_post_mixer_kernel, n_in=n_in, tf=1024, final_norm=final_norm),
        grid=(m // tm,),
        in_specs=in_specs,
        out_specs=pl.BlockSpec((tm, d), lambda i: (i, 0)),
        out_shape=jax.ShapeDtypeStruct((m, d), F32),
        compiler_params=_cparams(("parallel",)),
        name="post_mixer",
    )(*acts, *([w_out] * n_in), res, gains8, w1, w2, final8)


def _causal_conv_chunk(x, tail, w8, bias8=None):
    c, n = x.shape
    g = c // SUBLANES
    x3 = x.reshape(g, SUBLANES, n)
    acc = x3 * w8[CONV_WIDTH - 1]
    row = _iota((SUBLANES, n), 0)
    for k in range(1, CONV_WIDTH):
        r = pltpu.roll(x3, k, 1)
        prev = jnp.concatenate([pltpu.roll(tail, k, 0)[None], r[:g - 1]], axis=0)
        acc = acc + jnp.where(row < k, prev, r) * w8[CONV_WIDTH - 1 - k]
    if bias8 is not None:
        acc = acc + bias8
    return acc.reshape(c, n)


def _cumsum_rows(g):
    c = g.shape[0]
    tri = (_iota((c, c), 0) >= _iota((c, c), 1)).astype(F32)
    return _dot_f32(tri, g)


def _decay_matrix(gc_col, gc_row, exp=jnp.exp):
    c = gc_col.shape[0]
    causal = _iota((c, c), 0) >= _iota((c, c), 1)
    return jnp.where(causal, exp(gc_col - gc_row), 0.0)


def _split(x):
    hi = x.astype(BF16)
    return hi, (x - hi.astype(F32)).astype(BF16)


def _dot3(a, b, dims=_NN):
    (ah, al), (bh, bl) = a, b
    d = lambda p, q: lax.dot_general(p, q, dims, preferred_element_type=F32)
    return d(ah, bh) + (d(ah, bl) + d(al, bh))


def _unit_lower_inverses(n_list):
    c = n_list[0].shape[0]
    ri, ci = _iota((c, c), 0), _iota((c, c), 1)
    eye = (ri == ci).astype(F32)
    blk = 16
    same = (ri // blk) == (ci // blk)
    p = [jnp.where(same, -n, 0.0).astype(BF16) for n in n_list]
    inv = [eye + x for x in p]
    for _ in range(3):
        p = [_dot(x, x).astype(BF16) for x in p]
        inv = [i + _dot(i, x) for i, x in zip(inv, p)]
    while blk < c:
        pair = ((ri // (2 * blk)) == (ci // (2 * blk))) & ((ri // blk) != (ci // blk))
        inv_b = [i.astype(BF16) for i in inv]
        t = [_dot(jnp.where(pair, n, 0.0), i) for n, i in zip(n_list, inv_b)]
        inv = [i - _dot(i_b, x) for i, i_b, x in zip(inv, inv_b, t)]
        blk *= 2
    return inv


def _gdn_prompt_kernel(qkv_ref, z_ref, ba_ref, cw_ref, alog_ref, dtb_ref, nw_ref,
                       o_ref, s_out_ref, s_ref, tail_ref):
    c_idx = pl.program_id(1)

    @pl.when(c_idx == 0)
    def _():
        s_ref[...] = jnp.zeros_like(s_ref)
        tail_ref[...] = jnp.zeros_like(tail_ref)

    nb, c = qkv_ref.shape[0], qkv_ref.shape[1]
    hd = GDN_DIM
    ri, ci = _iota((c, c), 0), _iota((c, c), 1)
    strict = ri > ci
    chains = [(bb, h) for bb in range(nb) for h in range(GDN_HEADS)]

    conv, beta, gcum, gcum_t = [], [], [], []
    for bb in range(nb):
        x = qkv_ref[bb]
        conv.append(_silu(_causal_conv_chunk(x, tail_ref[bb], cw_ref[...])))
        tail_ref[bb] = x[c - SUBLANES:c]
        ba = ba_ref[bb]
        beta.append(_sigmoid(ba))
        gc = _cumsum_rows(-jnp.exp(alog_ref[...]) * _softplus(ba + dtb_ref[...]))
        gcum.append(gc)
        gcum_t.append(_transpose(gc))

    q, k, v, b_col, gc_col, g_end, decay = [], [], [], [], [], [], []
    for bb, h in chains:
        qh = conv[bb][:, h * hd:(h + 1) * hd]
        kh = conv[bb][:, GDN_OUT_DIM + h * hd:GDN_OUT_DIM + (h + 1) * hd]
        q.append(qh * lax.rsqrt(jnp.sum(qh * qh, axis=-1, keepdims=True) + NORM_EPS) * (hd ** -0.5))
        k.append(kh * lax.rsqrt(jnp.sum(kh * kh, axis=-1, keepdims=True) + NORM_EPS))
        v.append(conv[bb][:, 2 * GDN_OUT_DIM + h * hd:2 * GDN_OUT_DIM + (h + 1) * hd])
        b_col.append(beta[bb][:, h:h + 1])
        gc_col.append(gcum[bb][:, GDN_HEADS + h:GDN_HEADS + h + 1])
        g_end.append(gcum[bb][c - 1:c, GDN_HEADS + h:GDN_HEADS + h + 1])
        decay.append(_decay_matrix(gc_col[-1], gcum_t[bb][GDN_HEADS + h:GDN_HEADS + h + 1, :]))
    kk = [_dot(x, x, _NT) for x in k]
    qk = [_dot(x, y, _NT) * d for x, y, d in zip(q, k, decay)]
    t_inv = _unit_lower_inverses([jnp.where(strict, b * x * d, 0.0) for b, x, d in zip(b_col, kk, decay)])
    rhs = [jnp.concatenate([vv * b, kx * (b * jnp.exp(g))], axis=1)
           for vv, kx, b, g in zip(v, k, b_col, gc_col)]
    sol = [_dot(t, r) for t, r in zip(t_inv, rhs)]
    s_old = [s_ref[bb, h] for bb, h in chains]
    delta = [x[:, :hd] - _dot(x[:, hd:], s) for x, s in zip(sol, s_old)]
    y = [_dot(x * jnp.exp(g), s) + _dot(a, dl)
         for x, g, s, a, dl in zip(q, gc_col, s_old, qk, delta)]
    s_new = [s * jnp.exp(ge) + _dot(kx * jnp.exp(ge - g), dl, _TN)
             for s, ge, kx, g, dl in zip(s_old, g_end, k, gc_col, delta)]
    for (bb, h), sn, yy in zip(chains, s_new, y):
        s_ref[bb, h] = sn
        out = _rms_rows(yy, nw_ref[...]) * _silu(z_ref[bb, :, h * hd:(h + 1) * hd])
        o_ref[bb, :, h * hd:(h + 1) * hd] = out.astype(o_ref.dtype)

    @pl.when(c_idx == pl.num_programs(1) - 1)
    def _():
        s_out_ref[...] = s_ref[...]


def gdn_prompt(proj, conv_w, alog_row, dtb_row, norm_w, chunk=64, nb=8):
    b, l, _ = proj.shape
    nb = min(nb, b)
    nc = l // chunk
    col = lambda width, off: pl.BlockSpec((nb, chunk, width), lambda i, j: (i, j, off // width))
    return pl.pallas_call(
        _gdn_prompt_kernel,
        grid=(b // nb, nc),
        in_specs=[col(GDN_QKV_DIM, HYB_OFF_QKV), col(GDN_OUT_DIM, HYB_OFF_Z), col(LANES, HYB_OFF_BA),
                  _resident((CONV_WIDTH, SUBLANES, GDN_QKV_DIM)), _resident((1, LANES)), _resident((1, LANES)),
                  _resident((SUBLANES, GDN_DIM))],
        out_specs=[pl.BlockSpec((nb, chunk, GDN_OUT_DIM), lambda i, j: (i, j, 0)),
                   pl.BlockSpec((nb, GDN_HEADS, GDN_DIM, GDN_DIM), lambda i, j: (i, 0, 0, 0))],
        out_shape=[jax.ShapeDtypeStruct((b, l, GDN_OUT_DIM), BF16),
                   jax.ShapeDtypeStruct((b, GDN_HEADS, GDN_DIM, GDN_DIM), F32)],
        scratch_shapes=[pltpu.VMEM((nb, GDN_HEADS, GDN_DIM, GDN_DIM), F32),
                        pltpu.VMEM((nb, SUBLANES, GDN_QKV_DIM), F32)],
        compiler_params=_cparams(("parallel", "arbitrary")),
        name="gdn_prompt",
    )(proj, proj, proj, _rep8_rows(conv_w), alog_row, dtb_row, _rep8(norm_w))


def _ret_log_gamma(h):
    return math.log(1.0 - 2.0 ** (-5.0 - h))


def _rotary(t, cos_f, sin_s):
    n = t.shape[1]
    half = RET_KEY_DIM // 2
    first = (_iota(t.shape, 1) % RET_KEY_DIM) < half
    swapped = jnp.where(first, pltpu.roll(t, n - half, 1), pltpu.roll(t, half, 1))
    return t * cos_f + swapped * sin_s


def _group_norm_rows(x):
    xc = x - jnp.mean(x, axis=-1, keepdims=True)
    return xc * lax.rsqrt(jnp.mean(xc * xc, axis=-1, keepdims=True) + NORM_EPS)


def _ret_prompt_kernel(q_ref, k_ref, v_ref, gate_ref, cos_ref, sin_ref, nw_ref,
                       o_ref, s_out_ref, s_ref):
    c_idx = pl.program_id(1)

    @pl.when(c_idx == 0)
    def _():
        s_ref[...] = jnp.zeros_like(s_ref)

    nb, c = q_ref.shape[0], q_ref.shape[1]
    kd, vd = RET_KEY_DIM, RET_VAL_DIM
    ri, ci = _iota((c, c), 0), _iota((c, c), 1)
    causal = ri >= ci
    pos_col = _iota((c, 1), 0).astype(F32)
    lgs = [_ret_log_gamma(h) for h in range(RET_HEADS)]
    decay = [jnp.where(causal, jnp.exp((ri - ci).astype(F32) * lg), 0.0) for lg in lgs]
    grow = [jnp.exp((pos_col + 1.0) * lg) for lg in lgs]
    gtail = [jnp.exp((c - 1.0 - pos_col) * lg) for lg in lgs]
    chains = [(bb, h) for bb in range(nb) for h in range(RET_HEADS)]
    q = [_rotary(q_ref[bb], cos_ref[...], sin_ref[...]) for bb in range(nb)]
    k = [_rotary(k_ref[bb], cos_ref[...], sin_ref[...]) * (kd ** -0.5) for bb in range(nb)]
    qh = [q[bb][:, h * kd:(h + 1) * kd] for bb, h in chains]
    kh = [k[bb][:, h * kd:(h + 1) * kd] for bb, h in chains]
    vh = [v_ref[bb, :, h * vd:(h + 1) * vd] for bb, h in chains]
    s_old = [s_ref[bb, h] for bb, h in chains]
    att = [_dot(a, b, _NT) * decay[h] for a, b, (_, h) in zip(qh, kh, chains)]
    y = [_dot(a, v) + _dot(qq, s) * grow[h] for a, v, qq, s, (_, h) in zip(att, vh, qh, s_old, chains)]
    s_new = [s * math.exp(c * lgs[h]) + _dot(kk * gtail[h], v, _TN)
             for s, kk, v, (_, h) in zip(s_old, kh, vh, chains)]
    for (bb, h), yy, sn in zip(chains, y, s_new):
        s_ref[bb, h] = sn
        out = (_mul8(_group_norm_rows(yy), nw_ref[:, h * vd:(h + 1) * vd])
               * _silu(gate_ref[bb, :, h * vd:(h + 1) * vd]))
        o_ref[bb, :, h * vd:(h + 1) * vd] = out.astype(o_ref.dtype)

    @pl.when(c_idx == pl.num_programs(1) - 1)
    def _():
        s_out_ref[...] = s_ref[...]


def ret_prompt(proj, cos_f, sin_s, norm_w, chunk=128, nb=8):
    b, l, _ = proj.shape
    nb = min(nb, b)
    nc = l // chunk
    col = lambda width, off: pl.BlockSpec((nb, chunk, width), lambda i, j: (i, j, off // width))
    tab = pl.BlockSpec((chunk, RET_QK_DIM), lambda i, j: (j, 0))
    return pl.pallas_call(
        _ret_prompt_kernel,
        grid=(b // nb, nc),
        in_specs=[col(RET_QK_DIM, HYB_OFF_RQ), col(RET_QK_DIM, HYB_OFF_RK), col(RET_OUT_DIM, HYB_OFF_RV),
                  col(RET_OUT_DIM, HYB_OFF_RG), tab, tab, _resident((SUBLANES, RET_OUT_DIM))],
        out_specs=[pl.BlockSpec((nb, chunk, RET_OUT_DIM), lambda i, j: (i, j, 0)),
                   pl.BlockSpec((nb, RET_HEADS, RET_KEY_DIM, RET_VAL_DIM), lambda i, j: (i, 0, 0, 0))],
        out_shape=[jax.ShapeDtypeStruct((b, l, RET_OUT_DIM), BF16),
                   jax.ShapeDtypeStruct((b, RET_HEADS, RET_KEY_DIM, RET_VAL_DIM), F32)],
        scratch_shapes=[pltpu.VMEM((nb, RET_HEADS, RET_KEY_DIM, RET_VAL_DIM), F32)],
        compiler_params=_cparams(("parallel", "arbitrary")),
        name="ret_prompt",
    )(proj, proj, proj, proj, cos_f, sin_s, _rep8(norm_w))


def _split3(x):
    a = x.astype(BF16)
    r = x - a.astype(F32)
    b = r.astype(BF16)
    return a, b, (r - b.astype(F32)).astype(BF16)


def _spread(x, sel):
    return sum(jnp.dot(p, sel, preferred_element_type=F32) for p in _split3(x))


def _ssd_prompt_kernel(z_ref, x_ref, b_ref, c_ref, dt_ref, cw_ref, cb_ref, dtb_ref, alog_ref, dskip_ref,
                       nw_ref, sel_hd_ref, sel_c_ref, o_ref, s_out_ref, s_ref, tx_ref, tb_ref, tc_ref):
    c_idx = pl.program_id(1)

    @pl.when(c_idx == 0)
    def _():
        s_ref[...] = jnp.zeros_like(s_ref)
        tx_ref[...] = jnp.zeros_like(tx_ref)
        tb_ref[...] = jnp.zeros_like(tb_ref)
        tc_ref[...] = jnp.zeros_like(tc_ref)

    c = x_ref.shape[0]
    hd, gw, ns = SSM_HEAD_DIM, SSM_GROUP_COLS, SSM_STATE
    off_b, off_c = SSM_INNER, SSM_INNER + SSM_BC_DIM
    groups = range(SSM_GROUPS)

    br, cr = b_ref[...], c_ref[...]
    bm = _silu(_causal_conv_chunk(br, tb_ref[...], cw_ref[:, :, off_b:off_c], cb_ref[:, off_b:off_c]))
    cm = _silu(_causal_conv_chunk(cr, tc_ref[...], cw_ref[:, :, off_c:], cb_ref[:, off_c:]))
    tb_ref[...] = br[c - SUBLANES:c]
    tc_ref[...] = cr[c - SUBLANES:c]
    bg = [bm[:, g * ns:(g + 1) * ns] for g in groups]
    cg = [cm[:, g * ns:(g + 1) * ns] for g in groups]
    s_old = [s_ref[g] for g in groups]
    scores = [_dot(cc, bb, _NT) for cc, bb in zip(cg, bg)]
    y_inter = jnp.concatenate([_dot(cc, s) for cc, s in zip(cg, s_old)], axis=1)

    dt = _softplus(dt_ref[...] + dtb_ref[...])
    gcum = _cumsum_rows(dt * (-jnp.exp(alog_ref[...]) * math.log2(math.e)))
    gcum_t = _transpose(gcum)
    per_col = _spread(jnp.concatenate([dt, gcum], axis=0), sel_hd_ref[...])
    dt_x, gc_x = per_col[:c], per_col[c:]
    gc_colb = _spread(gcum, sel_c_ref[...])
    ge8 = jnp.broadcast_to(gc_x[c - 1:c, :], (SUBLANES, SSM_INNER))

    xr = x_ref[...]
    x = _silu(_causal_conv_chunk(xr, tx_ref[...], cw_ref[:, :, :off_b], cb_ref[:, :off_b]))
    tx_ref[...] = xr[c - SUBLANES:c]

    xdt = x * dt_x
    xdec = xdt * jnp.exp2(_add8(-gc_x, ge8))
    att = [scores[h // SSM_HPG] * _decay_matrix(gc_colb[:, h * c:(h + 1) * c], gcum_t[h:h + 1, :], jnp.exp2)
           for h in range(SSM_HEADS)]
    y_intra = jnp.concatenate([_dot(a, xdt[:, h * hd:(h + 1) * hd]) for h, a in enumerate(att)], axis=1)
    for g in groups:
        cols = slice(g * gw, (g + 1) * gw)
        s_ref[g] = _mul8(s_old[g], jnp.exp2(ge8[:, cols])) + _dot(bg[g], xdec[:, cols], _TN)
    y = (y_intra + y_inter * jnp.exp2(gc_x) + _mul8(x, dskip_ref[...])) * _silu(z_ref[...])
    for g in groups:
        cols = slice(g * gw, (g + 1) * gw)
        o_ref[:, cols] = _rms_rows(y[:, cols], nw_ref[:, cols]).astype(o_ref.dtype)

    @pl.when(c_idx == pl.num_programs(1) - 1)
    def _():
        for g in range(SSM_GROUPS):
            s_out_ref[g * SSM_HPG:(g + 1) * SSM_HPG] = _transpose(s_ref[g]).reshape(SSM_HPG, hd, ns)


def ssd_prompt(proj, conv_w, conv_b, dtb_row, alog_row, dskip_row, norm_w, chunk=128):
    b, l, _ = proj.shape
    assert chunk == LANES
    nc = l // chunk
    col = lambda width, off: pl.BlockSpec((None, chunk, width), lambda i, j: (i, j, off // width))
    row = _resident((1, LANES))
    lane = jnp.arange(LANES, dtype=jnp.int32)[:, None]
    sel_hd = (lane == jnp.arange(SSM_INNER, dtype=jnp.int32)[None, :] // SSM_HEAD_DIM).astype(BF16)
    sel_c = (lane == jnp.arange(SSM_HEADS * chunk, dtype=jnp.int32)[None, :] // chunk).astype(BF16)
    dskip8 = _rep8(jnp.repeat(dskip_row[0, :SSM_HEADS], SSM_HEAD_DIM))
    return pl.pallas_call(
        _ssd_prompt_kernel,
        grid=(b, nc),
        in_specs=[col(SSM_INNER, SSM_OFF_Z), col(SSM_INNER, SSM_OFF_X), col(SSM_BC_DIM, SSM_OFF_B),
                  col(SSM_BC_DIM, SSM_OFF_C), col(LANES, SSM_OFF_DT),
                  _resident((CONV_WIDTH, SUBLANES, SSM_CONV_DIM)), _resident((SUBLANES, SSM_CONV_DIM)),
                  row, row, _resident((SUBLANES, SSM_INNER)), _resident((SUBLANES, SSM_INNER)),
                  _resident(sel_hd.shape), _resident(sel_c.shape)],
        out_specs=[pl.BlockSpec((None, chunk, SSM_INNER), lambda i, j: (i, j, 0)),
                   pl.BlockSpec((None, SSM_HEADS, SSM_HEAD_DIM, SSM_STATE), lambda i, j: (i, 0, 0, 0))],
        out_shape=[jax.ShapeDtypeStruct((b, l, SSM_INNER), BF16),
                   jax.ShapeDtypeStruct((b, SSM_HEADS, SSM_HEAD_DIM, SSM_STATE), F32)],
        scratch_shapes=[pltpu.VMEM((SSM_GROUPS, SSM_STATE, SSM_GROUP_COLS), F32),
                        pltpu.VMEM((SUBLANES, SSM_INNER), F32),
                        pltpu.VMEM((SUBLANES, SSM_BC_DIM), F32), pltpu.VMEM((SUBLANES, SSM_BC_DIM), F32)],
        compiler_params=_cparams(("parallel", "arbitrary")),
        name="ssd_prompt",
    )(proj, proj, proj, proj, proj, _rep8_rows(conv_w), _rep8(conv_b), dtb_row, alog_row, dskip8,
      _rep8(norm_w), sel_hd, sel_c)


def _stacked_out(acc, n_inputs, out_index):
    return [pl.BlockSpec(memory_space=pl.ANY)], [acc], {n_inputs: out_index}


def _conv_step(x, buf_ref, w8_ref, bias8=None):
    acc = x * w8_ref[CONV_WIDTH - 1]
    for i in range(CONV_WIDTH - 1):
        acc = acc + buf_ref[i] * w8_ref[i]
    if bias8 is not None:
        acc = acc + bias8
    return acc


def _conv_state_kernel(buf_ref, x_ref, *rest):
    out_ref = rest[-1]
    for i in range(CONV_WIDTH - 2):
        out_ref[i] = buf_ref[i + 1]
    out_ref[CONV_WIDTH - 2] = x_ref[...]


def conv_state_update(conv_t, proj, col_off, layer, acc, wc):
    _, nbuf, b, n = conv_t.shape
    spec = pl.BlockSpec((None, nbuf, b, wc), lambda j: (layer, 0, 0, j))
    acc_specs, acc_args, aliases = _stacked_out(acc, 2, 0)
    return pl.pallas_call(
        _conv_state_kernel,
        grid=(n // wc,),
        in_specs=[spec, pl.BlockSpec((b, wc), lambda j: (0, col_off // wc + j))] + acc_specs,
        out_specs=spec,
        out_shape=jax.ShapeDtypeStruct(conv_t.shape, F32),
        input_output_aliases=aliases,
        compiler_params=_cparams(("parallel",)),
        name="conv_state",
    )(conv_t, proj, *acc_args)


def _gdn_decode_kernel(q_ref, k_ref, v_ref, z_ref, ba_ref, bq_ref, bk_ref, bv_ref, s_ref,
                       cwq_ref, cwk_ref, cwv_ref, alog_ref, dtb_ref, nw_ref, *rest):
    o_ref, s_out_ref = rest[-2:]
    hd = GDN_DIM
    tb = q_ref.shape[0]
    q = _silu(_conv_step(q_ref[...], bq_ref, cwq_ref))
    k = _silu(_conv_step(k_ref[...], bk_ref, cwk_ref))
    v = _silu(_conv_step(v_ref[...], bv_ref, cwv_ref))
    ba = ba_ref[...]
    beta = _sigmoid(ba)
    eg = jnp.exp(-jnp.exp(alog_ref[...]) * _softplus(ba + dtb_ref[...]))
    qn, kn = [], []
    for h in range(GDN_HEADS):
        qh, kh = q[:, h * hd:(h + 1) * hd], k[:, h * hd:(h + 1) * hd]
        qn.append(qh * lax.rsqrt(jnp.sum(qh * qh, axis=-1, keepdims=True) + NORM_EPS) * (hd ** -0.5))
        kn.append(kh * lax.rsqrt(jnp.sum(kh * kh, axis=-1, keepdims=True) + NORM_EPS))
    q_t = _transpose(jnp.concatenate(qn, axis=1))
    k_t = _transpose(jnp.concatenate(kn, axis=1))
    chains = [(b, h) for b in range(tb) for h in range(GDN_HEADS)]
    s_old = [s_ref[b, h] for b, h in chains]
    k_col = [k_t[h * hd:(h + 1) * hd, b:b + 1] for b, h in chains]
    q_col = [q_t[h * hd:(h + 1) * hd, b:b + 1] for b, h in chains]
    egb = [eg[b:b + 1, GDN_HEADS + h:GDN_HEADS + h + 1] for b, h in chains]
    ks = [jnp.sum(kc * s, axis=0, keepdims=True) for kc, s in zip(k_col, s_old)]
    delta = [beta[b:b + 1, h:h + 1] * (v[b:b + 1, h * hd:(h + 1) * hd] - e * x)
             for (b, h), e, x in zip(chains, egb, ks)]
    s_new = [e * s + kc * dl for e, s, kc, dl in zip(egb, s_old, k_col, delta)]
    for (b, h), sn in zip(chains, s_new):
        s_out_ref[b, h] = sn
    ys = [jnp.sum(qc * sn, axis=0, keepdims=True) for qc, sn in zip(q_col, s_new)]
    for h in range(GDN_HEADS):
        y = jnp.concatenate([ys[b * GDN_HEADS + h] for b in range(tb)], axis=0)
        y = _rms_rows(y, nw_ref[...]) * _silu(z_ref[:, h * hd:(h + 1) * hd])
        o_ref[:, h * hd:(h + 1) * hd] = y.astype(o_ref.dtype)


def gdn_decode(proj, state, conv_t, layer, acc, conv_w, alog_row, dtb_row, norm_w, tb=8):
    b = proj.shape[0]
    hd, od = GDN_DIM, GDN_OUT_DIM
    pcol = lambda off: pl.BlockSpec((tb, od), lambda i: (i, off // od))
    ccol = lambda part: pl.BlockSpec((None, CONV_WIDTH - 1, tb, od), lambda i: (layer, 0, i, part))
    wcol = lambda part: pl.BlockSpec((CONV_WIDTH, SUBLANES, od), lambda i: (0, 0, part))
    row = _resident((1, LANES))
    sspec = pl.BlockSpec((None, tb, GDN_HEADS, hd, hd), lambda i: (layer, i, 0, 0, 0))
    acc_specs, acc_args, aliases = _stacked_out(acc, 15, 1)
    assert tb == SUBLANES
    conv_w8 = _rep8_rows(conv_w)
    return pl.pallas_call(
        _gdn_decode_kernel,
        grid=(b // tb,),
        in_specs=[pcol(HYB_OFF_QKV), pcol(HYB_OFF_QKV + od), pcol(HYB_OFF_QKV + 2 * od),
                  pcol(HYB_OFF_Z), pl.BlockSpec((tb, LANES), lambda i: (i, HYB_OFF_BA // LANES)),
                  ccol(0), ccol(1), ccol(2), sspec,
                  wcol(0), wcol(1), wcol(2), row, row, _resident((SUBLANES, hd))] + acc_specs,
        out_specs=[pl.BlockSpec((tb, od), lambda i: (i, 0)), sspec],
        out_shape=[jax.ShapeDtypeStruct((b, od), BF16), jax.ShapeDtypeStruct(state.shape, F32)],
        input_output_aliases=aliases,
        compiler_params=_cparams(("parallel",)),
        name="gdn_decode",
    )(proj, proj, proj, proj, proj, conv_t, conv_t, conv_t, state,
      conv_w8, conv_w8, conv_w8, alog_row, dtb_row, _rep8(norm_w), *acc_args)


def _ret_decode_kernel(q_ref, k_ref, v_ref, gate_ref, s_ref, cos_ref, sin_ref, nw_ref, *rest):
    o_ref, s_out_ref = rest[-2:]
    q = _rotary(q_ref[...], cos_ref[...], sin_ref[...])
    k = _rotary(k_ref[...], cos_ref[...], sin_ref[...]) * (RET_KEY_DIM ** -0.5)
    q_t = _transpose(q)
    k_t = _transpose(k)
    tb = q.shape[0]
    kd, vd = RET_KEY_DIM, RET_VAL_DIM
    rows = []
    for b in range(tb):
        ys = []
        for h in range(RET_HEADS):
            gamma = math.exp(_ret_log_gamma(h))
            s = s_ref[b, h]
            s_new = s * gamma + k_t[h * kd:(h + 1) * kd, b:b + 1] * v_ref[b:b + 1, h * vd:(h + 1) * vd]
            s_out_ref[b, h] = s_new
            ys.append(jnp.sum(q_t[h * kd:(h + 1) * kd, b:b + 1] * s_new, axis=0, keepdims=True))
        rows.append(jnp.concatenate(ys, axis=1))
    y = jnp.concatenate(rows, axis=0)
    for h in range(RET_HEADS):
        yh = (_mul8(_group_norm_rows(y[:, h * vd:(h + 1) * vd]), nw_ref[:, h * vd:(h + 1) * vd])
              * _silu(gate_ref[:, h * vd:(h + 1) * vd]))
        o_ref[:, h * vd:(h + 1) * vd] = yh.astype(o_ref.dtype)


def ret_decode(proj, state, layer, acc, cos_f, sin_s, norm_w, tb=8):
    b = proj.shape[0]
    pcol = lambda width, off: pl.BlockSpec((tb, width), lambda i: (i, off // width))
    sspec = pl.BlockSpec((None, tb, RET_HEADS, RET_KEY_DIM, RET_VAL_DIM), lambda i: (layer, i, 0, 0, 0))
    acc_specs, acc_args, aliases = _stacked_out(acc, 8, 1)
    return pl.pallas_call(
        _ret_decode_kernel,
        grid=(b // tb,),
        in_specs=[pcol(RET_QK_DIM, HYB_OFF_RQ), pcol(RET_QK_DIM, HYB_OFF_RK), pcol(RET_OUT_DIM, HYB_OFF_RV),
                  pcol(RET_OUT_DIM, HYB_OFF_RG), sspec, _resident((1, RET_QK_DIM)), _resident((1, RET_QK_DIM)),
                  _resident((SUBLANES, RET_OUT_DIM))] + acc_specs,
        out_specs=[pl.BlockSpec((tb, RET_OUT_DIM), lambda i: (i, 0)), sspec],
        out_shape=[jax.ShapeDtypeStruct((b, RET_OUT_DIM), BF16), jax.ShapeDtypeStruct(state.shape, F32)],
        input_output_aliases=aliases,
        compiler_params=_cparams(("parallel",)),
        name="ret_decode",
    )(proj, proj, proj, proj, state, cos_f, sin_s, _rep8(norm_w), *acc_args)


def _ssd_decode_kernel(z_ref, x_ref, b_ref, c_ref, dt_ref, bx_ref, bb_ref, bc_ref, s_ref,
                       cwx_ref, cwb_ref, cwc_ref, cbx_ref, cbb_ref, cbc_ref,
                       dtb_ref, alog_ref, dskip_ref, nw_ref, *rest):
    o_ref, s_out_ref = rest[-2:]
    hd, gw, ns = SSM_HEAD_DIM, SSM_GROUP_COLS, SSM_STATE
    tb, width = x_ref.shape
    ng = width // gw
    first_head = pl.program_id(1) * (ng * SSM_HPG)
    x = _silu(_conv_step(x_ref[...], bx_ref, cwx_ref, cbx_ref[...]))
    bm = _silu(_conv_step(b_ref[...], bb_ref, cwb_ref, cbb_ref[...]))
    cm = _silu(_conv_step(c_ref[...], bc_ref, cwc_ref, cbc_ref[...]))
    dt = _softplus(dt_ref[...] + dtb_ref[...])
    spread = (_iota((LANES, width), 0) == first_head + _iota((LANES, width), 1) // hd).astype(BF16)
    per_head = jnp.concatenate([dt, -jnp.exp(alog_ref[...]), dskip_ref[...],
                                jnp.zeros((2 * SUBLANES - tb - 2, LANES), F32)], axis=0)
    ph_hi, ph_lo = _split(per_head)
    per_col = (jnp.dot(ph_hi, spread, preferred_element_type=F32)
               + jnp.dot(ph_lo, spread, preferred_element_type=F32))
    dt_x, a_x, dskip_x = per_col[0:tb], per_col[tb:tb + 1], per_col[tb + 1:tb + 2]
    eg_x = jnp.exp(dt_x * a_x)
    xdt_t = (x * dt_x).T
    row_id = _iota((tb, ns), 0)
    pairs = [(b, g) for b in range(tb) for g in range(ng)]
    outer = [_dot3(_split(xdt_t[g * gw:(g + 1) * gw, :]),
                   _split(jnp.where(row_id == b, bm[:, g * ns:(g + 1) * ns], 0.0))) for b, g in pairs]
    for (b, g), out in zip(pairs, outer):
        for hl in range(SSM_HPG):
            h = g * SSM_HPG + hl
            s_out_ref[b, h] = s_ref[b, h] * eg_x[b:b + 1, h * hd:h * hd + 1] + out[hl * hd:(hl + 1) * hd, :]
    ys = [_dot(cm[b:b + 1, g * ns:(g + 1) * ns],
               s_out_ref[b, g * SSM_HPG:(g + 1) * SSM_HPG].reshape(gw, ns), _NT) for b, g in pairs]
    y = jnp.concatenate([jnp.concatenate(ys[b * ng:(b + 1) * ng], axis=1) for b in range(tb)], axis=0)
    y = (y + dskip_x * x) * _silu(z_ref[...])
    for g in range(ng):
        cols = slice(g * gw, (g + 1) * gw)
        o_ref[:, cols] = _rms_rows(y[:, cols], nw_ref[:, cols]).astype(o_ref.dtype)


def ssd_decode(proj, state_t, conv_t, layer, acc, conv_w, conv_b, dtb_row, alog_row, dskip_row, norm_w, tb=8, ng=4):
    b = proj.shape[0]
    gw = ng * SSM_GROUP_COLS
    ns = ng * SSM_STATE
    assert tb == SUBLANES
    cw8, cb8, nw8 = _rep8_rows(conv_w), _rep8(conv_b), _rep8(norm_w)
    pcol = lambda width, off: pl.BlockSpec((tb, width), lambda i, g: (i, off // width + g))
    ccol = lambda width, off: pl.BlockSpec((None, CONV_WIDTH - 1, tb, width),
                                           lambda i, g: (layer, 0, i, off // width + g))
    wcol = lambda width, off: pl.BlockSpec((CONV_WIDTH, SUBLANES, width), lambda i, g: (0, 0, off // width + g))
    rcol = lambda width, off: pl.BlockSpec((SUBLANES, width), lambda i, g: (0, off // width + g))
    row = _resident((1, LANES))
    sspec = pl.BlockSpec((None, tb, ng * SSM_HPG, SSM_HEAD_DIM, SSM_STATE), lambda i, g: (layer, i, g, 0, 0))
    off_b, off_c = SSM_INNER, SSM_INNER + SSM_BC_DIM
    acc_specs, acc_args, aliases = _stacked_out(acc, 19, 1)
    return pl.pallas_call(
        _ssd_decode_kernel,
        grid=(b // tb, SSM_GROUPS // ng),
        in_specs=[pcol(gw, SSM_OFF_Z), pcol(gw, SSM_OFF_X), pcol(ns, SSM_OFF_B), pcol(ns, SSM_OFF_C),
                  pl.BlockSpec((tb, LANES), lambda i, g: (i, SSM_OFF_DT // LANES)),
                  ccol(gw, 0), ccol(ns, off_b), ccol(ns, off_c), sspec,
                  wcol(gw, 0), wcol(ns, off_b), wcol(ns, off_c),
                  rcol(gw, 0), rcol(ns, off_b), rcol(ns, off_c),
                  row, row, row, rcol(gw, 0)] + acc_specs,
        out_specs=[pl.BlockSpec((tb, gw), lambda i, g: (i, g)), sspec],
        out_shape=[jax.ShapeDtypeStruct((b, SSM_INNER), BF16), jax.ShapeDtypeStruct(state_t.shape, F32)],
        input_output_aliases=aliases,
        compiler_params=_cparams(("parallel", "parallel")),
        name="ssd_decode",
    )(proj, proj, proj, proj, proj, conv_t, conv_t, conv_t, state_t,
      cw8, cw8, cw8, cb8, cb8, cb8, dtb_row, alog_row, dskip_row, nw8, *acc_args)


def _lane_row(vals, offset):
    return jnp.zeros((1, LANES), F32).at[0, offset:offset + vals.shape[0]].set(vals.astype(F32))


def _prep_hyb_w_in(w):
    sizes = (GDN_QKV_DIM, GDN_OUT_DIM, GDN_HEADS, GDN_HEADS, RET_QK_DIM, RET_QK_DIM, RET_OUT_DIM, RET_OUT_DIM)
    offs = [0]
    for s in sizes:
        offs.append(offs[-1] + s)
    part = lambda i: w[..., offs[i]:offs[i + 1]]
    cols = [part(0), part(1), part(4), part(5), part(6), part(7), part(2), part(3)]
    pad = jnp.zeros(w.shape[:-1] + (HYB_N - offs[-1],), w.dtype)
    return jnp.concatenate(cols + [pad], axis=-1).astype(BF16)


def _prep_ssm_w_in(w):
    pad = jnp.zeros(w.shape[:-1] + (SSM_N - w.shape[-1],), w.dtype)
    return jnp.concatenate([w, pad], axis=-1).astype(BF16)


def _rope_tables(pos):
    half = RET_KEY_DIM // 2
    inv_freq = ROPE_BASE ** (-jnp.arange(half, dtype=F32) / half)
    ang = pos.astype(F32)[:, None] * inv_freq[None, :]
    cos, sin = jnp.cos(ang), jnp.sin(ang)
    cos_f = jnp.tile(jnp.concatenate([cos, cos], axis=1), (1, RET_HEADS))
    sin_s = jnp.tile(jnp.concatenate([-sin, sin], axis=1), (1, RET_HEADS))
    return cos_f, sin_s


def _trunk(x, pos, states, params, prompt):
    bsz, l, d = x.shape
    m = bsz * l
    tm = 512 if m % 512 == 0 else m
    x2 = x.reshape(m, d)
    cos_f, sin_s = _rope_tables(pos)
    new_gdn, new_gdn_conv, new_ret, new_ssm, new_ssm_conv = [], [], [], [], []
    if not prompt:
        s_gdn, c_gdn, s_ret, s_ssm, c_ssm = states
        c_gdn_t, c_ssm_t = jnp.swapaxes(c_gdn, 1, 2), jnp.swapaxes(c_ssm, 1, 2)
        s_ssm_t = jnp.swapaxes(s_ssm, 3, 4)
        acc_gdn, acc_gdn_conv, acc_ret, acc_ssm, acc_ssm_conv = (
            jnp.zeros_like(a) for a in (s_gdn, c_gdn_t, s_ret, s_ssm_t, c_ssm_t))
    for layer in range(DEPTH):
        i = layer // 2
        p = params
        if layer % 2 == 0:
            proj = norm_matmul(x2, p["norm_mix8"], layer, p["w_in_hyb"], i, tm)
            alog_row = _lane_row(p["gdn_a_log"][i], GDN_HEADS)
            dtb_row = _lane_row(p["gdn_dt_bias"][i], GDN_HEADS)
            if prompt:
                proj3 = proj.reshape(bsz, l, HYB_N)
                o_a, sg = gdn_prompt(proj3, p["gdn_conv_w"][i], alog_row, dtb_row, p["gdn_norm_w"][i])
                o_b, sr = ret_prompt(proj3, cos_f, sin_s, p["ret_norm_w"][i])
                cg = proj3[:, l - (CONV_WIDTH - 1):, :GDN_QKV_DIM]
                o_a, o_b = o_a.reshape(m, GDN_OUT_DIM), o_b.reshape(m, RET_OUT_DIM)
                new_gdn.append(sg)
                new_gdn_conv.append(cg)
                new_ret.append(sr)
            else:
                o_a, acc_gdn = gdn_decode(proj, s_gdn, c_gdn_t, i, acc_gdn, p["gdn_conv_w"][i], alog_row,
                                          dtb_row, p["gdn_norm_w"][i])
                acc_gdn_conv = conv_state_update(c_gdn_t, proj, HYB_OFF_QKV, i, acc_gdn_conv, GDN_OUT_DIM)
                o_b, acc_ret = ret_decode(proj, s_ret, i, acc_ret, cos_f, sin_s, p["ret_norm_w"][i])
            acts, w_out = [o_a, o_b], p["w_out_hyb"]
        else:
            proj = norm_matmul(x2, p["norm_mix8"], layer, p["w_in_ssm"], i, tm)
            dtb_row = _lane_row(p["ssm_dt_bias"][i], 0)
            alog_row = _lane_row(p["ssm_a_log"][i], 0)
            dskip_row = _lane_row(p["ssm_d"][i], 0)
            if prompt:
                proj3 = proj.reshape(bsz, l, SSM_N)
                y, ss = ssd_prompt(proj3, p["ssm_conv_w"][i], p["ssm_conv_b"][i], dtb_row, alog_row,
                                   dskip_row, p["ssm_norm_w"][i])
                cs = proj3[:, l - (CONV_WIDTH - 1):, SSM_OFF_X:SSM_OFF_X + SSM_CONV_DIM]
                y = y.reshape(m, SSM_INNER)
                new_ssm.append(ss)
                new_ssm_conv.append(cs)
            else:
                y, acc_ssm = ssd_decode(proj, s_ssm_t, c_ssm_t, i, acc_ssm, p["ssm_conv_w"][i],
                                        p["ssm_conv_b"][i], dtb_row, alog_row, dskip_row, p["ssm_norm_w"][i])
                acc_ssm_conv = conv_state_update(c_ssm_t, proj, SSM_OFF_X, i, acc_ssm_conv, D_MODEL)
            acts, w_out = [y], p["w_out_ssm"]
        x2 = post_mixer(acts, w_out, i, x2, p["norm_mlp8"], layer, p["mlp_w1"], p["mlp_w2"],
                        p["norm_final8"], layer == DEPTH - 1, 2 * tm if m % (2 * tm) == 0 else tm)
    y_out = x2.reshape(bsz, l, d)
    if prompt:
        return (y_out, jnp.stack(new_gdn), jnp.stack(new_gdn_conv), jnp.stack(new_ret),
                jnp.swapaxes(jnp.stack(new_ssm), 3, 4), jnp.stack(new_ssm_conv))
    return (y_out, acc_gdn, jnp.swapaxes(acc_gdn_conv, 1, 2), acc_ret,
            jnp.swapaxes(acc_ssm, 3, 4), jnp.swapaxes(acc_ssm_conv, 1, 2))


def kernel(x_prompt, x_sample, state_gdn, state_gdn_conv, state_ret, state_ssm, state_ssm_conv, norm_mix, norm_mlp, norm_final, w_in_hyb, gdn_conv_w, gdn_a_log, gdn_dt_bias, gdn_norm_w, ret_norm_w, w_out_hyb, w_in_ssm, ssm_conv_w, ssm_conv_b, ssm_dt_bias, ssm_a_log, ssm_d, ssm_norm_w, w_out_ssm, mlp_w1, mlp_w2):
    params = dict(
        norm_mix8=_rep8_rows(norm_mix), norm_mlp8=_rep8_rows(norm_mlp), norm_final8=_rep8(norm_final),
        w_in_hyb=_prep_hyb_w_in(w_in_hyb),
        gdn_conv_w=gdn_conv_w, gdn_a_log=gdn_a_log, gdn_dt_bias=gdn_dt_bias, gdn_norm_w=gdn_norm_w,
        ret_norm_w=ret_norm_w, w_out_hyb=w_out_hyb.astype(BF16),
        w_in_ssm=_prep_ssm_w_in(w_in_ssm),
        ssm_conv_w=ssm_conv_w, ssm_conv_b=ssm_conv_b, ssm_dt_bias=ssm_dt_bias, ssm_a_log=ssm_a_log,
        ssm_d=ssm_d, ssm_norm_w=ssm_norm_w, w_out_ssm=w_out_ssm.astype(BF16),
        mlp_w1=mlp_w1.astype(BF16), mlp_w2=mlp_w2.astype(BF16))
    lp, ls = x_prompt.shape[1], x_sample.shape[1]
    pos_prompt = jnp.arange(lp, dtype=jnp.int32)
    pos_sample = PAST_LEN + jnp.arange(ls, dtype=jnp.int32)
    y_p, p_gdn, p_gdn_conv, p_ret, p_ssm, p_ssm_conv = _trunk(
        x_prompt, pos_prompt, (None,) * 5, params, prompt=True)
    y_s, s_gdn, s_gdn_conv, s_ret, s_ssm, s_ssm_conv = _trunk(
        x_sample, pos_sample, (state_gdn, state_gdn_conv, state_ret, state_ssm, state_ssm_conv),
        params, prompt=False)
    return (y_p, y_s, p_gdn, p_gdn_conv, p_ret, p_ssm, p_ssm_conv,
            s_gdn, s_gdn_conv, s_ret, s_ssm, s_ssm_conv)
```

```python
import functools
import math

import jax
import jax.numpy as jnp
from jax import lax
from jax.experimental import pallas as pl
from jax.experimental.pallas import tpu as pltpu

F32 = jnp.float32
BF16 = jnp.bfloat16

D_MODEL = 1024
DEPTH = 4
CONV_WIDTH = 4
NORM_EPS = 1e-6
PAST_LEN = 16384

GDN_HEADS = 4
GDN_DIM = 128
GDN_QKV_DIM = 3 * GDN_HEADS * GDN_DIM
GDN_OUT_DIM = GDN_HEADS * GDN_DIM

RET_HEADS = 4
RET_KEY_DIM = 64
RET_VAL_DIM = 128
RET_QK_DIM = RET_HEADS * RET_KEY_DIM
RET_OUT_DIM = RET_HEADS * RET_VAL_DIM
ROPE_BASE = 10000.0

SSM_INNER = 2 * D_MODEL
SSM_HEAD_DIM = 64
SSM_HEADS = SSM_INNER // SSM_HEAD_DIM
SSM_GROUPS = 4
SSM_HPG = SSM_HEADS // SSM_GROUPS
SSM_STATE = 128
SSM_GROUP_COLS = SSM_HPG * SSM_HEAD_DIM
SSM_BC_DIM = SSM_GROUPS * SSM_STATE
SSM_CONV_DIM = SSM_INNER + 2 * SSM_BC_DIM
MLP_HIDDEN = 4 * D_MODEL

LANES = 128
SUBLANES = 8

HYB_OFF_QKV = 0
HYB_OFF_Z = GDN_QKV_DIM
HYB_OFF_RQ = HYB_OFF_Z + GDN_OUT_DIM
HYB_OFF_RK = HYB_OFF_RQ + RET_QK_DIM
HYB_OFF_RV = HYB_OFF_RK + RET_QK_DIM
HYB_OFF_RG = HYB_OFF_RV + RET_OUT_DIM
HYB_OFF_BA = HYB_OFF_RG + RET_OUT_DIM
HYB_N = 3840
SSM_OFF_Z = 0
SSM_OFF_X = SSM_INNER
SSM_OFF_B = SSM_OFF_X + SSM_INNER
SSM_OFF_C = SSM_OFF_B + SSM_BC_DIM
SSM_OFF_DT = SSM_OFF_C + SSM_BC_DIM
SSM_N = 5376
PROJ_TN = 768

VMEM_LIMIT = 56 * 1024 * 1024

_NN = (((1,), (0,)), ((), ()))
_NT = (((1,), (1,)), ((), ()))
_TN = (((0,), (0,)), ((), ()))


def _dot(a, b, dims=_NN):
    return lax.dot_general(a.astype(BF16), b.astype(BF16), dims, preferred_element_type=F32)


def _dot_f32(a, b, dims=_NN):
    return lax.dot_general(a, b, dims, precision=lax.Precision.HIGHEST, preferred_element_type=F32)


def _sigmoid(x):
    return 0.5 + 0.5 * jnp.tanh(0.5 * x)


def _silu(x):
    t = 0.5 * x
    return t + t * jnp.tanh(t)


def _softplus(x):
    return jnp.maximum(x, 0.0) + jnp.log(1.0 + jnp.exp(-jnp.abs(x)))


def _iota(shape, dim):
    return lax.broadcasted_iota(jnp.int32, shape, dim)


def _transpose(x):
    return x.T


def _cparams(sem):
    return pltpu.CompilerParams(dimension_semantics=sem, vmem_limit_bytes=VMEM_LIMIT)


def _resident(shape):
    nd = len(shape)
    return pl.BlockSpec(shape, lambda *_: (0,) * nd, pipeline_mode=pl.Buffered(1))


def _resident_layer(shape, layer, rows=None, row_block=0):
    rows = shape[1] if rows is None else rows
    return pl.BlockSpec((None, rows, shape[2]), lambda *_: (layer, row_block, 0), pipeline_mode=pl.Buffered(1))


def _rep8(v):
    v = v.reshape(1, -1)
    return jnp.broadcast_to(v, (SUBLANES, v.shape[1]))


def _rep8_rows(w):
    return jnp.broadcast_to(w[:, None, :], (w.shape[0], SUBLANES, w.shape[1]))


def _mul8(x, r8):
    r, n = x.shape
    return (x.reshape(r // SUBLANES, SUBLANES, n) * r8).reshape(r, n)


def _add8(x, r8):
    r, n = x.shape
    return (x.reshape(r // SUBLANES, SUBLANES, n) + r8).reshape(r, n)


def _rms_rows(x, gain8):
    return _mul8(x * lax.rsqrt(jnp.mean(x * x, axis=-1, keepdims=True) + NORM_EPS), gain8)


def _norm_matmul_kernel(x_ref, g_ref, w_ref, o_ref, *, tn):
    xn = _rms_rows(x_ref[...], g_ref[...]).astype(BF16)
    for j in range(w_ref.shape[1] // tn):
        o_ref[:, j * tn:(j + 1) * tn] = jnp.dot(xn, w_ref[:, j * tn:(j + 1) * tn],
                                                preferred_element_type=F32)


def norm_matmul(x, gains8, layer, w, w_layer, tm):
    m, d = x.shape
    n = w.shape[2]
    return pl.pallas_call(
        functools.partial(_norm_matmul_kernel, tn=PROJ_TN),
        grid=(m // tm,),
        in_specs=[pl.BlockSpec((tm, d), lambda i: (i, 0)), _resident_layer(gains8.shape, layer),
                  _resident_layer(w.shape, w_layer)],
        out_specs=pl.BlockSpec((tm, n), lambda i: (i, 0)),
        out_shape=jax.ShapeDtypeStruct((m, n), F32),
        compiler_params=_cparams(("parallel",)),
        name="norm_matmul",
    )(x, gains8, w)


def _post_mixer_kernel(*refs, n_in, tf, final_norm):
    a_refs, w_refs = refs[:n_in], refs[n_in:2 * n_in]
    r_ref, g_ref, w1_ref, w2_ref, gf_ref, o_ref = refs[2 * n_in:]
    x = r_ref[...]
    for a_ref, w_ref in zip(a_refs, w_refs):
        x = x + jnp.dot(a_ref[...], w_ref[...], preferred_element_type=F32)
    o_ref[...] = x
    x = o_ref[...]
    xn = _rms_rows(x, g_ref[...]).astype(BF16)
    acc = x
    for j in range(w1_ref.shape[1] // tf):
        h = jnp.dot(xn, w1_ref[:, j * tf:(j + 1) * tf], preferred_element_type=F32)
        h = jnp.maximum(h, 0.0)
        acc = acc + jnp.dot((h * h).astype(BF16), w2_ref[j * tf:(j + 1) * tf, :],
                            preferred_element_type=F32)
    if final_norm:
        acc = _rms_rows(acc, gf_ref[...])
    o_ref[...] = acc


def post_mixer(acts, w_out, w_layer, res, gains8, layer, w1, w2, final8, final_norm, tm):
    m, d = res.shape
    n_in = len(acts)
    k_rows = acts[0].shape[1]
    assert all(a.shape[1] == k_rows for a in acts) and n_in * k_rows == w_out.shape[1]
    in_specs = ([pl.BlockSpec((tm, k_rows), lambda i: (i, 0)) for _ in acts]
                + [_resident_layer(w_out.shape, w_layer, k_rows, j) for j in range(n_in)]
                + [pl.BlockSpec((tm, d), lambda i: (i, 0)), _resident_layer(gains8.shape, layer),
                   _resident_layer(w1.shape, layer), _resident_layer(w2.shape, layer), _resident((SUBLANES, d))])
    return pl.pallas_call(
        functools.partial(_post_mixer_kernel, n_in=n_in, tf=1024, final_norm=final_norm),
        grid=(m // tm,),
        in_specs=in_specs,
        out_specs=pl.BlockSpec((tm, d), lambda i: (i, 0)),
        out_shape=jax.ShapeDtypeStruct((m, d), F32),
        compiler_params=_cparams(("parallel",)),
        name="post_mixer",
    )(*acts, *([w_out] * n_in), res, gains8, w1, w2, final8)


def _causal_conv_chunk(x, tail, w8, bias8=None):
    c, n = x.shape
    g = c // SUBLANES
    x3 = x.reshape(g, SUBLANES, n)
    acc = x3 * w8[CONV_WIDTH - 1]
    row = _iota((SUBLANES, n), 0)
    for k in range(1, CONV_WIDTH):
        r = pltpu.roll(x3, k, 1)
        prev = jnp.concatenate([pltpu.roll(tail, k, 0)[None], r[:g - 1]], axis=0)
        acc = acc + jnp.where(row < k, prev, r) * w8[CONV_WIDTH - 1 - k]
    if bias8 is not None:
        acc = acc + bias8
    return acc.reshape(c, n)


def _cumsum_rows(g):
    c = g.shape[0]
    tri = (_iota((c, c), 0) >= _iota((c, c), 1)).astype(F32)
    return _dot_f32(tri, g)


def _decay_matrix(gc_col, gc_row, exp=jnp.exp):
    c = gc_col.shape[0]
    causal = _iota((c, c), 0) >= _iota((c, c), 1)
    return jnp.where(causal, exp(gc_col - gc_row), 0.0)


def _split(x):
    hi = x.astype(BF16)
    return hi, (x - hi.astype(F32)).astype(BF16)


def _dot3(a, b, dims=_NN):
    (ah, al), (bh, bl) = a, b
    d = lambda p, q: lax.dot_general(p, q, dims, preferred_element_type=F32)
    return d(ah, bh) + (d(ah, bl) + d(al, bh))


def _unit_lower_inverses(n_list):
    c = n_list[0].shape[0]
    ri, ci = _iota((c, c), 0), _iota((c, c), 1)
    eye = (ri == ci).astype(F32)
    blk = 16
    same = (ri // blk) == (ci // blk)
    p = [jnp.where(same, -n, 0.0).astype(BF16) for n in n_list]
    inv = [eye + x for x in p]
    for _ in range(3):
        p = [_dot(x, x).astype(BF16) for x in p]
        inv = [i + _dot(i, x) for i, x in zip(inv, p)]
    while blk < c:
        pair = ((ri // (2 * blk)) == (ci // (2 * blk))) & ((ri // blk) != (ci // blk))
        inv_b = [i.astype(BF16) for i in inv]
        t = [_dot(jnp.where(pair, n, 0.0), i) for n, i in zip(n_list, inv_b)]
        inv = [i - _dot(i_b, x) for i, i_b, x in zip(inv, inv_b, t)]
        blk *= 2
    return inv


def _gdn_prompt_kernel(qkv_ref, z_ref, ba_ref, cw_ref, alog_ref, dtb_ref, nw_ref,
                       o_ref, s_out_ref, s_ref, tail_ref):
    c_idx = pl.program_id(1)

    @pl.when(c_idx == 0)
    def _():
        s_ref[...] = jnp.zeros_like(s_ref)
        tail_ref[...] = jnp.zeros_like(tail_ref)

    nb, c = qkv_ref.shape[0], qkv_ref.shape[1]
    hd = GDN_DIM
    ri, ci = _iota((c, c), 0), _iota((c, c), 1)
    strict = ri > ci
    chains = [(bb, h) for bb in range(nb) for h in range(GDN_HEADS)]

    conv, beta, gcum, gcum_t = [], [], [], []
    for bb in range(nb):
        x = qkv_ref[bb]
        conv.append(_silu(_causal_conv_chunk(x, tail_ref[bb], cw_ref[...])))
        tail_ref[bb] = x[c - SUBLANES:c]
        ba = ba_ref[bb]
        beta.append(_sigmoid(ba))
        gc = _cumsum_rows(-jnp.exp(alog_ref[...]) * _softplus(ba + dtb_ref[...]))
        gcum.append(gc)
        gcum_t.append(_transpose(gc))

    q, k, v, b_col, gc_col, g_end, decay = [], [], [], [], [], [], []
    for bb, h in chains:
        qh = conv[bb][:, h * hd:(h + 1) * hd]
        kh = conv[bb][:, GDN_OUT_DIM + h * hd:GDN_OUT_DIM + (h + 1) * hd]
        q.append(qh * lax.rsqrt(jnp.sum(qh * qh, axis=-1, keepdims=True) + NORM_EPS) * (hd ** -0.5))
        k.append(kh * lax.rsqrt(jnp.sum(kh * kh, axis=-1, keepdims=True) + NORM_EPS))
        v.append(conv[bb][:, 2 * GDN_OUT_DIM + h * hd:2 * GDN_OUT_DIM + (h + 1) * hd])
        b_col.append(beta[bb][:, h:h + 1])
        gc_col.append(gcum[bb][:, GDN_HEADS + h:GDN_HEADS + h + 1])
        g_end.append(gcum[bb][c - 1:c, GDN_HEADS + h:GDN_HEADS + h + 1])
        decay.append(_decay_matrix(gc_col[-1], gcum_t[bb][GDN_HEADS + h:GDN_HEADS + h + 1, :]))
    kk = [_dot(x, x, _NT) for x in k]
    qk = [_dot(x, y, _NT) * d for x, y, d in zip(q, k, decay)]
    t_inv = _unit_lower_inverses([jnp.where(strict, b * x * d, 0.0) for b, x, d in zip(b_col, kk, decay)])
    rhs = [jnp.concatenate([vv * b, kx * (b * jnp.exp(g))], axis=1)
           for vv, kx, b, g in zip(v, k, b_col, gc_col)]
    sol = [_dot(t, r) for t, r in zip(t_inv, rhs)]
    s_old = [s_ref[bb, h] for bb, h in chains]
    delta = [x[:, :hd] - _dot(x[:, hd:], s) for x, s in zip(sol, s_old)]
    y = [_dot(x * jnp.exp(g), s) + _dot(a, dl)
         for x, g, s, a, dl in zip(q, gc_col, s_old, qk, delta)]
    s_new = [s * jnp.exp(ge) + _dot(kx * jnp.exp(ge - g), dl, _TN)
             for s, ge, kx, g, dl in zip(s_old, g_end, k, gc_col, delta)]
    for (bb, h), sn, yy in zip(chains, s_new, y):
        s_ref[bb, h] = sn
        out = _rms_rows(yy, nw_ref[...]) * _silu(z_ref[bb, :, h * hd:(h + 1) * hd])
        o_ref[bb, :, h * hd:(h + 1) * hd] = out.astype(o_ref.dtype)

    @pl.when(c_idx == pl.num_programs(1) - 1)
    def _():
        s_out_ref[...] = s_ref[...]


def gdn_prompt(proj, conv_w, alog_row, dtb_row, norm_w, chunk=64, nb=8):
    b, l, _ = proj.shape
    nb = min(nb, b)
    nc = l // chunk
    col = lambda width, off: pl.BlockSpec((nb, chunk, width), lambda i, j: (i, j, off // width))
    return pl.pallas_call(
        _gdn_prompt_kernel,
        grid=(b // nb, nc),
        in_specs=[col(GDN_QKV_DIM, HYB_OFF_QKV), col(GDN_OUT_DIM, HYB_OFF_Z), col(LANES, HYB_OFF_BA),
                  _resident((CONV_WIDTH, SUBLANES, GDN_QKV_DIM)), _resident((1, LANES)), _resident((1, LANES)),
                  _resident((SUBLANES, GDN_DIM))],
        out_specs=[pl.BlockSpec((nb, chunk, GDN_OUT_DIM), lambda i, j: (i, j, 0)),
                   pl.BlockSpec((nb, GDN_HEADS, GDN_DIM, GDN_DIM), lambda i, j: (i, 0, 0, 0))],
        out_shape=[jax.ShapeDtypeStruct((b, l, GDN_OUT_DIM), BF16),
                   jax.ShapeDtypeStruct((b, GDN_HEADS, GDN_DIM, GDN_DIM), F32)],
        scratch_shapes=[pltpu.VMEM((nb, GDN_HEADS, GDN_DIM, GDN_DIM), F32),
                        pltpu.VMEM((nb, SUBLANES, GDN_QKV_DIM), F32)],
        compiler_params=_cparams(("parallel", "arbitrary")),
        name="gdn_prompt",
    )(proj, proj, proj, _rep8_rows(conv_w), alog_row, dtb_row, _rep8(norm_w))


def _ret_log_gamma(h):
    return math.log(1.0 - 2.0 ** (-5.0 - h))


def _rotary(t, cos_f, sin_s):
    n = t.shape[1]
    half = RET_KEY_DIM // 2
    first = (_iota(t.shape, 1) % RET_KEY_DIM) < half
    swapped = jnp.where(first, pltpu.roll(t, n - half, 1), pltpu.roll(t, half, 1))
    return t * cos_f + swapped * sin_s


def _group_norm_rows(x):
    xc = x - jnp.mean(x, axis=-1, keepdims=True)
    return xc * lax.rsqrt(jnp.mean(xc * xc, axis=-1, keepdims=True) + NORM_EPS)


def _ret_prompt_kernel(q_ref, k_ref, v_ref, gate_ref, cos_ref, sin_ref, nw_ref,
                       o_ref, s_out_ref, s_ref):
    c_idx = pl.program_id(1)

    @pl.when(c_idx == 0)
    def _():
        s_ref[...] = jnp.zeros_like(s_ref)

    nb, c = q_ref.shape[0], q_ref.shape[1]
    kd, vd = RET_KEY_DIM, RET_VAL_DIM
    ri, ci = _iota((c, c), 0), _iota((c, c), 1)
    causal = ri >= ci
    pos_col = _iota((c, 1), 0).astype(F32)
    lgs = [_ret_log_gamma(h) for h in range(RET_HEADS)]
    decay = [jnp.where(causal, jnp.exp((ri - ci).astype(F32) * lg), 0.0) for lg in lgs]
    grow = [jnp.exp((pos_col + 1.0) * lg) for lg in lgs]
    gtail = [jnp.exp((c - 1.0 - pos_col) * lg) for lg in lgs]
    chains = [(bb, h) for bb in range(nb) for h in range(RET_HEADS)]
    q = [_rotary(q_ref[bb], cos_ref[...], sin_ref[...]) for bb in range(nb)]
    k = [_rotary(k_ref[bb], cos_ref[...], sin_ref[...]) * (kd ** -0.5) for bb in range(nb)]
    qh = [q[bb][:, h * kd:(h + 1) * kd] for bb, h in chains]
    kh = [k[bb][:, h * kd:(h + 1) * kd] for bb, h in chains]
    vh = [v_ref[bb, :, h * vd:(h + 1) * vd] for bb, h in chains]
    s_old = [s_ref[bb, h] for bb, h in chains]
    att = [_dot(a, b, _NT) * decay[h] for a, b, (_, h) in zip(qh, kh, chains)]
    y = [_dot(a, v) + _dot(qq, s) * grow[h] for a, v, qq, s, (_, h) in zip(att, vh, qh, s_old, chains)]
    s_new = [s * math.exp(c * lgs[h]) + _dot(kk * gtail[h], v, _TN)
             for s, kk, v, (_, h) in zip(s_old, kh, vh, chains)]
    for (bb, h), yy, sn in zip(chains, y, s_new):
        s_ref[bb, h] = sn
        out = (_mul8(_group_norm_rows(yy), nw_ref[:, h * vd:(h + 1) * vd])
               * _silu(gate_ref[bb, :, h * vd:(h + 1) * vd]))
        o_ref[bb, :, h * vd:(h + 1) * vd] = out.astype(o_ref.dtype)

    @pl.when(c_idx == pl.num_programs(1) - 1)
    def _():
        s_out_ref[...] = s_ref[...]


def ret_prompt(proj, cos_f, sin_s, norm_w, chunk=128, nb=8):
    b, l, _ = proj.shape
    nb = min(nb, b)
    nc = l // chunk
    col = lambda width, off: pl.BlockSpec((nb, chunk, width), lambda i, j: (i, j, off // width))
    tab = pl.BlockSpec((chunk, RET_QK_DIM), lambda i, j: (j, 0))
    return pl.pallas_call(
        _ret_prompt_kernel,
        grid=(b // nb, nc),
        in_specs=[col(RET_QK_DIM, HYB_OFF_RQ), col(RET_QK_DIM, HYB_OFF_RK), col(RET_OUT_DIM, HYB_OFF_RV),
                  col(RET_OUT_DIM, HYB_OFF_RG), tab, tab, _resident((SUBLANES, RET_OUT_DIM))],
        out_specs=[pl.BlockSpec((nb, chunk, RET_OUT_DIM), lambda i, j: (i, j, 0)),
                   pl.BlockSpec((nb, RET_HEADS, RET_KEY_DIM, RET_VAL_DIM), lambda i, j: (i, 0, 0, 0))],
        out_shape=[jax.ShapeDtypeStruct((b, l, RET_OUT_DIM), BF16),
                   jax.ShapeDtypeStruct((b, RET_HEADS, RET_KEY_DIM, RET_VAL_DIM), F32)],
        scratch_shapes=[pltpu.VMEM((nb, RET_HEADS, RET_KEY_DIM, RET_VAL_DIM), F32)],
        compiler_params=_cparams(("parallel", "arbitrary")),
        name="ret_prompt",
    )(proj, proj, proj, proj, cos_f, sin_s, _rep8(norm_w))


def _split3(x):
    a = x.astype(BF16)
    r = x - a.astype(F32)
    b = r.astype(BF16)
    return a, b, (r - b.astype(F32)).astype(BF16)


def _spread(x, sel):
    return sum(jnp.dot(p, sel, preferred_element_type=F32) for p in _split3(x))


def _ssd_prompt_kernel(z_ref, x_ref, b_ref, c_ref, dt_ref, cw_ref, cb_ref, dtb_ref, alog_ref, dskip_ref,
                       nw_ref, sel_hd_ref, sel_c_ref, o_ref, s_out_ref, s_ref, tx_ref, tb_ref, tc_ref):
    c_idx = pl.program_id(1)

    @pl.when(c_idx == 0)
    def _():
        s_ref[...] = jnp.zeros_like(s_ref)
        tx_ref[...] = jnp.zeros_like(tx_ref)
        tb_ref[...] = jnp.zeros_like(tb_ref)
        tc_ref[...] = jnp.zeros_like(tc_ref)

    c = x_ref.shape[0]
    hd, gw, ns = SSM_HEAD_DIM, SSM_GROUP_COLS, SSM_STATE
    off_b, off_c = SSM_INNER, SSM_INNER + SSM_BC_DIM
    groups = range(SSM_GROUPS)

    br, cr = b_ref[...], c_ref[...]
    bm = _silu(_causal_conv_chunk(br, tb_ref[...], cw_ref[:, :, off_b:off_c], cb_ref[:, off_b:off_c]))
    cm = _silu(_causal_conv_chunk(cr, tc_ref[...], cw_ref[:, :, off_c:], cb_ref[:, off_c:]))
    tb_ref[...] = br[c - SUBLANES:c]
    tc_ref[...] = cr[c - SUBLANES:c]
    bg = [bm[:, g * ns:(g + 1) * ns] for g in groups]
    cg = [cm[:, g * ns:(g + 1) * ns] for g in groups]
    s_old = [s_ref[g] for g in groups]
    scores = [_dot(cc, bb, _NT) for cc, bb in zip(cg, bg)]
    y_inter = jnp.concatenate([_dot(cc, s) for cc, s in zip(cg, s_old)], axis=1)

    dt = _softplus(dt_ref[...] + dtb_ref[...])
    gcum = _cumsum_rows(dt * (-jnp.exp(alog_ref[...]) * math.log2(math.e)))
    gcum_t = _transpose(gcum)
    per_col = _spread(jnp.concatenate([dt, gcum], axis=0), sel_hd_ref[...])
    dt_x, gc_x = per_col[:c], per_col[c:]
    gc_colb = _spread(gcum, sel_c_ref[...])
    ge8 = jnp.broadcast_to(gc_x[c - 1:c, :], (SUBLANES, SSM_INNER))

    xr = x_ref[...]
    x = _silu(_causal_conv_chunk(xr, tx_ref[...], cw_ref[:, :, :off_b], cb_ref[:, :off_b]))
    tx_ref[...] = xr[c - SUBLANES:c]

    xdt = x * dt_x
    xdec = xdt * jnp.exp2(_add8(-gc_x, ge8))
    att = [scores[h // SSM_HPG] * _decay_matrix(gc_colb[:, h * c:(h + 1) * c], gcum_t[h:h + 1, :], jnp.exp2)
           for h in range(SSM_HEADS)]
    y_intra = jnp.concatenate([_dot(a, xdt[:, h * hd:(h + 1) * hd]) for h, a in enumerate(att)], axis=1)
    for g in groups:
        cols = slice(g * gw, (g + 1) * gw)
        s_ref[g] = _mul8(s_old[g], jnp.exp2(ge8[:, cols])) + _dot(bg[g], xdec[:, cols], _TN)
    y = (y_intra + y_inter * jnp.exp2(gc_x) + _mul8(x, dskip_ref[...])) * _silu(z_ref[...])
    for g in groups:
        cols = slice(g * gw, (g + 1) * gw)
        o_ref[:, cols] = _rms_rows(y[:, cols], nw_ref[:, cols]).astype(o_ref.dtype)

    @pl.when(c_idx == pl.num_programs(1) - 1)
    def _():
        for g in range(SSM_GROUPS):
            s_out_ref[g * SSM_HPG:(g + 1) * SSM_HPG] = _transpose(s_ref[g]).reshape(SSM_HPG, hd, ns)


def ssd_prompt(proj, conv_w, conv_b, dtb_row, alog_row, dskip_row, norm_w, chunk=128):
    b, l, _ = proj.shape
    assert chunk == LANES
    nc = l // chunk
    col = lambda width, off: pl.BlockSpec((None, chunk, width), lambda i, j: (i, j, off // width))
    row = _resident((1, LANES))
    lane = jnp.arange(LANES, dtype=jnp.int32)[:, None]
    sel_hd = (lane == jnp.arange(SSM_INNER, dtype=jnp.int32)[None, :] // SSM_HEAD_DIM).astype(BF16)
    sel_c = (lane == jnp.arange(SSM_HEADS * chunk, dtype=jnp.int32)[None, :] // chunk).astype(BF16)
    dskip8 = _rep8(jnp.repeat(dskip_row[0, :SSM_HEADS], SSM_HEAD_DIM))
    return pl.pallas_call(
        _ssd_prompt_kernel,
        grid=(b, nc),
        in_specs=[col(SSM_INNER, SSM_OFF_Z), col(SSM_INNER, SSM_OFF_X), col(SSM_BC_DIM, SSM_OFF_B),
                  col(SSM_BC_DIM, SSM_OFF_C), col(LANES, SSM_OFF_DT),
                  _resident((CONV_WIDTH, SUBLANES, SSM_CONV_DIM)), _resident((SUBLANES, SSM_CONV_DIM)),
                  row, row, _resident((SUBLANES, SSM_INNER)), _resident((SUBLANES, SSM_INNER)),
                  _resident(sel_hd.shape), _resident(sel_c.shape)],
        out_specs=[pl.BlockSpec((None, chunk, SSM_INNER), lambda i, j: (i, j, 0)),
                   pl.BlockSpec((None, SSM_HEADS, SSM_HEAD_DIM, SSM_STATE), lambda i, j: (i, 0, 0, 0))],
        out_shape=[jax.ShapeDtypeStruct((b, l, SSM_INNER), BF16),
                   jax.ShapeDtypeStruct((b, SSM_HEADS, SSM_HEAD_DIM, SSM_STATE), F32)],
        scratch_shapes=[pltpu.VMEM((SSM_GROUPS, SSM_STATE, SSM_GROUP_COLS), F32),
                        pltpu.VMEM((SUBLANES, SSM_INNER), F32),
                        pltpu.VMEM((SUBLANES, SSM_BC_DIM), F32), pltpu.VMEM((SUBLANES, SSM_BC_DIM), F32)],
        compiler_params=_cparams(("parallel", "arbitrary")),
        name="ssd_prompt",
    )(proj, proj, proj, proj, proj, _rep8_rows(conv_w), _rep8(conv_b), dtb_row, alog_row, dskip8,
      _rep8(norm_w), sel_hd, sel_c)


def _stacked_out(acc, n_inputs, out_index):
    return [pl.BlockSpec(memory_space=pl.ANY)], [acc], {n_inputs: out_index}


def _conv_step(x, buf_ref, w8_ref, bias8=None):
    acc = x * w8_ref[CONV_WIDTH - 1]
    for i in range(CONV_WIDTH - 1):
        acc = acc + buf_ref[i] * w8_ref[i]
    if bias8 is not None:
        acc = acc + bias8
    return acc


def _conv_state_kernel(buf_ref, x_ref, *rest):
    out_ref = rest[-1]
    for i in range(CONV_WIDTH - 2):
        out_ref[i] = buf_ref[i + 1]
    out_ref[CONV_WIDTH - 2] = x_ref[...]


def conv_state_update(conv_t, proj, col_off, layer, acc, wc):
    _, nbuf, b, n = conv_t.shape
    spec = pl.BlockSpec((None, nbuf, b, wc), lambda j: (layer, 0, 0, j))
    acc_specs, acc_args, aliases = _stacked_out(acc, 2, 0)
    return pl.pallas_call(
        _conv_state_kernel,
        grid=(n // wc,),
        in_specs=[spec, pl.BlockSpec((b, wc), lambda j: (0, col_off // wc + j))] + acc_specs,
        out_specs=spec,
        out_shape=jax.ShapeDtypeStruct(conv_t.shape, F32),
        input_output_aliases=aliases,
        compiler_params=_cparams(("parallel",)),
        name="conv_state",
    )(conv_t, proj, *acc_args)


def _gdn_decode_kernel(q_ref, k_ref, v_ref, z_ref, ba_ref, bq_ref, bk_ref, bv_ref, s_ref,
                       cwq_ref, cwk_ref, cwv_ref, alog_ref, dtb_ref, nw_ref, *rest):
    o_ref, s_out_ref = rest[-2:]
    hd = GDN_DIM
    tb = q_ref.shape[0]
    q = _silu(_conv_step(q_ref[...], bq_ref, cwq_ref))
    k = _silu(_conv_step(k_ref[...], bk_ref, cwk_ref))
    v = _silu(_conv_step(v_ref[...], bv_ref, cwv_ref))
    ba = ba_ref[...]
    beta = _sigmoid(ba)
    eg = jnp.exp(-jnp.exp(alog_ref[...]) * _softplus(ba + dtb_ref[...]))
    qn, kn = [], []
    for h in range(GDN_HEADS):
        qh, kh = q[:, h * hd:(h + 1) * hd], k[:, h * hd:(h + 1) * hd]
        qn.append(qh * lax.rsqrt(jnp.sum(qh * qh, axis=-1, keepdims=True) + NORM_EPS) * (hd ** -0.5))
        kn.append(kh * lax.rsqrt(jnp.sum(kh * kh, axis=-1, keepdims=True) + NORM_EPS))
    q_t = _transpose(jnp.concatenate(qn, axis=1))
    k_t = _transpose(jnp.concatenate(kn, axis=1))
    chains = [(b, h) for b in range(tb) for h in range(GDN_HEADS)]
    s_old = [s_ref[b, h] for b, h in chains]
    k_col = [k_t[h * hd:(h + 1) * hd, b:b + 1] for b, h in chains]
    q_col = [q_t[h * hd:(h + 1) * hd, b:b + 1] for b, h in chains]
    egb = [eg[b:b + 1, GDN_HEADS + h:GDN_HEADS + h + 1] for b, h in chains]
    ks = [jnp.sum(kc * s, axis=0, keepdims=True) for kc, s in zip(k_col, s_old)]
    delta = [beta[b:b + 1, h:h + 1] * (v[b:b + 1, h * hd:(h + 1) * hd] - e * x)
             for (b, h), e, x in zip(chains, egb, ks)]
    s_new = [e * s + kc * dl for e, s, kc, dl in zip(egb, s_old, k_col, delta)]
    for (b, h), sn in zip(chains, s_new):
        s_out_ref[b, h] = sn
    ys = [jnp.sum(qc * sn, axis=0, keepdims=True) for qc, sn in zip(q_col, s_new)]
    for h in range(GDN_HEADS):
        y = jnp.concatenate([ys[b * GDN_HEADS + h] for b in range(tb)], axis=0)
        y = _rms_rows(y, nw_ref[...]) * _silu(z_ref[:, h * hd:(h + 1) * hd])
        o_ref[:, h * hd:(h + 1) * hd] = y.astype(o_ref.dtype)


def gdn_decode(proj, state, conv_t, layer, acc, conv_w, alog_row, dtb_row, norm_w, tb=8):
    b = proj.shape[0]
    hd, od = GDN_DIM, GDN_OUT_DIM
    pcol = lambda off: pl.BlockSpec((tb, od), lambda i: (i, off // od))
    ccol = lambda part: pl.BlockSpec((None, CONV_WIDTH - 1, tb, od), lambda i: (layer, 0, i, part))
    wcol = lambda part: pl.BlockSpec((CONV_WIDTH, SUBLANES, od), lambda i: (0, 0, part))
    row = _resident((1, LANES))
    sspec = pl.BlockSpec((None, tb, GDN_HEADS, hd, hd), lambda i: (layer, i, 0, 0, 0))
    acc_specs, acc_args, aliases = _stacked_out(acc, 15, 1)
    assert tb == SUBLANES
    conv_w8 = _rep8_rows(conv_w)
    return pl.pallas_call(
        _gdn_decode_kernel,
        grid=(b // tb,),
        in_specs=[pcol(HYB_OFF_QKV), pcol(HYB_OFF_QKV + od), pcol(HYB_OFF_QKV + 2 * od),
                  pcol(HYB_OFF_Z), pl.BlockSpec((tb, LANES), lambda i: (i, HYB_OFF_BA // LANES)),
                  ccol(0), ccol(1), ccol(2), sspec,
                  wcol(0), wcol(1), wcol(2), row, row, _resident((SUBLANES, hd))] + acc_specs,
        out_specs=[pl.BlockSpec((tb, od), lambda i: (i, 0)), sspec],
        out_shape=[jax.ShapeDtypeStruct((b, od), BF16), jax.ShapeDtypeStruct(state.shape, F32)],
        input_output_aliases=aliases,
        compiler_params=_cparams(("parallel",)),
        name="gdn_decode",
    )(proj, proj, proj, proj, proj, conv_t, conv_t, conv_t, state,
      conv_w8, conv_w8, conv_w8, alog_row, dtb_row, _rep8(norm_w), *acc_args)


def _ret_decode_kernel(q_ref, k_ref, v_ref, gate_ref, s_ref, cos_ref, sin_ref, nw_ref, *rest):
    o_ref, s_out_ref = rest[-2:]
    q = _rotary(q_ref[...], cos_ref[...], sin_ref[...])
    k = _rotary(k_ref[...], cos_ref[...], sin_ref[...]) * (RET_KEY_DIM ** -0.5)
    q_t = _transpose(q)
    k_t = _transpose(k)
    tb = q.shape[0]
    kd, vd = RET_KEY_DIM, RET_VAL_DIM
    rows = []
    for b in range(tb):
        ys = []
        for h in range(RET_HEADS):
            gamma = math.exp(_ret_log_gamma(h))
            s = s_ref[b, h]
            s_new = s * gamma + k_t[h * kd:(h + 1) * kd, b:b + 1] * v_ref[b:b + 1, h * vd:(h + 1) * vd]
            s_out_ref[b, h] = s_new
            ys.append(jnp.sum(q_t[h * kd:(h + 1) * kd, b:b + 1] * s_new, axis=0, keepdims=True))
        rows.append(jnp.concatenate(ys, axis=1))
    y = jnp.concatenate(rows, axis=0)
    for h in range(RET_HEADS):
        yh = (_mul8(_group_norm_rows(y[:, h * vd:(h + 1) * vd]), nw_ref[:, h * vd:(h + 1) * vd])
              * _silu(gate_ref[:, h * vd:(h + 1) * vd]))
        o_ref[:, h * vd:(h + 1) * vd] = yh.astype(o_ref.dtype)


def ret_decode(proj, state, layer, acc, cos_f, sin_s, norm_w, tb=8):
    b = proj.shape[0]
    pcol = lambda width, off: pl.BlockSpec((tb, width), lambda i: (i, off // width))
    sspec = pl.BlockSpec((None, tb, RET_HEADS, RET_KEY_DIM, RET_VAL_DIM), lambda i: (layer, i, 0, 0, 0))
    acc_specs, acc_args, aliases = _stacked_out(acc, 8, 1)
    return pl.pallas_call(
        _ret_decode_kernel,
        grid=(b // tb,),
        in_specs=[pcol(RET_QK_DIM, HYB_OFF_RQ), pcol(RET_QK_DIM, HYB_OFF_RK), pcol(RET_OUT_DIM, HYB_OFF_RV),
                  pcol(RET_OUT_DIM, HYB_OFF_RG), sspec, _resident((1, RET_QK_DIM)), _resident((1, RET_QK_DIM)),
                  _resident((SUBLANES, RET_OUT_DIM))] + acc_specs,
        out_specs=[pl.BlockSpec((tb, RET_OUT_DIM), lambda i: (i, 0)), sspec],
        out_shape=[jax.ShapeDtypeStruct((b, RET_OUT_DIM), BF16), jax.ShapeDtypeStruct(state.shape, F32)],
        input_output_aliases=aliases,
        compiler_params=_cparams(("parallel",)),
        name="ret_decode",
    )(proj, proj, proj, proj, state, cos_f, sin_s, _rep8(norm_w), *acc_args)


def _ssd_decode_kernel(z_ref, x_ref, b_ref, c_ref, dt_ref, bx_ref, bb_ref, bc_ref, s_ref,
                       cwx_ref, cwb_ref, cwc_ref, cbx_ref, cbb_ref, cbc_ref,
                       dtb_ref, alog_ref, dskip_ref, nw_ref, *rest):
    o_ref, s_out_ref = rest[-2:]
    hd, gw, ns = SSM_HEAD_DIM, SSM_GROUP_COLS, SSM_STATE
    tb, width = x_ref.shape
    ng = width // gw
    first_head = pl.program_id(1) * (ng * SSM_HPG)
    x = _silu(_conv_step(x_ref[...], bx_ref, cwx_ref, cbx_ref[...]))
    bm = _silu(_conv_step(b_ref[...], bb_ref, cwb_ref, cbb_ref[...]))
    cm = _silu(_conv_step(c_ref[...], bc_ref, cwc_ref, cbc_ref[...]))
    dt = _softplus(dt_ref[...] + dtb_ref[...])
    spread = (_iota((LANES, width), 0) == first_head + _iota((LANES, width), 1) // hd).astype(BF16)
    per_head = jnp.concatenate([dt, -jnp.exp(alog_ref[...]), dskip_ref[...],
                                jnp.zeros((2 * SUBLANES - tb - 2, LANES), F32)], axis=0)
    ph_hi, ph_lo = _split(per_head)
    per_col = (jnp.dot(ph_hi, spread, preferred_element_type=F32)
               + jnp.dot(ph_lo, spread, preferred_element_type=F32))
    dt_x, a_x, dskip_x = per_col[0:tb], per_col[tb:tb + 1], per_col[tb + 1:tb + 2]
    eg_x = jnp.exp(dt_x * a_x)
    xdt_t = (x * dt_x).T
    row_id = _iota((tb, ns), 0)
    pairs = [(b, g) for b in range(tb) for g in range(ng)]
    outer = [_dot3(_split(xdt_t[g * gw:(g + 1) * gw, :]),
                   _split(jnp.where(row_id == b, bm[:, g * ns:(g + 1) * ns], 0.0))) for b, g in pairs]
    for (b, g), out in zip(pairs, outer):
        for hl in range(SSM_HPG):
            h = g * SSM_HPG + hl
            s_out_ref[b, h] = s_ref[b, h] * eg_x[b:b + 1, h * hd:h * hd + 1] + out[hl * hd:(hl + 1) * hd, :]
    ys = [_dot(cm[b:b + 1, g * ns:(g + 1) * ns],
               s_out_ref[b, g * SSM_HPG:(g + 1) * SSM_HPG].reshape(gw, ns), _NT) for b, g in pairs]
    y = jnp.concatenate([jnp.concatenate(ys[b * ng:(b + 1) * ng], axis=1) for b in range(tb)], axis=0)
    y = (y + dskip_x * x) * _silu(z_ref[...])
    for g in range(ng):
        cols = slice(g * gw, (g + 1) * gw)
        o_ref[:, cols] = _rms_rows(y[:, cols], nw_ref[:, cols]).astype(o_ref.dtype)


def ssd_decode(proj, state_t, conv_t, layer, acc, conv_w, conv_b, dtb_row, alog_row, dskip_row, norm_w, tb=8, ng=4):
    b = proj.shape[0]
    gw = ng * SSM_GROUP_COLS
    ns = ng * SSM_STATE
    assert tb == SUBLANES
    cw8, cb8, nw8 = _rep8_rows(conv_w), _rep8(conv_b), _rep8(norm_w)
    pcol = lambda width, off: pl.BlockSpec((tb, width), lambda i, g: (i, off // width + g))
    ccol = lambda width, off: pl.BlockSpec((None, CONV_WIDTH - 1, tb, width),
                                           lambda i, g: (layer, 0, i, off // width + g))
    wcol = lambda width, off: pl.BlockSpec((CONV_WIDTH, SUBLANES, width), lambda i, g: (0, 0, off // width + g))
    rcol = lambda width, off: pl.BlockSpec((SUBLANES, width), lambda i, g: (0, off // width + g))
    row = _resident((1, LANES))
    sspec = pl.BlockSpec((None, tb, ng * SSM_HPG, SSM_HEAD_DIM, SSM_STATE), lambda i, g: (layer, i, g, 0, 0))
    off_b, off_c = SSM_INNER, SSM_INNER + SSM_BC_DIM
    acc_specs, acc_args, aliases = _stacked_out(acc, 19, 1)
    return pl.pallas_call(
        _ssd_decode_kernel,
        grid=(b // tb, SSM_GROUPS // ng),
        in_specs=[pcol(gw, SSM_OFF_Z), pcol(gw, SSM_OFF_X), pcol(ns, SSM_OFF_B), pcol(ns, SSM_OFF_C),
                  pl.BlockSpec((tb, LANES), lambda i, g: (i, SSM_OFF_DT // LANES)),
                  ccol(gw, 0), ccol(ns, off_b), ccol(ns, off_c), sspec,
                  wcol(gw, 0), wcol(ns, off_b), wcol(ns, off_c),
                  rcol(gw, 0), rcol(ns, off_b), rcol(ns, off_c),
                  row, row, row, rcol(gw, 0)] + acc_specs,
        out_specs=[pl.BlockSpec((tb, gw), lambda i, g: (i, g)), sspec],
        out_shape=[jax.ShapeDtypeStruct((b, SSM_INNER), BF16), jax.ShapeDtypeStruct(state_t.shape, F32)],
        input_output_aliases=aliases,
        compiler_params=_cparams(("parallel", "parallel")),
        name="ssd_decode",
    )(proj, proj, proj, proj, proj, conv_t, conv_t, conv_t, state_t,
      cw8, cw8, cw8, cb8, cb8, cb8, dtb_row, alog_row, dskip_row, nw8, *acc_args)


def _lane_row(vals, offset):
    return jnp.zeros((1, LANES), F32).at[0, offset:offset + vals.shape[0]].set(vals.astype(F32))


def _prep_hyb_w_in(w):
    sizes = (GDN_QKV_DIM, GDN_OUT_DIM, GDN_HEADS, GDN_HEADS, RET_QK_DIM, RET_QK_DIM, RET_OUT_DIM, RET_OUT_DIM)
    offs = [0]
    for s in sizes:
        offs.append(offs[-1] + s)
    part = lambda i: w[..., offs[i]:offs[i + 1]]
    cols = [part(0), part(1), part(4), part(5), part(6), part(7), part(2), part(3)]
    pad = jnp.zeros(w.shape[:-1] + (HYB_N - offs[-1],), w.dtype)
    return jnp.concatenate(cols + [pad], axis=-1).astype(BF16)


def _prep_ssm_w_in(w):
    pad = jnp.zeros(w.shape[:-1] + (SSM_N - w.shape[-1],), w.dtype)
    return jnp.concatenate([w, pad], axis=-1).astype(BF16)


def _rope_tables(pos):
    half = RET_KEY_DIM // 2
    inv_freq = ROPE_BASE ** (-jnp.arange(half, dtype=F32) / half)
    ang = pos.astype(F32)[:, None] * inv_freq[None, :]
    cos, sin = jnp.cos(ang), jnp.sin(ang)
    cos_f = jnp.tile(jnp.concatenate([cos, cos], axis=1), (1, RET_HEADS))
    sin_s = jnp.tile(jnp.concatenate([-sin, sin], axis=1), (1, RET_HEADS))
    return cos_f, sin_s


def _trunk(x, pos, states, params, prompt):
    bsz, l, d = x.shape
    m = bsz * l
    tm = 512 if m % 512 == 0 else m
    x2 = x.reshape(m, d)
    cos_f, sin_s = _rope_tables(pos)
    new_gdn, new_gdn_conv, new_ret, new_ssm, new_ssm_conv = [], [], [], [], []
    if not prompt:
        s_gdn, c_gdn, s_ret, s_ssm, c_ssm = states
        c_gdn_t, c_ssm_t = jnp.swapaxes(c_gdn, 1, 2), jnp.swapaxes(c_ssm, 1, 2)
        s_ssm_t = jnp.swapaxes(s_ssm, 3, 4)
        acc_gdn, acc_gdn_conv, acc_ret, acc_ssm, acc_ssm_conv = (
            jnp.zeros_like(a) for a in (s_gdn, c_gdn_t, s_ret, s_ssm_t, c_ssm_t))
    for layer in range(DEPTH):
        i = layer // 2
        p = params
        if layer % 2 == 0:
            proj = norm_matmul(x2, p["norm_mix8"], layer, p["w_in_hyb"], i, 2 * tm if m % (2 * tm) == 0 else tm)
            alog_row = _lane_row(p["gdn_a_log"][i], GDN_HEADS)
            dtb_row = _lane_row(p["gdn_dt_bias"][i], GDN_HEADS)
            if prompt:
                proj3 = proj.reshape(bsz, l, HYB_N)
                o_a, sg = gdn_prompt(proj3, p["gdn_conv_w"][i], alog_row, dtb_row, p["gdn_norm_w"][i])
                o_b, sr = ret_prompt(proj3, cos_f, sin_s, p["ret_norm_w"][i])
                cg = proj3[:, l - (CONV_WIDTH - 1):, :GDN_QKV_DIM]
                o_a, o_b = o_a.reshape(m, GDN_OUT_DIM), o_b.reshape(m, RET_OUT_DIM)
                new_gdn.append(sg)
                new_gdn_conv.append(cg)
                new_ret.append(sr)
            else:
                o_a, acc_gdn = gdn_decode(proj, s_gdn, c_gdn_t, i, acc_gdn, p["gdn_conv_w"][i], alog_row,
                                          dtb_row, p["gdn_norm_w"][i])
                acc_gdn_conv = conv_state_update(c_gdn_t, proj, HYB_OFF_QKV, i, acc_gdn_conv, GDN_OUT_DIM)
                o_b, acc_ret = ret_decode(proj, s_ret, i, acc_ret, cos_f, sin_s, p["ret_norm_w"][i])
            acts, w_out = [o_a, o_b], p["w_out_hyb"]
        else:
            proj = norm_matmul(x2, p["norm_mix8"], layer, p["w_in_ssm"], i, tm)
            dtb_row = _lane_row(p["ssm_dt_bias"][i], 0)
            alog_row = _lane_row(p["ssm_a_log"][i], 0)
            dskip_row = _lane_row(p["ssm_d"][i], 0)
            if prompt:
                proj3 = proj.reshape(bsz, l, SSM_N)
                y, ss = ssd_prompt(proj3, p["ssm_conv_w"][i], p["ssm_conv_b"][i], dtb_row, alog_row,
                                   dskip_row, p["ssm_norm_w"][i])
                cs = proj3[:, l - (CONV_WIDTH - 1):, SSM_OFF_X:SSM_OFF_X + SSM_CONV_DIM]
                y = y.reshape(m, SSM_INNER)
                new_ssm.append(ss)
                new_ssm_conv.append(cs)
            else:
                y, acc_ssm = ssd_decode(proj, s_ssm_t, c_ssm_t, i, acc_ssm, p["ssm_conv_w"][i],
                                        p["ssm_conv_b"][i], dtb_row, alog_row, dskip_row, p["ssm_norm_w"][i])
                acc_ssm_conv = conv_state_update(c_ssm_t, proj, SSM_OFF_X, i, acc_ssm_conv, D_MODEL)
            acts, w_out = [y], p["w_out_ssm"]
        x2 = post_mixer(acts, w_out, i, x2, p["norm_mlp8"], layer, p["mlp_w1"], p["mlp_w2"],
                        p["norm_final8"], layer == DEPTH - 1, 2 * tm if m % (2 * tm) == 0 else tm)
    y_out = x2.reshape(bsz, l, d)
    if prompt:
        return (y_out, jnp.stack(new_gdn), jnp.stack(new_gdn_conv), jnp.stack(new_ret),
                jnp.swapaxes(jnp.stack(new_ssm), 3, 4), jnp.stack(new_ssm_conv))
    return (y_out, acc_gdn, jnp.swapaxes(acc_gdn_conv, 1, 2), acc_ret,
            jnp.swapaxes(acc_ssm, 3, 4), jnp.swapaxes(acc_ssm_conv, 1, 2))


def kernel(x_prompt, x_sample, state_gdn, state_gdn_conv, state_ret, state_ssm, state_ssm_conv, norm_mix, norm_mlp, norm_final, w_in_hyb, gdn_conv_w, gdn_a_log, gdn_dt_bias, gdn_norm_w, ret_norm_w, w_out_hyb, w_in_ssm, ssm_conv_w, ssm_conv_b, ssm_dt_bias, ssm_a_log, ssm_d, ssm_norm_w, w_out_ssm, mlp_w1, mlp_w2):
    params = dict(
        norm_mix8=_rep8_rows(norm_mix), norm_mlp8=_rep8_rows(norm_mlp), norm_final8=_rep8(norm_final),
        w_in_hyb=_prep_hyb_w_in(w_in_hyb),
        gdn_conv_w=gdn_conv_w, gdn_a_log=gdn_a_log, gdn_dt_bias=gdn_dt_bias, gdn_norm_w=gdn_norm_w,
        ret_norm_w=ret_norm_w, w_out_hyb=w_out_hyb.astype(BF16),
        w_in_ssm=_prep_ssm_w_in(w_in_ssm),
        ssm_conv_w=ssm_conv_w, ssm_conv_b=ssm_conv_b, ssm_dt_bias=ssm_dt_bias, ssm_a_log=ssm_a_log,
        ssm_d=ssm_d, ssm_norm_w=ssm_norm_w, w_out_ssm=w_out_ssm.astype(BF16),
        mlp_w1=mlp_w1.astype(BF16), mlp_w2=mlp_w2.astype(BF16))
    lp, ls = x_prompt.shape[1], x_sample.shape[1]
    pos_prompt = jnp.arange(lp, dtype=jnp.int32)
    pos_sample = PAST_LEN + jnp.arange(ls, dtype=jnp.int32)
    y_p, p_gdn, p_gdn_conv, p_ret, p_ssm, p_ssm_conv = _trunk(
        x_prompt, pos_prompt, (None,) * 5, params, prompt=True)
    y_s, s_gdn, s_gdn_conv, s_ret, s_ssm, s_ssm_conv = _trunk(
        x_sample, pos_sample, (state_gdn, state_gdn_conv, state_ret, state_ssm, state_ssm_conv),
        params, prompt=False)
    return (y_p, y_s, p_gdn, p_gdn_conv, p_ret, p_ssm, p_ssm_conv,
            s_gdn, s_gdn_conv, s_ret, s_ssm, s_ssm_conv)
```
